```python
import jax, jax.numpy as jnp
from jax import lax
import numpy as np

D_MODEL = 1024
BATCH = 8
SEQ = 8192
DEPTH = 2

D_MIX = D_MODEL
CONV_WIDTH = D_MIX // 4
CONV_GROUPS = 4
CONV_K = 31
MLA_HEADS = 8
MLA_NOPE = 64
MLA_ROPE = 32
MLA_QK = MLA_NOPE + MLA_ROPE
MLA_V = 64
MLA_WIDTH = MLA_HEADS * MLA_V
Q_LORA = 768
KV_LORA = 256
ROPE_THETA = 10000.0
Q_BLOCK = 128
SG_WIDTH = D_MIX - CONV_WIDTH - MLA_WIDTH
SG_HEADS = 4
SG_HEAD_DIM = SG_WIDTH // SG_HEADS
SG_CHUNK = 128
IN_SIZES = (CONV_WIDTH, CONV_WIDTH, CONV_WIDTH,
            Q_LORA, KV_LORA, MLA_ROPE, MLA_WIDTH,
            SG_WIDTH, SG_WIDTH, SG_WIDTH)
IN_COLS = 3 * CONV_WIDTH + Q_LORA + KV_LORA + MLA_ROPE + MLA_WIDTH + 3 * SG_WIDTH
EPS = 1e-6

kernel_name = 'hybrid_conv_mla_sgu_parallel_heads'


def _rms_norm(x, g):
    xf = x.astype(jnp.float32)
    y = xf * lax.rsqrt(jnp.mean(xf * xf, axis=-1, keepdims=True) + EPS)
    return (y * g.astype(jnp.float32)).astype(x.dtype)


def _layer_norm(x, g, b):
    xf = x.astype(jnp.float32)
    mu = jnp.mean(xf, axis=-1, keepdims=True)
    var = jnp.mean(jnp.square(xf - mu), axis=-1, keepdims=True)
    y = (xf - mu) * lax.rsqrt(var + EPS) * g.astype(jnp.float32) + b.astype(jnp.float32)
    return y.astype(x.dtype)


def _rope_tables(seq):
    half = MLA_ROPE // 2
    inv_freq = ROPE_THETA ** (-jnp.arange(half, dtype=jnp.float32) / half)
    ang = jnp.arange(seq, dtype=jnp.float32)[:, None] * inv_freq[None, :]
    return jnp.cos(ang), jnp.sin(ang)


def _apply_rope(x, cos, sin):
    half = MLA_ROPE // 2
    c = cos[None, :, None, :].astype(x.dtype)
    s = sin[None, :, None, :].astype(x.dtype)
    x1, x2 = x[..., :half], x[..., half:]
    return jnp.concatenate([x1 * c - x2 * s, x1 * s + x2 * c], axis=-1)


def _conv_branch(a, a_glu, conv_w, conv_b, ln_g, ln_b, pw_w, pw_b):
    y = a * jax.nn.sigmoid(a_glu)
    y = lax.conv_general_dilated(
        y, conv_w[:, None, :], window_strides=(1,),
        padding=[(CONV_K - 1, 0)],
        dimension_numbers=('NWC', 'WIO', 'NWC'),
        feature_group_count=CONV_WIDTH) + conv_b
    y = jax.nn.silu(_layer_norm(y, ln_g, ln_b))
    return y @ pw_w + pw_b


def _causal_attention(q, k, v):
    B, S, H, Dq = q.shape
    nb = S // Q_BLOCK
    scale = Dq ** -0.5
    q_blocks = jnp.moveaxis(q.reshape(B, nb, Q_BLOCK, H, Dq), 1, 0)
    k_pos = jnp.arange(S)

    def block(args):
        qb, i = args
        q_pos = i * Q_BLOCK + jnp.arange(Q_BLOCK)
        s = jnp.einsum('bqhd,bkhd->bhqk', qb, k).astype(jnp.float32) * scale
        s = jnp.where(k_pos[None, :] <= q_pos[:, None], s, -jnp.inf)
        p = jax.nn.softmax(s, axis=-1).astype(v.dtype)
        return jnp.einsum('bhqk,bkhd->bqhd', p, v)

    out = lax.map(block, (q_blocks, jnp.arange(nb)))
    return jnp.moveaxis(out, 0, 1).reshape(B, S, H * v.shape[-1])


def _mla_branch(c_q, c_kv, k_rope, q_norm_g, w_uq, kv_norm_g, w_ukv,
                qk_q_g, qk_k_g, cos, sin):
    B, S, _ = c_q.shape
    q = (_rms_norm(c_q, q_norm_g) @ w_uq).reshape(B, S, MLA_HEADS, MLA_QK)
    kv = (_rms_norm(c_kv, kv_norm_g) @ w_ukv).reshape(B, S, MLA_HEADS, MLA_NOPE + MLA_V)
    k_nope, v = kv[..., :MLA_NOPE], kv[..., MLA_NOPE:]
    k_r = jnp.broadcast_to(k_rope[:, :, None, :], (B, S, MLA_HEADS, MLA_ROPE))
    k = jnp.concatenate([k_nope, k_r], axis=-1)
    q = _rms_norm(q, qk_q_g)
    k = _rms_norm(k, qk_k_g)
    q = jnp.concatenate([q[..., :MLA_NOPE], _apply_rope(q[..., MLA_NOPE:], cos, sin)], axis=-1)
    k = jnp.concatenate([k[..., :MLA_NOPE], _apply_rope(k[..., MLA_NOPE:], cos, sin)], axis=-1)
    return _causal_attention(q, k, v)


def _sgu_branch(u, v, ln_g, ln_b, sg_w, sg_b):
    B, S, _ = u.shape
    nc = S // SG_CHUNK
    u = jax.nn.gelu(u)
    v = _layer_norm(jax.nn.gelu(v), ln_g, ln_b)
    v = v.reshape(B, nc, SG_CHUNK, SG_HEADS, SG_HEAD_DIM)
    mask = jnp.tril(jnp.ones((SG_CHUNK, SG_CHUNK), dtype=bool))
    w = jnp.where(mask[None], sg_w, jnp.zeros_like(sg_w))
    mixed = jnp.einsum('gts,bcsgd->bctgd', w, v) + sg_b.T[None, None, :, :, None]
    return u * mixed.reshape(B, S, SG_WIDTH)


def _layer(x, cos, sin, norm_g, w_in, conv_w, conv_b, conv_ln_g, conv_ln_b,
           conv_pw_w, conv_pw_b, q_norm_g, w_uq, kv_norm_g, w_ukv, qk_q_g, qk_k_g,
           sg_ln_g, sg_ln_b, sg_w, sg_b, branch_norm_g, w_out):
    h = _rms_norm(x, norm_g)
    proj = h @ w_in
    idx = np.cumsum(IN_SIZES)[:-1].tolist()
    (a, a_glu, z_conv, c_q, c_kv, k_rope, z_mla,
     u_sg, v_sg, z_sg) = jnp.split(proj, idx, axis=-1)

    y_conv = _conv_branch(a, a_glu, conv_w, conv_b, conv_ln_g, conv_ln_b,
                          conv_pw_w, conv_pw_b) * jax.nn.silu(z_conv)
    y_mla = _mla_branch(c_q, c_kv, k_rope, q_norm_g, w_uq, kv_norm_g, w_ukv,
                        qk_q_g, qk_k_g, cos, sin) * jax.nn.silu(z_mla)
    y_sg = _sgu_branch(u_sg, v_sg, sg_ln_g, sg_ln_b, sg_w, sg_b) * jax.nn.silu(z_sg)

    g_conv = branch_norm_g[:CONV_WIDTH]
    g_mla = branch_norm_g[CONV_WIDTH:CONV_WIDTH + MLA_WIDTH]
    g_sg = branch_norm_g[CONV_WIDTH + MLA_WIDTH:]
    y = jnp.concatenate([_rms_norm(y_conv, g_conv),
                         _rms_norm(y_mla, g_mla),
                         _rms_norm(y_sg, g_sg)], axis=-1)
    return x + y @ w_out


def _fwd_setup_inputs(seed: int = 0) -> dict:
    key = jax.random.key(seed)
    ks = jax.random.split(key, 24)
    f32 = jnp.float32

    def nrm(k, shape, scale):
        return jax.random.normal(k, shape, f32) * scale

    def gain(k, shape):
        return 1.0 + 0.02 * jax.random.normal(k, shape, f32)

    L = DEPTH
    return {
        'x': jax.random.normal(ks[0], (BATCH, SEQ, D_MODEL), f32),
        'norm_g': gain(ks[1], (L, D_MODEL)),
        'w_in': nrm(ks[2], (L, D_MODEL, IN_COLS), D_MODEL ** -0.5),
        'conv_w': nrm(ks[3], (L, CONV_K, CONV_WIDTH), CONV_K ** -0.5),
        'conv_b': nrm(ks[4], (L, CONV_WIDTH), 0.01),
        'conv_ln_g': gain(ks[5], (L, CONV_WIDTH)),
        'conv_ln_b': nrm(ks[6], (L, CONV_WIDTH), 0.01),
        'conv_pw_w': nrm(ks[7], (L, CONV_WIDTH, CONV_WIDTH), CONV_WIDTH ** -0.5),
        'conv_pw_b': nrm(ks[8], (L, CONV_WIDTH), 0.01),
        'q_norm_g': gain(ks[9], (L, Q_LORA)),
        'w_uq': nrm(ks[10], (L, Q_LORA, MLA_HEADS * MLA_QK), Q_LORA ** -0.5),
        'kv_norm_g': gain(ks[11], (L, KV_LORA)),
        'w_ukv': nrm(ks[12], (L, KV_LORA, MLA_HEADS * (MLA_NOPE + MLA_V)), KV_LORA ** -0.5),
        'qk_q_g': gain(ks[13], (L, MLA_QK)),
        'qk_k_g': gain(ks[14], (L, MLA_QK)),
        'sg_ln_g': gain(ks[15], (L, SG_WIDTH)),
        'sg_ln_b': nrm(ks[16], (L, SG_WIDTH), 0.01),
        'sg_w': nrm(ks[17], (L, SG_HEADS, SG_CHUNK, SG_CHUNK), SG_CHUNK ** -0.5),
        'sg_b': 1.0 + nrm(ks[18], (L, SG_HEADS, SG_CHUNK), 0.1),
        'branch_norm_g': gain(ks[19], (L, D_MIX)),
        'w_out': nrm(ks[20], (L, D_MIX, D_MODEL), D_MIX ** -0.5),
    }


def _fwd_reference(x, norm_g, w_in, conv_w, conv_b, conv_ln_g, conv_ln_b, conv_pw_w,
              conv_pw_b, q_norm_g, w_uq, kv_norm_g, w_ukv, qk_q_g, qk_k_g,
              sg_ln_g, sg_ln_b, sg_w, sg_b, branch_norm_g, w_out):
    cos, sin = _rope_tables(x.shape[1])
    for l in range(DEPTH):
        x = _layer(x, cos, sin, norm_g[l], w_in[l], conv_w[l], conv_b[l],
                   conv_ln_g[l], conv_ln_b[l], conv_pw_w[l], conv_pw_b[l],
                   q_norm_g[l], w_uq[l], kv_norm_g[l], w_ukv[l], qk_q_g[l], qk_k_g[l],
                   sg_ln_g[l], sg_ln_b[l], sg_w[l], sg_b[l], branch_norm_g[l], w_out[l])
    return x


import jax as _jax
import jax.numpy as _jnp

TWIN_FORMAT = 'train_step'
FWD_PARAMS = ['x', 'norm_g', 'w_in', 'conv_w', 'conv_b', 'conv_ln_g', 'conv_ln_b', 'conv_pw_w', 'conv_pw_b', 'q_norm_g', 'w_uq', 'kv_norm_g', 'w_ukv', 'qk_q_g', 'qk_k_g', 'sg_ln_g', 'sg_ln_b', 'sg_w', 'sg_b', 'branch_norm_g', 'w_out']
TWIN_WEIGHTS = ['norm_g', 'w_in', 'conv_w', 'conv_b', 'conv_ln_g', 'conv_ln_b', 'conv_pw_w', 'conv_pw_b', 'q_norm_g', 'w_uq', 'kv_norm_g', 'w_ukv', 'qk_q_g', 'qk_k_g', 'sg_ln_g', 'sg_ln_b', 'sg_w', 'sg_b', 'branch_norm_g', 'w_out']
TWIN_DIFF_INPUT = 'x'
TWIN_INPUTS = ['x', 'norm_g', 'w_in', 'conv_w', 'conv_b', 'conv_ln_g', 'conv_ln_b', 'conv_pw_w', 'conv_pw_b', 'q_norm_g', 'w_uq', 'kv_norm_g', 'w_ukv', 'qk_q_g', 'qk_k_g', 'sg_ln_g', 'sg_ln_b', 'sg_w', 'sg_b', 'branch_norm_g', 'w_out', 'loss_target', 'm_norm_g', 'm_w_in', 'm_conv_w', 'm_conv_b', 'm_conv_ln_g', 'm_conv_ln_b', 'm_conv_pw_w', 'm_conv_pw_b', 'm_q_norm_g', 'm_w_uq', 'm_kv_norm_g', 'm_w_ukv', 'm_qk_q_g', 'm_qk_k_g', 'm_sg_ln_g', 'm_sg_ln_b', 'm_sg_w', 'm_sg_b', 'm_branch_norm_g', 'm_w_out', 'v_norm_g', 'v_w_in', 'v_conv_w', 'v_conv_b', 'v_conv_ln_g', 'v_conv_ln_b', 'v_conv_pw_w', 'v_conv_pw_b', 'v_q_norm_g', 'v_w_uq', 'v_kv_norm_g', 'v_w_ukv', 'v_qk_q_g', 'v_qk_k_g', 'v_sg_ln_g', 'v_sg_ln_b', 'v_sg_w', 'v_sg_b', 'v_branch_norm_g', 'v_w_out']
TWIN_OUTPUTS = ['loss', 'grad_x', 'grad_norm_g', 'grad_w_in', 'grad_conv_w', 'grad_conv_b', 'grad_conv_ln_g', 'grad_conv_ln_b', 'grad_conv_pw_w', 'grad_conv_pw_b', 'grad_q_norm_g', 'grad_w_uq', 'grad_kv_norm_g', 'grad_w_ukv', 'grad_qk_q_g', 'grad_qk_k_g', 'grad_sg_ln_g', 'grad_sg_ln_b', 'grad_sg_w', 'grad_sg_b', 'grad_branch_norm_g', 'grad_w_out', 'delta_norm_g', 'delta_w_in', 'delta_conv_w', 'delta_conv_b', 'delta_conv_ln_g', 'delta_conv_ln_b', 'delta_conv_pw_w', 'delta_conv_pw_b', 'delta_q_norm_g', 'delta_w_uq', 'delta_kv_norm_g', 'delta_w_ukv', 'delta_qk_q_g', 'delta_qk_k_g', 'delta_sg_ln_g', 'delta_sg_ln_b', 'delta_sg_w', 'delta_sg_b', 'delta_branch_norm_g', 'delta_w_out', 'new_m_norm_g', 'new_m_w_in', 'new_m_conv_w', 'new_m_conv_b', 'new_m_conv_ln_g', 'new_m_conv_ln_b', 'new_m_conv_pw_w', 'new_m_conv_pw_b', 'new_m_q_norm_g', 'new_m_w_uq', 'new_m_kv_norm_g', 'new_m_w_ukv', 'new_m_qk_q_g', 'new_m_qk_k_g', 'new_m_sg_ln_g', 'new_m_sg_ln_b', 'new_m_sg_w', 'new_m_sg_b', 'new_m_branch_norm_g', 'new_m_w_out', 'new_v_norm_g', 'new_v_w_in', 'new_v_conv_w', 'new_v_conv_b', 'new_v_conv_ln_g', 'new_v_conv_ln_b', 'new_v_conv_pw_w', 'new_v_conv_pw_b', 'new_v_q_norm_g', 'new_v_w_uq', 'new_v_kv_norm_g', 'new_v_w_ukv', 'new_v_qk_q_g', 'new_v_qk_k_g', 'new_v_sg_ln_g', 'new_v_sg_ln_b', 'new_v_sg_w', 'new_v_sg_b', 'new_v_branch_norm_g', 'new_v_w_out']
TWIN_LEAF_KINDS = {'loss': 'loss', 'grad_x': 'grad_x', 'grad_norm_g': 'grad_w', 'grad_w_in': 'grad_w', 'grad_conv_w': 'grad_w', 'grad_conv_b': 'grad_w', 'grad_conv_ln_g': 'grad_w', 'grad_conv_ln_b': 'grad_w', 'grad_conv_pw_w': 'grad_w', 'grad_conv_pw_b': 'grad_w', 'grad_q_norm_g': 'grad_w', 'grad_w_uq': 'grad_w', 'grad_kv_norm_g': 'grad_w', 'grad_w_ukv': 'grad_w', 'grad_qk_q_g': 'grad_w', 'grad_qk_k_g': 'grad_w', 'grad_sg_ln_g': 'grad_w', 'grad_sg_ln_b': 'grad_w', 'grad_sg_w': 'grad_w', 'grad_sg_b': 'grad_w', 'grad_branch_norm_g': 'grad_w', 'grad_w_out': 'grad_w', 'delta_norm_g': 'delta_w', 'delta_w_in': 'delta_w', 'delta_conv_w': 'delta_w', 'delta_conv_b': 'delta_w', 'delta_conv_ln_g': 'delta_w', 'delta_conv_ln_b': 'delta_w', 'delta_conv_pw_w': 'delta_w', 'delta_conv_pw_b': 'delta_w', 'delta_q_norm_g': 'delta_w', 'delta_w_uq': 'delta_w', 'delta_kv_norm_g': 'delta_w', 'delta_w_ukv': 'delta_w', 'delta_qk_q_g': 'delta_w', 'delta_qk_k_g': 'delta_w', 'delta_sg_ln_g': 'delta_w', 'delta_sg_ln_b': 'delta_w', 'delta_sg_w': 'delta_w', 'delta_sg_b': 'delta_w', 'delta_branch_norm_g': 'delta_w', 'delta_w_out': 'delta_w', 'new_m_norm_g': 'new_m', 'new_m_w_in': 'new_m', 'new_m_conv_w': 'new_m', 'new_m_conv_b': 'new_m', 'new_m_conv_ln_g': 'new_m', 'new_m_conv_ln_b': 'new_m', 'new_m_conv_pw_w': 'new_m', 'new_m_conv_pw_b': 'new_m', 'new_m_q_norm_g': 'new_m', 'new_m_w_uq': 'new_m', 'new_m_kv_norm_g': 'new_m', 'new_m_w_ukv': 'new_m', 'new_m_qk_q_g': 'new_m', 'new_m_qk_k_g': 'new_m', 'new_m_sg_ln_g': 'new_m', 'new_m_sg_ln_b': 'new_m', 'new_m_sg_w': 'new_m', 'new_m_sg_b': 'new_m', 'new_m_branch_norm_g': 'new_m', 'new_m_w_out': 'new_m', 'new_v_norm_g': 'new_v', 'new_v_w_in': 'new_v', 'new_v_conv_w': 'new_v', 'new_v_conv_b': 'new_v', 'new_v_conv_ln_g': 'new_v', 'new_v_conv_ln_b': 'new_v', 'new_v_conv_pw_w': 'new_v', 'new_v_conv_pw_b': 'new_v', 'new_v_q_norm_g': 'new_v', 'new_v_w_uq': 'new_v', 'new_v_kv_norm_g': 'new_v', 'new_v_w_ukv': 'new_v', 'new_v_qk_q_g': 'new_v', 'new_v_qk_k_g': 'new_v', 'new_v_sg_ln_g': 'new_v', 'new_v_sg_ln_b': 'new_v', 'new_v_sg_w': 'new_v', 'new_v_sg_b': 'new_v', 'new_v_branch_norm_g': 'new_v', 'new_v_w_out': 'new_v'}


def _forward(args):
    return _fwd_reference(*[args[k] for k in FWD_PARAMS])


def _output_shape():
    def fwd():
        inp = _fwd_setup_inputs(0)
        return _fwd_reference(*[inp[k] for k in FWD_PARAMS])
    out = _jax.eval_shape(fwd)
    return out.shape, out.dtype

N_MICROBATCH = 1
ADAM_LR = 0.001
ADAM_B1 = 0.9
ADAM_B2 = 0.999
ADAM_EPS = 1e-08
ADAM_WD = 0.01
ADAM_STEP = 10
PER_EXAMPLE_BATCH_AXIS = {'x': 0, 'loss_target': 0}
SHARED_INPUTS = []
_WEIGHT_DTYPES = {'norm_g': _jnp.float32, 'w_in': _jnp.float32, 'conv_w': _jnp.float32, 'conv_b': _jnp.float32, 'conv_ln_g': _jnp.float32, 'conv_ln_b': _jnp.float32, 'conv_pw_w': _jnp.float32, 'conv_pw_b': _jnp.float32, 'q_norm_g': _jnp.float32, 'w_uq': _jnp.float32, 'kv_norm_g': _jnp.float32, 'w_ukv': _jnp.float32, 'qk_q_g': _jnp.float32, 'qk_k_g': _jnp.float32, 'sg_ln_g': _jnp.float32, 'sg_ln_b': _jnp.float32, 'sg_w': _jnp.float32, 'sg_b': _jnp.float32, 'branch_norm_g': _jnp.float32, 'w_out': _jnp.float32}
MOMENT_SCALE = {'norm_g': 1.524336e+00, 'w_in': 8.610377e-01, 'conv_w': 7.826627e-01, 'conv_b': 1.003037e+01, 'conv_ln_g': 3.463237e+00, 'conv_ln_b': 6.065275e+00, 'conv_pw_w': 1.849813e+00, 'conv_pw_b': 1.184538e+01, 'q_norm_g': 5.064986e-01, 'w_uq': 5.302392e-01, 'kv_norm_g': 3.077268e+00, 'w_ukv': 1.304614e+00, 'qk_q_g': 1.412316e+00, 'qk_k_g': 1.387820e+00, 'sg_ln_g': 5.347480e-01, 'sg_ln_b': 3.730734e-01, 'sg_w': 2.602580e-01, 'sg_b': 3.731286e-01, 'branch_norm_g': 6.585814e+01, 'w_out': 2.460789e+00}


def _to_microbatches(a, axis):
    t = _jnp.moveaxis(a, axis, 0)
    t = t.reshape((N_MICROBATCH, t.shape[0] // N_MICROBATCH) + t.shape[1:])
    return _jnp.moveaxis(t, 1, axis + 1)


def setup_inputs(seed: int = 0) -> dict:
    inp = _fwd_setup_inputs(seed)
    key = _jax.random.fold_in(_jax.random.key(seed), 7919)
    shape, _ = _output_shape()
    out = dict(inp)
    out["loss_target"] = _jax.random.normal(_jax.random.fold_in(key, 0), shape, _jnp.float32)
    for i, name in enumerate(TWIN_WEIGHTS):
        w = inp[name].astype(_jnp.float32)
        if MOMENT_SCALE is None:
            s = _jnp.sqrt(_jnp.mean(_jnp.square(w)) + 1e-30)
        else:
            s = MOMENT_SCALE[name]
        km, kv = _jax.random.split(_jax.random.fold_in(key, i + 1))
        out[name] = w
        out["m_" + name] = s * _jax.random.normal(km, w.shape, _jnp.float32)
        out["v_" + name] = (s * s) * _jax.random.uniform(kv, w.shape, _jnp.float32, 0.5, 1.5)
    if N_MICROBATCH > 1:
        for name, axis in PER_EXAMPLE_BATCH_AXIS.items():
            out[name] = _to_microbatches(out[name], axis)
    return {'x': out['x'], 'norm_g': out['norm_g'], 'w_in': out['w_in'], 'conv_w': out['conv_w'], 'conv_b': out['conv_b'], 'conv_ln_g': out['conv_ln_g'], 'conv_ln_b': out['conv_ln_b'], 'conv_pw_w': out['conv_pw_w'], 'conv_pw_b': out['conv_pw_b'], 'q_norm_g': out['q_norm_g'], 'w_uq': out['w_uq'], 'kv_norm_g': out['kv_norm_g'], 'w_ukv': out['w_ukv'], 'qk_q_g': out['qk_q_g'], 'qk_k_g': out['qk_k_g'], 'sg_ln_g': out['sg_ln_g'], 'sg_ln_b': out['sg_ln_b'], 'sg_w': out['sg_w'], 'sg_b': out['sg_b'], 'branch_norm_g': out['branch_norm_g'], 'w_out': out['w_out'], 'loss_target': out['loss_target'], 'm_norm_g': out['m_norm_g'], 'm_w_in': out['m_w_in'], 'm_conv_w': out['m_conv_w'], 'm_conv_b': out['m_conv_b'], 'm_conv_ln_g': out['m_conv_ln_g'], 'm_conv_ln_b': out['m_conv_ln_b'], 'm_conv_pw_w': out['m_conv_pw_w'], 'm_conv_pw_b': out['m_conv_pw_b'], 'm_q_norm_g': out['m_q_norm_g'], 'm_w_uq': out['m_w_uq'], 'm_kv_norm_g': out['m_kv_norm_g'], 'm_w_ukv': out['m_w_ukv'], 'm_qk_q_g': out['m_qk_q_g'], 'm_qk_k_g': out['m_qk_k_g'], 'm_sg_ln_g': out['m_sg_ln_g'], 'm_sg_ln_b': out['m_sg_ln_b'], 'm_sg_w': out['m_sg_w'], 'm_sg_b': out['m_sg_b'], 'm_branch_norm_g': out['m_branch_norm_g'], 'm_w_out': out['m_w_out'], 'v_norm_g': out['v_norm_g'], 'v_w_in': out['v_w_in'], 'v_conv_w': out['v_conv_w'], 'v_conv_b': out['v_conv_b'], 'v_conv_ln_g': out['v_conv_ln_g'], 'v_conv_ln_b': out['v_conv_ln_b'], 'v_conv_pw_w': out['v_conv_pw_w'], 'v_conv_pw_b': out['v_conv_pw_b'], 'v_q_norm_g': out['v_q_norm_g'], 'v_w_uq': out['v_w_uq'], 'v_kv_norm_g': out['v_kv_norm_g'], 'v_w_ukv': out['v_w_ukv'], 'v_qk_q_g': out['v_qk_q_g'], 'v_qk_k_g': out['v_qk_k_g'], 'v_sg_ln_g': out['v_sg_ln_g'], 'v_sg_ln_b': out['v_sg_ln_b'], 'v_sg_w': out['v_sg_w'], 'v_sg_b': out['v_sg_b'], 'v_branch_norm_g': out['v_branch_norm_g'], 'v_w_out': out['v_w_out']}


def _loss(weights, diff, rest, loss_target):
    with _jax.named_scope("forward"):
        args = {**rest, TWIN_DIFF_INPUT: diff, **{k: w.astype(_WEIGHT_DTYPES[k]) for k, w in weights.items()}}
        y = _forward(args)
    with _jax.named_scope("loss_head"):
        err = _jnp.square(y.astype(_jnp.float32) - loss_target)
        return 0.5 * _jnp.sum(_jnp.mean(err, axis=-1)) if err.ndim else 0.5 * err


def _adamw(w, g, m, v):
    m = ADAM_B1 * m + (1.0 - ADAM_B1) * g
    v = ADAM_B2 * v + (1.0 - ADAM_B2) * _jnp.square(g)
    m_hat = m / (1.0 - ADAM_B1 ** ADAM_STEP)
    v_hat = v / (1.0 - ADAM_B2 ** ADAM_STEP)
    delta = -ADAM_LR * (m_hat / (_jnp.sqrt(v_hat) + ADAM_EPS) + ADAM_WD * w)
    return delta, m, v


def reference(x, norm_g, w_in, conv_w, conv_b, conv_ln_g, conv_ln_b, conv_pw_w, conv_pw_b, q_norm_g, w_uq, kv_norm_g, w_ukv, qk_q_g, qk_k_g, sg_ln_g, sg_ln_b, sg_w, sg_b, branch_norm_g, w_out, loss_target, m_norm_g, m_w_in, m_conv_w, m_conv_b, m_conv_ln_g, m_conv_ln_b, m_conv_pw_w, m_conv_pw_b, m_q_norm_g, m_w_uq, m_kv_norm_g, m_w_ukv, m_qk_q_g, m_qk_k_g, m_sg_ln_g, m_sg_ln_b, m_sg_w, m_sg_b, m_branch_norm_g, m_w_out, v_norm_g, v_w_in, v_conv_w, v_conv_b, v_conv_ln_g, v_conv_ln_b, v_conv_pw_w, v_conv_pw_b, v_q_norm_g, v_w_uq, v_kv_norm_g, v_w_ukv, v_qk_q_g, v_qk_k_g, v_sg_ln_g, v_sg_ln_b, v_sg_w, v_sg_b, v_branch_norm_g, v_w_out):
    given = dict(x=x, norm_g=norm_g, w_in=w_in, conv_w=conv_w, conv_b=conv_b, conv_ln_g=conv_ln_g, conv_ln_b=conv_ln_b, conv_pw_w=conv_pw_w, conv_pw_b=conv_pw_b, q_norm_g=q_norm_g, w_uq=w_uq, kv_norm_g=kv_norm_g, w_ukv=w_ukv, qk_q_g=qk_q_g, qk_k_g=qk_k_g, sg_ln_g=sg_ln_g, sg_ln_b=sg_ln_b, sg_w=sg_w, sg_b=sg_b, branch_norm_g=branch_norm_g, w_out=w_out, loss_target=loss_target, m_norm_g=m_norm_g, m_w_in=m_w_in, m_conv_w=m_conv_w, m_conv_b=m_conv_b, m_conv_ln_g=m_conv_ln_g, m_conv_ln_b=m_conv_ln_b, m_conv_pw_w=m_conv_pw_w, m_conv_pw_b=m_conv_pw_b, m_q_norm_g=m_q_norm_g, m_w_uq=m_w_uq, m_kv_norm_g=m_kv_norm_g, m_w_ukv=m_w_ukv, m_qk_q_g=m_qk_q_g, m_qk_k_g=m_qk_k_g, m_sg_ln_g=m_sg_ln_g, m_sg_ln_b=m_sg_ln_b, m_sg_w=m_sg_w, m_sg_b=m_sg_b, m_branch_norm_g=m_branch_norm_g, m_w_out=m_w_out, v_norm_g=v_norm_g, v_w_in=v_w_in, v_conv_w=v_conv_w, v_conv_b=v_conv_b, v_conv_ln_g=v_conv_ln_g, v_conv_ln_b=v_conv_ln_b, v_conv_pw_w=v_conv_pw_w, v_conv_pw_b=v_conv_pw_b, v_q_norm_g=v_q_norm_g, v_w_uq=v_w_uq, v_kv_norm_g=v_kv_norm_g, v_w_ukv=v_w_ukv, v_qk_q_g=v_qk_q_g, v_qk_k_g=v_qk_k_g, v_sg_ln_g=v_sg_ln_g, v_sg_ln_b=v_sg_ln_b, v_sg_w=v_sg_w, v_sg_b=v_sg_b, v_branch_norm_g=v_branch_norm_g, v_w_out=v_w_out)
    weights = {n: given[n] for n in TWIN_WEIGHTS}
    shared = {n: given[n] for n in SHARED_INPUTS}
    per_example = {n: given[n] for n in ['x']}
    grad_fn = _jax.value_and_grad(_loss, argnums=(0, 1))

    def one_microbatch(ex, loss_target):
        ex = dict(ex)
        diff = ex.pop(TWIN_DIFF_INPUT)
        return grad_fn(weights, diff, {**shared, **ex}, loss_target)

    if N_MICROBATCH == 1:
        loss, (grad_w, grad_x) = one_microbatch(per_example, given["loss_target"])
    else:
        def body(carry, xs):
            loss_sum, grad_sum = carry
            l_k, (gw_k, gx_k) = one_microbatch(xs[0], xs[1])
            with _jax.named_scope("update"):
                return (loss_sum + l_k, _jax.tree.map(_jnp.add, grad_sum, gw_k)), gx_k

        init = (_jnp.zeros((), _jnp.float32), _jax.tree.map(_jnp.zeros_like, weights))
        (loss, grad_w), grad_x = _jax.lax.scan(body, init, (per_example, given["loss_target"]))
    with _jax.named_scope("update"):
        delta_w, new_m, new_v = {}, {}, {}
        for n in TWIN_WEIGHTS:
            delta_w[n], new_m[n], new_v[n] = _adamw(weights[n], grad_w[n], given["m_" + n], given["v_" + n])
    return (loss, grad_x, *[grad_w[n] for n in TWIN_WEIGHTS], *[delta_w[n] for n in TWIN_WEIGHTS],
            *[new_m[n] for n in TWIN_WEIGHTS], *[new_v[n] for n in TWIN_WEIGHTS])
```

```python
import functools

import jax
import jax.numpy as jnp
import numpy as np
from jax import lax
from jax.experimental import pallas as pl
from jax.experimental.pallas import tpu as pltpu

f32 = jnp.float32
bf16 = jnp.bfloat16

DEPTH = 2
D_MODEL = 1024
CONV_W = 256
CONV_K = 31
HEADS = 8
NOPE = 64
ROPE = 32
QK = NOPE + ROPE
VDIM = 64
MLA_W = HEADS * VDIM
Q_LORA = 768
KV_LORA = 256
SG_W = 256
SG_HEADS = 4
SG_CHUNK = 128
IN_COLS = 3104
ROPE_THETA = 10000.0
EPS = 1e-6
HEAD_PAD = 128
HALO = 32
LANES = 1024
ROW_ALIGN = 256

ADAM_LR = 0.001
ADAM_B1 = 0.9
ADAM_B2 = 0.999
ADAM_EPS = 1e-08
ADAM_WD = 0.01
ADAM_STEP = 10

PROJ_PAD = 3200
SEG = {
    "zm": (0, 512), "a": (512, 768), "glu": (768, 1024), "zc": (1024, 1280), "cq": (1280, 2048),
    "ckv": (2048, 2304), "kr": (2304, 2432), "us": (2432, 2688), "vs": (2688, 2944), "zs": (2944, 3200),
}
REST = PROJ_PAD - 512

WEIGHTS = ["norm_g", "w_in", "conv_w", "conv_b", "conv_ln_g", "conv_ln_b", "conv_pw_w", "conv_pw_b", "q_norm_g",
           "w_uq", "kv_norm_g", "w_ukv", "qk_q_g", "qk_k_g", "sg_ln_g", "sg_ln_b", "sg_w", "sg_b",
           "branch_norm_g", "w_out"]
BIG = {"w_in": 2, "conv_w": 2, "conv_pw_w": 1, "w_uq": 1, "w_ukv": 2, "w_out": 1}
SMALL = [n for n in WEIGHTS if n not in BIG]

BR_W = [("conv_w", (32, 256)), ("conv_b", (1, 256)), ("cln_g", (1, 256)), ("cln_b", (1, 256)),
        ("pw_w", (256, 256)), ("pw_b", (1, 256)), ("qn_g", (1, 768)), ("w_uq", (768, 1024)),
        ("kvn_g", (1, 256)), ("w_k", (256, 1024)), ("w_v", (256, 1024)), ("qkq_g", (1, 128)),
        ("qkk_g", (1, 128)), ("sln_g", (1, 256)), ("sln_b", (1, 256)), ("sg_w", (4, 128, 128)),
        ("sg_b", (4, 128, 1)), ("bn_c", (1, 256)), ("bn_s", (1, 256))]


@jax.custom_vjp
def _mm(a, w):
    return jnp.dot(a.astype(bf16), w.astype(bf16), preferred_element_type=f32)


def _mm_fwd(a, w):
    return _mm(a, w), (a, w)


def _mm_bwd(res, ct):
    a, w = res
    ctb = ct.astype(bf16)
    da = lax.dot_general(ctb, w.astype(bf16), (((1,), (1,)), ((), ())), preferred_element_type=f32)
    dw = lax.dot_general(a.astype(bf16), ctb, (((0,), (0,)), ((), ())), preferred_element_type=f32)
    return da.astype(a.dtype), dw.astype(w.dtype)


_mm.defvjp(_mm_fwd, _mm_bwd)


@jax.custom_vjp
def _rope(x, c, s1, s2):
    return x * c + pltpu.roll(x, HEAD_PAD - 16, 1) * s1 + pltpu.roll(x, 16, 1) * s2


def _rope_fwd(x, c, s1, s2):
    return _rope(x, c, s1, s2), (c, s1, s2)


def _rope_bwd(res, ct):
    c, s1, s2 = res
    dx = ct * c + pltpu.roll(ct * s1, 16, 1) + pltpu.roll(ct * s2, HEAD_PAD - 16, 1)
    return dx, jnp.zeros_like(c), jnp.zeros_like(s1), jnp.zeros_like(s2)


_rope.defvjp(_rope_fwd, _rope_bwd)


def _rms(x, g):
    return x * lax.rsqrt(jnp.mean(x * x, axis=-1, keepdims=True) + EPS) * g


def _head_rms(x, g):
    return x * lax.rsqrt(jnp.sum(x * x, axis=-1, keepdims=True) * (1.0 / QK) + EPS) * g


def _ln(x, g, b):
    mu = jnp.mean(x, axis=-1, keepdims=True)
    xc = x - mu
    var = jnp.mean(xc * xc, axis=-1, keepdims=True)
    return xc * lax.rsqrt(var + EPS) * g + b


def _branch(p, w, c):
    tm = p["a"].shape[0]
    yg = p["a"] * jax.nn.sigmoid(p["glu"])
    yh = p["ha"] * jax.nn.sigmoid(p["hglu"]) * c["hmask"]
    ycat = jnp.concatenate([yh, yg], axis=0)
    acc = jnp.zeros((tm, CONV_W), f32)
    for k in range(CONV_K):
        off = HALO - (CONV_K - 1) + k
        acc = acc + ycat[off:off + tm, :] * w["conv_w"][k:k + 1, :]
    yl = jax.nn.silu(_ln(acc + w["conv_b"], w["cln_g"], w["cln_b"]))
    ypw = _mm(yl, w["pw_w"]) + w["pw_b"]
    ync = _rms(ypw * jax.nn.silu(p["zc"]), w["bn_c"])
    cqn = _rms(p["cq"], w["qn_g"])
    qf = _mm(cqn, w["w_uq"])
    ckvn = _rms(p["ckv"], w["kvn_g"])
    kf = _mm(ckvn, w["w_k"])
    v = _mm(ckvn, w["w_v"])
    qs, ks = [], []
    for h in range(HEADS):
        sl = slice(h * HEAD_PAD, (h + 1) * HEAD_PAD)
        qh = _head_rms(qf[:, sl], w["qkq_g"])
        qs.append(_rope(qh, c["rc"], c["rs1"], c["rs2"]) * (QK ** -0.5))
        kh = _head_rms(kf[:, sl] + p["kr"], w["qkk_g"])
        ks.append(_rope(kh, c["rc"], c["rs1"], c["rs2"]))
    q = jnp.concatenate(qs, axis=-1)
    k = jnp.concatenate(ks, axis=-1)
    u = jax.nn.gelu(p["us"])
    v2 = _ln(jax.nn.gelu(p["vs"]), w["sln_g"], w["sln_b"])
    r = lax.broadcasted_iota(jnp.int32, (SG_CHUNK, SG_CHUNK), 0)
    cc = lax.broadcasted_iota(jnp.int32, (SG_CHUNK, SG_CHUNK), 1)
    lane = lax.broadcasted_iota(jnp.int32, (1, SG_W), 1) // (SG_W // SG_HEADS)
    wm = [jnp.where(r >= cc, w["sg_w"][g], 0.0) for g in range(SG_HEADS)]
    rows = []
    for ci in range(tm // SG_CHUNK):
        vc = v2[ci * SG_CHUNK:(ci + 1) * SG_CHUNK, :]
        mixed = jnp.zeros((SG_CHUNK, SG_W), f32)
        for g in range(SG_HEADS):
            mixed = mixed + jnp.where(lane == g, _mm(wm[g], vc) + w["sg_b"][g], 0.0)
        rows.append(mixed)
    mixed = jnp.concatenate(rows, axis=0) if len(rows) > 1 else rows[0]
    yns = _rms(u * mixed * jax.nn.silu(p["zs"]), w["bn_s"])
    return q, k, v, ync, yns


def _mla_out(o, zm, g):
    return _rms(o * jax.nn.silu(zm), g)


def _load_branch_inputs(proj_ref, halo_ref):
    p = {n: proj_ref[:, SEG[n][0]:SEG[n][1]].astype(f32) for n in SEG if n != "zm"}
    p["ha"] = halo_ref[:, 0:CONV_W].astype(f32)
    p["hglu"] = halo_ref[:, CONV_W:2 * CONV_W].astype(f32)
    return p


def _load_branch_weights(refs):
    w = {}
    for (n, _), r in zip(BR_W, refs):
        if n in ("sg_w", "sg_b"):
            w[n] = [r[g] for g in range(SG_HEADS)]
        else:
            w[n] = r[...]
    return w


def _const_spec(shape):
    nd = len(shape)
    return pl.BlockSpec(shape, lambda *_: (0,) * nd)


def _inproj_fwd(x, g, w, tm, name):
    T = x.shape[0]
    nc = 5
    cw = PROJ_PAD // nc

    def body(x_ref, g_ref, w_ref, o_ref):
        h = _rms(x_ref[...], g_ref[...]).astype(bf16)
        for j in range(nc):
            o_ref[:, j * cw:(j + 1) * cw] = jnp.dot(
                h, w_ref[:, j * cw:(j + 1) * cw], preferred_element_type=f32).astype(bf16)

    return pl.pallas_call(
        body, name=name, grid=(T // tm,),
        in_specs=[pl.BlockSpec((tm, D_MODEL), lambda i: (i, 0)), _const_spec((1, D_MODEL)),
                  _const_spec((D_MODEL, PROJ_PAD))],
        out_specs=pl.BlockSpec((tm, PROJ_PAD), lambda i: (i, 0)),
        out_shape=jax.ShapeDtypeStruct((T, PROJ_PAD), bf16),
    )(x, g, w)


def _branch_fwd(proj, tabs, bw, tm, name):
    T = proj.shape[0]
    hb = tm // HALO

    def body(proj_ref, halo_ref, rc_ref, rs1_ref, rs2_ref, *rest):
        wrefs, (q_ref, k_ref, v_ref, ync_ref, yns_ref) = rest[:len(BR_W)], rest[len(BR_W):]
        i = pl.program_id(0)
        c = {"hmask": (i > 0).astype(f32), "rc": rc_ref[...], "rs1": rs1_ref[...], "rs2": rs2_ref[...]}
        q, k, v, ync, yns = _branch(_load_branch_inputs(proj_ref, halo_ref), _load_branch_weights(wrefs), c)
        q_ref[...] = q.astype(bf16)
        k_ref[...] = k.astype(bf16)
        v_ref[...] = v.astype(bf16)
        ync_ref[...] = ync.astype(bf16)
        yns_ref[...] = yns.astype(bf16)

    row = lambda wd: pl.BlockSpec((tm, wd), lambda i: (i, 0))
    wide = HEADS * HEAD_PAD
    return pl.pallas_call(
        body, name=name, grid=(T // tm,),
        in_specs=[row(PROJ_PAD), pl.BlockSpec((HALO, 2 * CONV_W), lambda i: (jnp.maximum(i * hb - 1, 0), 1)),
                  row(HEAD_PAD), row(HEAD_PAD), row(HEAD_PAD)] + [_const_spec(s) for _, s in BR_W],
        out_specs=[row(wide), row(wide), row(wide), row(CONV_W), row(SG_W)],
        out_shape=[jax.ShapeDtypeStruct((T, wide), bf16)] * 3 + [jax.ShapeDtypeStruct((T, CONV_W), bf16)] * 2,
    )(proj, proj, *tabs, *[bw[n] for n, _ in BR_W])


def _attn_fwd(q, k, v, tb, name):
    T = q.shape[0]
    nb = T // tb
    pw = 2 * HEAD_PAD

    def body(q_ref, k_ref, v_ref, o_ref, lse_ref):
        qi = pl.program_id(1)
        krow = lax.broadcasted_iota(jnp.int32, (tb, tb), 0)
        qcol = lax.broadcasted_iota(jnp.int32, (tb, tb), 1)
        o_t = jnp.zeros((HEAD_PAD, tb), f32)
        for e in range(2):
            hs = slice(e * HEAD_PAD, (e + 1) * HEAD_PAD)
            qh = q_ref[:, hs]

            def step(off, carry, masked, hs=hs, qh=qh):
                m, l, acc = carry
                kb = k_ref[pl.ds(off, tb), hs]
                vb = v_ref[pl.ds(off, tb), hs]
                s_t = lax.dot_general(kb, qh, (((1,), (1,)), ((), ())), preferred_element_type=f32)
                if masked:
                    s_t = jnp.where(krow <= qcol, s_t, -1e30)
                m_new = jnp.maximum(m, jnp.max(s_t, axis=0, keepdims=True))
                p_t = jnp.exp(s_t - m_new)
                alpha = jnp.exp(m - m_new)
                l = alpha * l + jnp.sum(p_t, axis=0, keepdims=True)
                acc = alpha * acc + lax.dot_general(vb, p_t.astype(bf16), (((0,), (0,)), ((), ())),
                                                    preferred_element_type=f32)
                return m_new, l, acc

            init = (jnp.full((1, tb), -1e30, f32), jnp.zeros((1, tb), f32), jnp.zeros((HEAD_PAD, tb), f32))
            carry = lax.fori_loop(0, qi, lambda ki, cr: step(pl.multiple_of(ki * tb, tb), cr, False), init)
            m, l, acc = step(pl.multiple_of(qi * tb, tb), carry, True)
            o_t = o_t + acc / l
            lse_ref[e] = m + jnp.log(l)
        o_ref[...] = o_t.T

    return pl.pallas_call(
        body, name=name, grid=(HEADS // 2, nb),
        in_specs=[pl.BlockSpec((tb, pw), lambda h, i: (i, h)), pl.BlockSpec((T, pw), lambda h, i: (0, h)),
                  pl.BlockSpec((T, pw), lambda h, i: (0, h))],
        out_specs=[pl.BlockSpec((tb, HEAD_PAD), lambda h, i: (i, h)), pl.BlockSpec((2, 1, tb), lambda h, i: (h, 0, i))],
        out_shape=[jax.ShapeDtypeStruct((T, MLA_W), f32), jax.ShapeDtypeStruct((HEADS, 1, T), f32)],
    )(q, k, v)


def _out_fwd(x, o, proj, ync, yns, g_m, w_out, tm, name):
    T = x.shape[0]

    def body(x_ref, o_ref, zm_ref, ync_ref, yns_ref, g_ref, w_ref, y_ref):
        ynm = _mla_out(o_ref[...], zm_ref[...].astype(f32), g_ref[...]).astype(bf16)
        y = x_ref[...]
        y = y + jnp.dot(ync_ref[...], w_ref[0:CONV_W, :], preferred_element_type=f32)
        y = y + jnp.dot(ynm, w_ref[CONV_W:CONV_W + MLA_W, :], preferred_element_type=f32)
        y = y + jnp.dot(yns_ref[...], w_ref[CONV_W + MLA_W:, :], preferred_element_type=f32)
        y_ref[...] = y

    row = lambda wd: pl.BlockSpec((tm, wd), lambda i: (i, 0))
    return pl.pallas_call(
        body, name=name, grid=(T // tm,),
        in_specs=[row(D_MODEL), row(MLA_W), row(MLA_W), row(CONV_W), row(SG_W), _const_spec((1, MLA_W)),
                  _const_spec((D_MODEL, D_MODEL))],
        out_specs=row(D_MODEL),
        out_shape=jax.ShapeDtypeStruct((T, D_MODEL), f32),
    )(x, o, proj, ync, yns, g_m, w_out)


def _loss_grad(y, tgt, tm, name):
    T = y.shape[0]
    nt = T // tm

    def body(y_ref, t_ref, dy_ref, loss_ref, acc_ref):
        i = pl.program_id(0)

        @pl.when(i == 0)
        def _():
            acc_ref[...] = jnp.zeros_like(acc_ref)

        err = y_ref[...] - t_ref[...]
        dy_ref[...] = err * (1.0 / D_MODEL)
        acc_ref[...] += jnp.sum(err * err, axis=0, keepdims=True)

        @pl.when(i == nt - 1)
        def _():
            loss_ref[...] = jnp.full((1, HEAD_PAD), 0.5 / D_MODEL, f32) * jnp.sum(acc_ref[...])

    row = pl.BlockSpec((tm, D_MODEL), lambda i: (i, 0))
    return pl.pallas_call(
        body, name=name, grid=(nt,), in_specs=[row, row],
        out_specs=[row, _const_spec((1, HEAD_PAD))],
        out_shape=[jax.ShapeDtypeStruct((T, D_MODEL), f32), jax.ShapeDtypeStruct((1, HEAD_PAD), f32)],
        scratch_shapes=[pltpu.VMEM((1, D_MODEL), f32)],
    )(y, tgt)


def _out_bwd(dout, o, proj, ync, yns, g_m, w_out, tm, name):
    T = dout.shape[0]

    def body(dy_ref, o_ref, zm_ref, ync_ref, yns_ref, g_ref, w_ref,
             do_ref, dl_ref, dzm_ref, dync_ref, dyns_ref, dw_ref, dg_ref):
        i = pl.program_id(0)

        @pl.when(i == 0)
        def _():
            dw_ref[...] = jnp.zeros_like(dw_ref)
            dg_ref[...] = jnp.zeros_like(dg_ref)

        dyb = dy_ref[...].astype(bf16)
        nt = (((1,), (1,)), ((), ()))
        tn = (((0,), (0,)), ((), ()))
        d_c = lax.dot_general(dyb, w_ref[0:CONV_W, :], nt, preferred_element_type=f32)
        d_m = lax.dot_general(dyb, w_ref[CONV_W:CONV_W + MLA_W, :], nt, preferred_element_type=f32)
        d_s = lax.dot_general(dyb, w_ref[CONV_W + MLA_W:, :], nt, preferred_element_type=f32)
        o = o_ref[...]
        ynm, vjp = jax.vjp(_mla_out, o, zm_ref[...].astype(f32), g_ref[...])
        do, dzm, dg = vjp(d_m)
        do_ref[...] = do.astype(bf16)
        dzm_ref[...] = dzm.astype(bf16)
        dync_ref[...] = d_c.astype(bf16)
        dyns_ref[...] = d_s.astype(bf16)
        dg_ref[...] += dg
        dw_ref[0:CONV_W, :] += lax.dot_general(ync_ref[...], dyb, tn, preferred_element_type=f32)
        dw_ref[CONV_W:CONV_W + MLA_W, :] += lax.dot_general(ynm.astype(bf16), dyb, tn, preferred_element_type=f32)
        dw_ref[CONV_W + MLA_W:, :] += lax.dot_general(yns_ref[...], dyb, tn, preferred_element_type=f32)
        prod = do * o
        hi = prod.astype(bf16)
        lo = (prod - hi.astype(f32)).astype(bf16)
        sel = (lax.broadcasted_iota(jnp.int32, (HEADS, MLA_W), 1) // VDIM
               == lax.broadcasted_iota(jnp.int32, (HEADS, MLA_W), 0)).astype(bf16)
        dl_ref[...] = (lax.dot_general(sel, hi, nt, preferred_element_type=f32)
                       + lax.dot_general(sel, lo, nt, preferred_element_type=f32))

    row = lambda wd: pl.BlockSpec((tm, wd), lambda i: (i, 0))
    return pl.pallas_call(
        body, name=name, grid=(T // tm,),
        in_specs=[row(D_MODEL), row(MLA_W), row(MLA_W), row(CONV_W), row(SG_W), _const_spec((1, MLA_W)),
                  _const_spec((D_MODEL, D_MODEL))],
        out_specs=[row(MLA_W), pl.BlockSpec((HEADS, tm), lambda i: (0, i)), row(MLA_W), row(CONV_W), row(SG_W),
                   _const_spec((D_MODEL, D_MODEL)), _const_spec((1, MLA_W))],
        out_shape=[jax.ShapeDtypeStruct((T, MLA_W), bf16), jax.ShapeDtypeStruct((HEADS, T), f32),
                   jax.ShapeDtypeStruct((T, MLA_W), bf16), jax.ShapeDtypeStruct((T, CONV_W), bf16),
                   jax.ShapeDtypeStruct((T, SG_W), bf16), jax.ShapeDtypeStruct((D_MODEL, D_MODEL), f32),
                   jax.ShapeDtypeStruct((1, MLA_W), f32)],
    )(dout, o, proj, ync, yns, g_m, w_out)


def _attn_bwd(q, k, v, do, lse, delta, tb, name):
    T = q.shape[0]
    nb = T // tb
    pw = 2 * HEAD_PAD
    nt = (((1,), (1,)), ((), ()))
    tn = (((0,), (0,)), ((), ()))

    def body(q_ref, do_ref, lse_ref, dl_ref, k_ref, v_ref, dq_ref, dk_ref, dv_ref):
        kj = pl.program_id(1)

        @pl.when(kj == 0)
        def _():
            dq_ref[...] = jnp.zeros_like(dq_ref)

        krow = lax.broadcasted_iota(jnp.int32, (tb, tb), 0)
        qcol = lax.broadcasted_iota(jnp.int32, (tb, tb), 1)
        for e in range(2):
            hs = slice(e * HEAD_PAD, (e + 1) * HEAD_PAD)
            kb = k_ref[:, hs]
            vb = v_ref[:, hs]

            def step(off, carry, masked, e=e, hs=hs, kb=kb, vb=vb):
                dk, dv = carry
                qb = q_ref[pl.ds(off, tb), hs]
                dob = do_ref[pl.ds(off, tb), :]
                s_t = lax.dot_general(kb, qb, nt, preferred_element_type=f32)
                p_t = jnp.exp(s_t - lse_ref[e, :, pl.ds(off, tb)])
                if masked:
                    p_t = jnp.where(krow <= qcol, p_t, 0.0)
                dv = dv + jnp.dot(p_t.astype(bf16), dob, preferred_element_type=f32)
                dp_t = lax.dot_general(vb, dob, nt, preferred_element_type=f32)
                ds_t = (p_t * (dp_t - dl_ref[e, :, pl.ds(off, tb)])).astype(bf16)
                dk = dk + jnp.dot(ds_t, qb, preferred_element_type=f32)
                dq_ref[pl.ds(off, tb), hs] += lax.dot_general(ds_t, kb, tn, preferred_element_type=f32)
                return dk, dv

            zero = jnp.zeros((tb, HEAD_PAD), f32)
            carry = step(pl.multiple_of(kj * tb, tb), (zero, zero), True)
            dk, dv = lax.fori_loop(kj + 1, nb, lambda qi, cr: step(pl.multiple_of(qi * tb, tb), cr, False), carry)
            dk_ref[:, hs] = dk
            dv_ref[:, hs] = dv

    wide = HEADS * HEAD_PAD
    return pl.pallas_call(
        body, name=name, grid=(HEADS // 2, nb),
        in_specs=[pl.BlockSpec((T, pw), lambda h, j: (0, h)), pl.BlockSpec((T, HEAD_PAD), lambda h, j: (0, h)),
                  pl.BlockSpec((2, 1, T), lambda h, j: (h, 0, 0)), pl.BlockSpec((2, 1, T), lambda h, j: (h, 0, 0)),
                  pl.BlockSpec((tb, pw), lambda h, j: (j, h)), pl.BlockSpec((tb, pw), lambda h, j: (j, h))],
        out_specs=[pl.BlockSpec((T, pw), lambda h, j: (0, h)), pl.BlockSpec((tb, pw), lambda h, j: (j, h)),
                   pl.BlockSpec((tb, pw), lambda h, j: (j, h))],
        out_shape=[jax.ShapeDtypeStruct((T, wide), f32)] * 3,
    )(q, do, lse, delta, k, v)


def _branch_bwd(proj, tabs, bw, dq, dk, dv, dync, dyns, tm, name):
    T = proj.shape[0]
    nt = T // tm
    hb = tm // HALO
    nw = len(BR_W)

    def body(proj_ref, halo_ref, rc_ref, rs1_ref, rs2_ref, *rest):
        wrefs = rest[:nw]
        dq_ref, dk_ref, dv_ref, dync_ref, dyns_ref = rest[nw:nw + 5]
        dp_ref = rest[nw + 5]
        gwrefs = rest[nw + 6:2 * nw + 6]
        carry_ref = rest[2 * nw + 6]
        i = pl.program_id(0)
        r = nt - 1 - i

        @pl.when(i == 0)
        def _():
            carry_ref[...] = jnp.zeros_like(carry_ref)
            for g in gwrefs:
                g[...] = jnp.zeros_like(g)

        c = {"hmask": (r > 0).astype(f32), "rc": rc_ref[...], "rs1": rs1_ref[...], "rs2": rs2_ref[...]}
        _, vjp = jax.vjp(lambda p_, w_: _branch(p_, w_, c), _load_branch_inputs(proj_ref, halo_ref),
                         _load_branch_weights(wrefs))
        cts = (dq_ref[...], dk_ref[...], dv_ref[...], dync_ref[...].astype(f32), dyns_ref[...].astype(f32))
        dp, dw = vjp(cts)
        for n in SEG:
            if n in ("zm", "a", "glu"):
                continue
            dp_ref[:, SEG[n][0] - 512:SEG[n][1] - 512] = dp[n].astype(bf16)
        for n, hn, lo in (("a", "ha", 0), ("glu", "hglu", CONV_W)):
            d = dp[n]
            tail = d[tm - HALO:, :] + carry_ref[:, lo:lo + CONV_W]
            s0 = SEG[n][0] - 512
            dp_ref[0:tm - HALO, s0:s0 + CONV_W] = d[:tm - HALO, :].astype(bf16)
            dp_ref[tm - HALO:tm, s0:s0 + CONV_W] = tail.astype(bf16)
        carry_ref[:, 0:CONV_W] = dp["ha"]
        carry_ref[:, CONV_W:] = dp["hglu"]
        for (n, _), g in zip(BR_W, gwrefs):
            if n in ("sg_w", "sg_b"):
                for gi in range(SG_HEADS):
                    g[gi] += dw[n][gi]
            else:
                g[...] += dw[n]

    row = lambda wd: pl.BlockSpec((tm, wd), lambda i: (nt - 1 - i, 0))
    wide = HEADS * HEAD_PAD
    return pl.pallas_call(
        body, name=name, grid=(nt,),
        in_specs=[row(PROJ_PAD),
                  pl.BlockSpec((HALO, 2 * CONV_W), lambda i: (jnp.maximum((nt - 1 - i) * hb - 1, 0), 1)),
                  row(HEAD_PAD), row(HEAD_PAD), row(HEAD_PAD)] + [_const_spec(s) for _, s in BR_W]
                 + [row(wide), row(wide), row(wide), row(CONV_W), row(SG_W)],
        out_specs=[row(REST)] + [_const_spec(s) for _, s in BR_W],
        out_shape=[jax.ShapeDtypeStruct((T, REST), bf16)] + [jax.ShapeDtypeStruct(s, f32) for _, s in BR_W],
        scratch_shapes=[pltpu.VMEM((HALO, 2 * CONV_W), f32)],
    )(proj, proj, *tabs, *[bw[n] for n, _ in BR_W], dq, dk, dv, dync, dyns)


def _inproj_bwd_dx(x, g, w, dzm, dprest, dout, tm, name):
    T = x.shape[0]
    nt_dims = (((1,), (1,)), ((), ()))

    def body(x_ref, g_ref, w_ref, dzm_ref, dpr_ref, dout_ref, dx_ref, dg_ref, h_ref):
        i = pl.program_id(0)

        @pl.when(i == 0)
        def _():
            dg_ref[...] = jnp.zeros_like(dg_ref)

        dh = lax.dot_general(dzm_ref[...], w_ref[:, 0:512], nt_dims, preferred_element_type=f32)
        dh = dh + lax.dot_general(dpr_ref[...], w_ref[:, 512:], nt_dims, preferred_element_type=f32)
        h, vjp = jax.vjp(_rms, x_ref[...], g_ref[...])
        dx, dg = vjp(dh)
        dx_ref[...] = dout_ref[...] + dx
        dg_ref[...] += dg
        h_ref[...] = h.astype(bf16)

    row = lambda wd: pl.BlockSpec((tm, wd), lambda i: (i, 0))
    return pl.pallas_call(
        body, name=name, grid=(T // tm,),
        in_specs=[row(D_MODEL), _const_spec((1, D_MODEL)), _const_spec((D_MODEL, PROJ_PAD)), row(512), row(REST),
                  row(D_MODEL)],
        out_specs=[row(D_MODEL), _const_spec((1, D_MODEL)), row(D_MODEL)],
        out_shape=[jax.ShapeDtypeStruct((T, D_MODEL), f32), jax.ShapeDtypeStruct((1, D_MODEL), f32),
                   jax.ShapeDtypeStruct((T, D_MODEL), bf16)],
    )(x, g, w, dzm, dprest, dout)


def _inproj_bwd_dw(h, dzm, dprest, tm, name):
    T = h.shape[0]
    nt = T // tm
    cw = 128 * 3
    tn = (((0,), (0,)), ((), ()))

    def body_for(nsteps):
        def body(h_ref, d_ref, o_ref):
            i = pl.program_id(1)

            @pl.when(i == 0)
            def _():
                o_ref[...] = jnp.zeros_like(o_ref)

            o_ref[...] += lax.dot_general(h_ref[...], d_ref[...], tn, preferred_element_type=f32)
        return body

    def run(d, width, cwid, nm):
        return pl.pallas_call(
            body_for(nt), name=nm, grid=(width // cwid, nt),
            in_specs=[pl.BlockSpec((tm, D_MODEL), lambda j, i: (i, 0)), pl.BlockSpec((tm, cwid), lambda j, i: (i, j))],
            out_specs=pl.BlockSpec((D_MODEL, cwid), lambda j, i: (0, j)),
            out_shape=jax.ShapeDtypeStruct((D_MODEL, width), f32),
        )(h, d)

    return jnp.concatenate([run(dzm, 512, 512, name + "_zm"), run(dprest, REST, cw, name + "_rest")], axis=1)


def _sum4(buf, name):
    R = buf.shape[1]
    tr = ROW_ALIGN

    def body(b_ref, o_ref):
        o_ref[...] = ((b_ref[0] + b_ref[1]) + b_ref[2]) + b_ref[3]

    return pl.pallas_call(
        body, name=name, grid=(R // tr,),
        in_specs=[pl.BlockSpec((4, tr, LANES), lambda i: (0, i, 0))],
        out_specs=pl.BlockSpec((tr, LANES), lambda i: (i, 0)),
        out_shape=jax.ShapeDtypeStruct((R, LANES), f32),
    )(buf)


def _adamw(w, m, v, p_a, p_b, name):
    R = w.shape[0]
    tr = ROW_ALIGN

    def body(w_ref, m_ref, v_ref, a_ref, b_ref, g_ref, d_ref, nm_ref, nv_ref):
        g = a_ref[...] + b_ref[...]
        m_new = ADAM_B1 * m_ref[...] + (1.0 - ADAM_B1) * g
        v_new = ADAM_B2 * v_ref[...] + (1.0 - ADAM_B2) * (g * g)
        m_hat = m_new / (1.0 - ADAM_B1 ** ADAM_STEP)
        v_hat = v_new / (1.0 - ADAM_B2 ** ADAM_STEP)
        g_ref[...] = g
        d_ref[...] = -ADAM_LR * (m_hat / (jnp.sqrt(v_hat) + ADAM_EPS) + ADAM_WD * w_ref[...])
        nm_ref[...] = m_new
        nv_ref[...] = v_new

    spec = pl.BlockSpec((tr, LANES), lambda i: (i, 0))
    return pl.pallas_call(
        body, name=name, grid=(R // tr,), in_specs=[spec] * 5, out_specs=[spec] * 4,
        out_shape=[jax.ShapeDtypeStruct((R, LANES), f32)] * 4,
    )(w, m, v, p_a, p_b)


def _chip_exchange(src, per_target, name):
    R = src.shape[-2]

    def body(src_ref, out_ref, send_sems, recv_sems, local_sem):
        x, y, c = lax.axis_index("x"), lax.axis_index("y"), lax.axis_index("c")
        me = 2 * x + y
        chips = [(1 - x, y), (x, 1 - y), (1 - x, 1 - y)]

        def block_for(t):
            return src_ref.at[t] if per_target else src_ref

        mine = pltpu.make_async_copy(block_for(me), out_ref.at[me], local_sem)
        mine.start()
        copies = []
        for k, (tx, ty) in enumerate(chips):
            cp = pltpu.make_async_remote_copy(
                src_ref=block_for(2 * tx + ty), dst_ref=out_ref.at[me], send_sem=send_sems.at[k],
                recv_sem=recv_sems.at[k], device_id=(tx, ty, c), device_id_type=pl.DeviceIdType.MESH)
            cp.start()
            copies.append(cp)
        for k, (tx, ty) in enumerate(chips):
            pltpu.make_async_remote_copy(
                src_ref=block_for(me), dst_ref=out_ref.at[2 * tx + ty], send_sem=send_sems.at[k],
                recv_sem=recv_sems.at[k], device_id=(tx, ty, c), device_id_type=pl.DeviceIdType.MESH).wait_recv()
        for cp in copies:
            cp.wait_send()
        mine.wait()

    return pl.pallas_call(
        body, name=name,
        in_specs=[pl.BlockSpec(memory_space=pl.ANY)], out_specs=pl.BlockSpec(memory_space=pl.ANY),
        out_shape=jax.ShapeDtypeStruct((4, R, LANES), src.dtype),
        scratch_shapes=[pltpu.SemaphoreType.DMA((3,)), pltpu.SemaphoreType.DMA((3,)), pltpu.SemaphoreType.DMA],
    )(src)


def _sibling_exchange(src, name):
    def body(src_ref, out_ref, send_sem, recv_sem):
        x, y, c = lax.axis_index("x"), lax.axis_index("y"), lax.axis_index("c")
        cp = pltpu.make_async_remote_copy(
            src_ref=src_ref, dst_ref=out_ref, send_sem=send_sem, recv_sem=recv_sem,
            device_id=(x, y, 1 - c), device_id_type=pl.DeviceIdType.MESH)
        cp.start()
        cp.wait()

    return pl.pallas_call(
        body, name=name,
        in_specs=[pl.BlockSpec(memory_space=pl.ANY)], out_specs=pl.BlockSpec(memory_space=pl.ANY),
        out_shape=jax.ShapeDtypeStruct(src.shape, src.dtype),
        scratch_shapes=[pltpu.SemaphoreType.DMA, pltpu.SemaphoreType.DMA],
    )(src)


def _pack(arrs, dtype):
    flat = jnp.concatenate([a.reshape(-1).astype(dtype) for a in arrs])
    n = flat.shape[0]
    rows = -(-n // (LANES * ROW_ALIGN)) * ROW_ALIGN
    return jnp.pad(flat, (0, rows * LANES - n)).reshape(rows, LANES)


def _unpack(packed, shapes):
    flat = packed.reshape(-1)
    out, off = [], 0
    for s in shapes:
        n = int(np.prod(s))
        out.append(flat[off:off + n].reshape(s))
        off += n
    return out


def _shard_shape(name, full_shape):
    s = list(full_shape)
    s[BIG[name]] //= 4
    return tuple(s)


FULL_SHAPES = {
    "w_in": (DEPTH, D_MODEL, IN_COLS), "conv_w": (DEPTH, CONV_K, CONV_W), "conv_pw_w": (DEPTH, CONV_W, CONV_W),
    "w_uq": (DEPTH, Q_LORA, HEADS * QK), "w_ukv": (DEPTH, KV_LORA, HEADS * (NOPE + VDIM)),
    "w_out": (DEPTH, D_MODEL, D_MODEL),
}


def _pad_w_in(w):
    z = lambda n: jnp.zeros((D_MODEL, n), w.dtype)
    return jnp.concatenate([w[:, 1824:2336], w[:, 0:1792], z(64), w[:, 1792:1824], z(32), w[:, 2336:]], axis=1)


def _unpad_w_in(d):
    return jnp.concatenate([d[:, 512:2304], d[:, 2368:2400], d[:, 0:512], d[:, 2432:]], axis=1)


def _v_cols():
    return [(h % 2) * VDIM for h in range(HEADS)]


def _layer_weights(full, small, l):
    w_uq = full["w_uq"][l].astype(f32).reshape(Q_LORA, HEADS, QK)
    w_uq = jnp.pad(w_uq, ((0, 0), (0, 0), (0, HEAD_PAD - QK))).reshape(Q_LORA, HEADS * HEAD_PAD)
    ukv = full["w_ukv"][l].astype(f32).reshape(KV_LORA, HEADS, NOPE + VDIM)
    w_k = jnp.pad(ukv[:, :, :NOPE], ((0, 0), (0, 0), (0, HEAD_PAD - NOPE))).reshape(KV_LORA, HEADS * HEAD_PAD)
    zeros = jnp.zeros((KV_LORA, VDIM), f32)
    w_v = jnp.concatenate(
        [jnp.concatenate([ukv[:, h, NOPE:], zeros] if h % 2 == 0 else [zeros, ukv[:, h, NOPE:]], axis=1)
         for h in range(HEADS)], axis=1)
    row = lambda a: a.reshape(1, -1)
    bn = small["branch_norm_g"][l]
    bw = {
        "conv_w": jnp.pad(full["conv_w"][l], ((0, 1), (0, 0))), "conv_b": row(small["conv_b"][l]),
        "cln_g": row(small["conv_ln_g"][l]), "cln_b": row(small["conv_ln_b"][l]),
        "pw_w": full["conv_pw_w"][l].astype(f32), "pw_b": row(small["conv_pw_b"][l]),
        "qn_g": row(small["q_norm_g"][l]), "w_uq": w_uq, "kvn_g": row(small["kv_norm_g"][l]),
        "w_k": w_k, "w_v": w_v,
        "qkq_g": jnp.pad(row(small["qk_q_g"][l]), ((0, 0), (0, HEAD_PAD - QK))),
        "qkk_g": jnp.pad(row(small["qk_k_g"][l]), ((0, 0), (0, HEAD_PAD - QK))),
        "sln_g": row(small["sg_ln_g"][l]), "sln_b": row(small["sg_ln_b"][l]),
        "sg_w": small["sg_w"][l], "sg_b": small["sg_b"][l].reshape(SG_HEADS, SG_CHUNK, 1),
        "bn_c": row(bn[:CONV_W]), "bn_s": row(bn[CONV_W + MLA_W:]),
    }
    return {
        "norm_g": row(small["norm_g"][l]), "w_in": _pad_w_in(full["w_in"][l]), "bw": bw,
        "bn_m": row(bn[CONV_W:CONV_W + MLA_W]), "w_out": full["w_out"][l],
    }


def _layer_grads(gb, dw_in_pad, dnorm_g, dw_out, dbn_m):
    duq = gb["w_uq"].reshape(Q_LORA, HEADS, HEAD_PAD)[:, :, :QK].reshape(Q_LORA, HEADS * QK)
    dk = gb["w_k"].reshape(KV_LORA, HEADS, HEAD_PAD)[:, :, :NOPE]
    dv = gb["w_v"].reshape(KV_LORA, HEADS, HEAD_PAD)
    dv = jnp.stack([dv[:, h, c0:c0 + VDIM] for h, c0 in enumerate(_v_cols())], axis=1)
    dukv = jnp.concatenate([dk, dv], axis=2).reshape(KV_LORA, HEADS * (NOPE + VDIM))
    return {
        "norm_g": dnorm_g[0], "w_in": _unpad_w_in(dw_in_pad), "conv_w": gb["conv_w"][:CONV_K],
        "conv_b": gb["conv_b"][0], "conv_ln_g": gb["cln_g"][0], "conv_ln_b": gb["cln_b"][0],
        "conv_pw_w": gb["pw_w"], "conv_pw_b": gb["pw_b"][0], "q_norm_g": gb["qn_g"][0], "w_uq": duq,
        "kv_norm_g": gb["kvn_g"][0], "w_ukv": dukv, "qk_q_g": gb["qkq_g"][0, :QK], "qk_k_g": gb["qkk_g"][0, :QK],
        "sg_ln_g": gb["sln_g"][0], "sg_ln_b": gb["sln_b"][0], "sg_w": gb["sg_w"], "sg_b": gb["sg_b"][:, :, 0],
        "branch_norm_g": jnp.concatenate([gb["bn_c"][0], dbn_m[0], gb["bn_s"][0]]), "w_out": dw_out,
    }


def _rope_tables(T):
    half = ROPE // 2
    inv_freq = ROPE_THETA ** (-jnp.arange(half, dtype=f32) / half)
    ang = jnp.arange(T, dtype=f32)[:, None] * inv_freq[None, :]
    cos, sin = jnp.cos(ang), jnp.sin(ang)
    one = jnp.ones((T, NOPE), f32)
    z = lambda n: jnp.zeros((T, n), f32)
    rc = jnp.concatenate([one, cos, cos, jnp.ones((T, HEAD_PAD - QK), f32)], axis=1)
    rs1 = jnp.concatenate([z(NOPE), -sin, z(half), z(HEAD_PAD - QK)], axis=1)
    rs2 = jnp.concatenate([z(NOPE), z(half), sin, z(HEAD_PAD - QK)], axis=1)
    return rc, rs1, rs2


def kernel(x, norm_g, w_in, conv_w, conv_b, conv_ln_g, conv_ln_b, conv_pw_w, conv_pw_b, q_norm_g, w_uq, kv_norm_g, w_ukv, qk_q_g, qk_k_g, sg_ln_g, sg_ln_b, sg_w, sg_b, branch_norm_g, w_out, loss_target, m_norm_g, m_w_in, m_conv_w, m_conv_b, m_conv_ln_g, m_conv_ln_b, m_conv_pw_w, m_conv_pw_b, m_q_norm_g, m_w_uq, m_kv_norm_g, m_w_ukv, m_qk_q_g, m_qk_k_g, m_sg_ln_g, m_sg_ln_b, m_sg_w, m_sg_b, m_branch_norm_g, m_w_out, v_norm_g, v_w_in, v_conv_w, v_conv_b, v_conv_ln_g, v_conv_ln_b, v_conv_pw_w, v_conv_pw_b, v_q_norm_g, v_w_uq, v_kv_norm_g, v_w_ukv, v_qk_q_g, v_qk_k_g, v_sg_ln_g, v_sg_ln_b, v_sg_w, v_sg_b, v_branch_norm_g, v_w_out):
    wts = dict(zip(WEIGHTS, [norm_g, w_in, conv_w, conv_b, conv_ln_g, conv_ln_b, conv_pw_w, conv_pw_b, q_norm_g,
                             w_uq, kv_norm_g, w_ukv, qk_q_g, qk_k_g, sg_ln_g, sg_ln_b, sg_w, sg_b, branch_norm_g,
                             w_out]))
    mom_m = dict(zip(WEIGHTS, [m_norm_g, m_w_in, m_conv_w, m_conv_b, m_conv_ln_g, m_conv_ln_b, m_conv_pw_w,
                               m_conv_pw_b, m_q_norm_g, m_w_uq, m_kv_norm_g, m_w_ukv, m_qk_q_g, m_qk_k_g,
                               m_sg_ln_g, m_sg_ln_b, m_sg_w, m_sg_b, m_branch_norm_g, m_w_out]))
    mom_v = dict(zip(WEIGHTS, [v_norm_g, v_w_in, v_conv_w, v_conv_b, v_conv_ln_g, v_conv_ln_b, v_conv_pw_w,
                               v_conv_pw_b, v_q_norm_g, v_w_uq, v_kv_norm_g, v_w_ukv, v_qk_q_g, v_qk_k_g,
                               v_sg_ln_g, v_sg_ln_b, v_sg_w, v_sg_b, v_branch_norm_g, v_w_out]))
    xs = x[0]
    tgt = loss_target[0]
    T = xs.shape[0]
    tm = min(256, T)
    tb = min(512, T // 2)

    big_names = [n for n in BIG if n != "conv_w"]
    cw_bits = lax.bitcast_convert_type(wts["conv_w"], bf16)
    payload = _pack([wts[n] for n in big_names] + [cw_bits], bf16)
    gathered = _chip_exchange(payload, False, "gather_weights")
    shard_shapes = [wts[n].shape for n in big_names] + [cw_bits.shape]
    per_chip = [_unpack(gathered[j], shard_shapes) for j in range(4)]
    full = {}
    for i, n in enumerate(big_names):
        full[n] = jnp.concatenate([per_chip[j][i] for j in range(4)], axis=BIG[n])
    full["conv_w"] = jnp.concatenate(
        [lax.bitcast_convert_type(per_chip[j][-1], f32) for j in range(4)], axis=BIG["conv_w"])
    small = {n: wts[n] for n in SMALL}

    tabs = _rope_tables(T)
    lw = [_layer_weights(full, small, l) for l in range(DEPTH)]

    acts = []
    h_in = xs
    for l in range(DEPTH):
        W = lw[l]
        proj = _inproj_fwd(h_in, W["norm_g"], W["w_in"], tm, f"inproj_fwd_{l}")
        q, k, v, ync, yns = _branch_fwd(proj, tabs, W["bw"], tm, f"branch_fwd_{l}")
        o, lse = _attn_fwd(q, k, v, tb, f"attn_fwd_{l}")
        y = _out_fwd(h_in, o, proj, ync, yns, W["bn_m"], W["w_out"], tm, f"out_fwd_{l}")
        acts.append((h_in, proj, q, k, v, ync, yns, o, lse))
        h_in = y

    dy, loss_row = _loss_grad(h_in, tgt, tm, "loss_grad")
    loss = lax.psum(loss_row[0, 0], ("x", "y", "c"))

    grads = [None] * DEPTH
    for l in reversed(range(DEPTH)):
        W = lw[l]
        xin, proj, q, k, v, ync, yns, o, lse = acts[l]
        do, delta, dzm, dync, dyns, dw_out, dbn_m = _out_bwd(
            dy, o, proj, ync, yns, W["bn_m"], W["w_out"], tm, f"out_bwd_{l}")
        dq, dk, dv = _attn_bwd(q, k, v, do, lse, delta.reshape(HEADS, 1, T), tb, f"attn_bwd_{l}")
        outs = _branch_bwd(proj, tabs, W["bw"], dq, dk, dv, dync, dyns, tm, f"branch_bwd_{l}")
        dprest, gb = outs[0], dict(zip([n for n, _ in BR_W], outs[1:]))
        dx, dnorm_g, hb = _inproj_bwd_dx(xin, W["norm_g"], W["w_in"], dzm, dprest, dy, tm, f"inproj_bwd_dx_{l}")
        dw_in_pad = _inproj_bwd_dw(hb, dzm, dprest, tm, f"inproj_bwd_dw_{l}")
        grads[l] = _layer_grads(gb, dw_in_pad, dnorm_g, dw_out, dbn_m)
        dy = dx
    grad_x = dy[None]
    g_full = {n: jnp.stack([grads[l][n] for l in range(DEPTH)]) for n in WEIGHTS}

    small_vec = [g_full[n] for n in SMALL]
    per_target = []
    for j in range(4):
        parts = []
        for n in BIG:
            ax, sz = BIG[n], FULL_SHAPES[n][BIG[n]] // 4
            parts.append(lax.slice_in_dim(g_full[n], j * sz, (j + 1) * sz, axis=ax))
        per_target.append(_pack(parts + small_vec, f32))
    contrib = _chip_exchange(jnp.stack(per_target), True, "exchange_grads")
    part = _sum4(contrib, "sum_chips")
    other = _sibling_exchange(part, "exchange_cores")

    order = list(BIG) + SMALL
    shapes = [wts[n].shape for n in order]
    g_p, d_p, nm_p, nv_p = _adamw(
        _pack([wts[n] for n in order], f32), _pack([mom_m[n] for n in order], f32),
        _pack([mom_v[n] for n in order], f32), part, other, "adamw")
    res = {}
    for kind, packed in (("grad", g_p), ("delta", d_p), ("new_m", nm_p), ("new_v", nv_p)):
        res[kind] = dict(zip(order, _unpack(packed, shapes)))
    out = [loss, grad_x]
    for kind in ("grad", "delta", "new_m", "new_v"):
        out += [res[kind][n] for n in WEIGHTS]
    return tuple(out)
```

```python
import functools

import jax
import jax.numpy as jnp
import numpy as np
from jax import lax
from jax.experimental import pallas as pl
from jax.experimental.pallas import tpu as pltpu

f32 = jnp.float32
bf16 = jnp.bfloat16

DEPTH = 2
D_MODEL = 1024
CONV_W = 256
CONV_K = 31
HEADS = 8
NOPE = 64
ROPE = 32
QK = NOPE + ROPE
VDIM = 64
MLA_W = HEADS * VDIM
Q_LORA = 768
KV_LORA = 256
SG_W = 256
SG_HEADS = 4
SG_CHUNK = 128
IN_COLS = 3104
ROPE_THETA = 10000.0
EPS = 1e-6
HEAD_PAD = 128
LOG2E = 1.4426950408889634
LN2 = 0.6931471805599453
Q_SCALE = QK ** -0.5 * LOG2E
HALO = 32
LANES = 1024
ROW_ALIGN = 256

ADAM_LR = 0.001
ADAM_B1 = 0.9
ADAM_B2 = 0.999
ADAM_EPS = 1e-08
ADAM_WD = 0.01
ADAM_STEP = 10

PROJ_PAD = 3200
SEG = {
    "zm": (0, 512), "a": (512, 768), "glu": (768, 1024), "zc": (1024, 1280), "cq": (1280, 2048),
    "ckv": (2048, 2304), "kr": (2304, 2432), "us": (2432, 2688), "vs": (2688, 2944), "zs": (2944, 3200),
}
REST = PROJ_PAD - 512

WEIGHTS = ["norm_g", "w_in", "conv_w", "conv_b", "conv_ln_g", "conv_ln_b", "conv_pw_w", "conv_pw_b", "q_norm_g",
           "w_uq", "kv_norm_g", "w_ukv", "qk_q_g", "qk_k_g", "sg_ln_g", "sg_ln_b", "sg_w", "sg_b",
           "branch_norm_g", "w_out"]
BIG = {"w_in": 2, "conv_w": 2, "conv_pw_w": 1, "w_uq": 1, "w_ukv": 2, "w_out": 1}
SMALL = [n for n in WEIGHTS if n not in BIG]

BR_W = [("conv_w", (32, 256)), ("conv_b", (1, 256)), ("cln_g", (1, 256)), ("cln_b", (1, 256)),
        ("pw_w", (256, 256)), ("pw_b", (1, 256)), ("qn_g", (1, 768)), ("w_uq", (768, 1024)),
        ("kvn_g", (1, 256)), ("w_k", (256, 1024)), ("w_v", (256, 1024)), ("qkq_g", (1, 128)),
        ("qkk_g", (1, 128)), ("sln_g", (1, 256)), ("sln_b", (1, 256)), ("sg_w", (4, 128, 128)),
        ("sg_b", (4, 128, 1)), ("bn_c", (1, 256)), ("bn_s", (1, 256))]


@jax.custom_vjp
def _mm(a, w):
    return jnp.dot(a.astype(bf16), w.astype(bf16), preferred_element_type=f32)


def _mm_fwd(a, w):
    return _mm(a, w), (a, w)


def _mm_bwd(res, ct):
    a, w = res
    ctb = ct.astype(bf16)
    da = lax.dot_general(ctb, w.astype(bf16), (((1,), (1,)), ((), ())), preferred_element_type=f32)
    dw = lax.dot_general(a.astype(bf16), ctb, (((0,), (0,)), ((), ())), preferred_element_type=f32)
    return da.astype(a.dtype), dw.astype(w.dtype)


_mm.defvjp(_mm_fwd, _mm_bwd)


@jax.custom_vjp
def _rope(x, c, s1, s2):
    return x * c + pltpu.roll(x, HEAD_PAD - 16, 1) * s1 + pltpu.roll(x, 16, 1) * s2


def _rope_fwd(x, c, s1, s2):
    return _rope(x, c, s1, s2), (c, s1, s2)


def _rope_bwd(res, ct):
    c, s1, s2 = res
    dx = ct * c + pltpu.roll(ct * s1, 16, 1) + pltpu.roll(ct * s2, HEAD_PAD - 16, 1)
    return dx, jnp.zeros_like(c), jnp.zeros_like(s1), jnp.zeros_like(s2)


_rope.defvjp(_rope_fwd, _rope_bwd)


def _rms(x, g):
    return x * lax.rsqrt(jnp.mean(x * x, axis=-1, keepdims=True) + EPS) * g


def _head_rms(x, g):
    return x * lax.rsqrt(jnp.sum(x * x, axis=-1, keepdims=True) * (1.0 / QK) + EPS) * g


def _ln(x, g, b):
    mu = jnp.mean(x, axis=-1, keepdims=True)
    xc = x - mu
    var = jnp.mean(xc * xc, axis=-1, keepdims=True)
    return xc * lax.rsqrt(var + EPS) * g + b


def _branch(p, w, c):
    tm = p["a"].shape[0]
    yg = p["a"] * jax.nn.sigmoid(p["glu"])
    yh = p["ha"] * jax.nn.sigmoid(p["hglu"]) * c["hmask"]
    ycat = jnp.concatenate([yh, yg], axis=0)
    shifted = [ycat] + [ycat[b:b + tm + HALO - 8, :] for b in range(1, 8)]
    acc = jnp.zeros((tm, CONV_W), f32)
    for k in range(CONV_K):
        off = HALO - (CONV_K - 1) + k
        a8 = off - off % 8
        acc = acc + shifted[off % 8][a8:a8 + tm, :] * w["conv_w"][k:k + 1, :]
    yl = jax.nn.silu(_ln(acc + w["conv_b"], w["cln_g"], w["cln_b"]))
    ypw = _mm(yl, w["pw_w"]) + w["pw_b"]
    ync = _rms(ypw * jax.nn.silu(p["zc"]), w["bn_c"])
    cqn = _rms(p["cq"], w["qn_g"])
    qf = _mm(cqn, w["w_uq"])
    ckvn = _rms(p["ckv"], w["kvn_g"])
    kf = _mm(ckvn, w["w_k"])
    v = _mm(ckvn, w["w_v"])
    qs, ks = [], []
    for h in range(HEADS):
        sl = slice(h * HEAD_PAD, (h + 1) * HEAD_PAD)
        qh = _head_rms(qf[:, sl], w["qkq_g"])
        qs.append(_rope(qh, c["rc"], c["rs1"], c["rs2"]) * Q_SCALE)
        kh = _head_rms(kf[:, sl] + p["kr"], w["qkk_g"])
        ks.append(_rope(kh, c["rc"], c["rs1"], c["rs2"]))
    q = jnp.concatenate(qs, axis=-1)
    k = jnp.concatenate(ks, axis=-1)
    u = jax.nn.gelu(p["us"])
    v2 = _ln(jax.nn.gelu(p["vs"]), w["sln_g"], w["sln_b"])
    r = lax.broadcasted_iota(jnp.int32, (SG_CHUNK, SG_CHUNK), 0)
    cc = lax.broadcasted_iota(jnp.int32, (SG_CHUNK, SG_CHUNK), 1)
    lane = lax.broadcasted_iota(jnp.int32, (1, SG_W), 1) // (SG_W // SG_HEADS)
    wm = [jnp.where(r >= cc, w["sg_w"][g], 0.0) for g in range(SG_HEADS)]
    rows = []
    for ci in range(tm // SG_CHUNK):
        vc = v2[ci * SG_CHUNK:(ci + 1) * SG_CHUNK, :]
        mixed = jnp.zeros((SG_CHUNK, SG_W), f32)
        for g in range(SG_HEADS):
            mixed = mixed + jnp.where(lane == g, _mm(wm[g], vc) + w["sg_b"][g], 0.0)
        rows.append(mixed)
    mixed = jnp.concatenate(rows, axis=0) if len(rows) > 1 else rows[0]
    yns = _rms(u * mixed * jax.nn.silu(p["zs"]), w["bn_s"])
    return q, k, v, ync, yns


def _mla_out(o, zm, g):
    return _rms(o * jax.nn.silu(zm), g)


def _load_branch_inputs(proj_ref, halo_ref):
    p = {n: proj_ref[:, SEG[n][0]:SEG[n][1]].astype(f32) for n in SEG if n != "zm"}
    p["ha"] = halo_ref[:, 0:CONV_W].astype(f32)
    p["hglu"] = halo_ref[:, CONV_W:2 * CONV_W].astype(f32)
    return p


def _load_branch_weights(refs):
    w = {}
    for (n, _), r in zip(BR_W, refs):
        if n in ("sg_w", "sg_b"):
            w[n] = [r[g] for g in range(SG_HEADS)]
        else:
            w[n] = r[...]
    return w


def _const_spec(shape):
    nd = len(shape)
    return pl.BlockSpec(shape, lambda *_: (0,) * nd)


def _inproj_fwd(x, g, w, tm, name):
    T = x.shape[0]
    nc = 5
    cw = PROJ_PAD // nc

    def body(x_ref, g_ref, w_ref, o_ref):
        h = _rms(x_ref[...], g_ref[...]).astype(bf16)
        for j in range(nc):
            o_ref[:, j * cw:(j + 1) * cw] = jnp.dot(
                h, w_ref[:, j * cw:(j + 1) * cw], preferred_element_type=f32).astype(bf16)

    return pl.pallas_call(
        body, name=name, grid=(T // tm,),
        in_specs=[pl.BlockSpec((tm, D_MODEL), lambda i: (i, 0)), _const_spec((1, D_MODEL)),
                  _const_spec((D_MODEL, PROJ_PAD))],
        out_specs=pl.BlockSpec((tm, PROJ_PAD), lambda i: (i, 0)),
        out_shape=jax.ShapeDtypeStruct((T, PROJ_PAD), bf16),
    )(x, g, w)


def _branch_fwd(proj, tabs, bw, tm, name):
    T = proj.shape[0]
    hb = tm // HALO

    def body(proj_ref, halo_ref, rc_ref, rs1_ref, rs2_ref, *rest):
        wrefs, (q_ref, k_ref, v_ref, ync_ref, yns_ref) = rest[:len(BR_W)], rest[len(BR_W):]
        i = pl.program_id(0)
        c = {"hmask": (i > 0).astype(f32), "rc": rc_ref[...], "rs1": rs1_ref[...], "rs2": rs2_ref[...]}
        q, k, v, ync, yns = _branch(_load_branch_inputs(proj_ref, halo_ref), _load_branch_weights(wrefs), c)
        q_ref[...] = q.astype(bf16)
        k_ref[...] = k.astype(bf16)
        v_ref[...] = v.astype(bf16)
        ync_ref[...] = ync.astype(bf16)
        yns_ref[...] = yns.astype(bf16)

    row = lambda wd: pl.BlockSpec((tm, wd), lambda i: (i, 0))
    wide = HEADS * HEAD_PAD
    return pl.pallas_call(
        body, name=name, grid=(T // tm,),
        in_specs=[row(PROJ_PAD), pl.BlockSpec((HALO, 2 * CONV_W), lambda i: (jnp.maximum(i * hb - 1, 0), 1)),
                  row(HEAD_PAD), row(HEAD_PAD), row(HEAD_PAD)] + [_const_spec(s) for _, s in BR_W],
        out_specs=[row(wide), row(wide), row(wide), row(CONV_W), row(SG_W)],
        out_shape=[jax.ShapeDtypeStruct((T, wide), bf16)] * 3 + [jax.ShapeDtypeStruct((T, CONV_W), bf16)] * 2,
    )(proj, proj, *tabs, *[bw[n] for n, _ in BR_W])


def _attn_fwd(q, k, v, tb, name):
    T = q.shape[0]
    nb = T // tb
    pw = 2 * HEAD_PAD

    def body(q_ref, k_ref, v_ref, o_ref, lse_ref):
        qi = pl.program_id(1)
        krow = lax.broadcasted_iota(jnp.int32, (tb, tb), 0)
        qcol = lax.broadcasted_iota(jnp.int32, (tb, tb), 1)
        heads = [slice(e * HEAD_PAD, (e + 1) * HEAD_PAD) for e in range(2)]
        qh = [q_ref[:, hs] for hs in heads]

        def step(off, carry, masked):
            out = []
            for e, hs in enumerate(heads):
                m, l, acc = carry[e]
                kb = k_ref[pl.ds(off, tb), hs]
                vb = v_ref[pl.ds(off, tb), hs]
                s_t = lax.dot_general(kb, qh[e], (((1,), (1,)), ((), ())), preferred_element_type=f32)
                if masked:
                    s_t = jnp.where(krow <= qcol, s_t, -1e30)
                m_new = jnp.maximum(m, jnp.max(s_t, axis=0, keepdims=True))
                p_t = jnp.exp2(s_t - m_new)
                alpha = jnp.exp2(m - m_new)
                l = alpha * l + jnp.sum(p_t, axis=0, keepdims=True)
                acc = alpha * acc + lax.dot_general(vb, p_t.astype(bf16), (((0,), (0,)), ((), ())),
                                                    preferred_element_type=f32)
                out.append((m_new, l, acc))
            return tuple(out)

        one = (jnp.full((1, tb), -1e30, f32), jnp.zeros((1, tb), f32), jnp.zeros((HEAD_PAD, tb), f32))
        carry = lax.fori_loop(0, qi, lambda ki, cr: step(pl.multiple_of(ki * tb, tb), cr, False), (one, one))
        carry = step(pl.multiple_of(qi * tb, tb), carry, True)
        o_t = jnp.zeros((HEAD_PAD, tb), f32)
        for e in range(2):
            m, l, acc = carry[e]
            o_t = o_t + acc / l
            lse_ref[e] = m + jnp.log2(l)
        o_ref[...] = o_t.T

    return pl.pallas_call(
        body, name=name, grid=(HEADS // 2, nb),
        in_specs=[pl.BlockSpec((tb, pw), lambda h, i: (i, h)), pl.BlockSpec((T, pw), lambda h, i: (0, h)),
                  pl.BlockSpec((T, pw), lambda h, i: (0, h))],
        out_specs=[pl.BlockSpec((tb, HEAD_PAD), lambda h, i: (i, h)), pl.BlockSpec((2, 1, tb), lambda h, i: (h, 0, i))],
        out_shape=[jax.ShapeDtypeStruct((T, MLA_W), f32), jax.ShapeDtypeStruct((HEADS, 1, T), f32)],
    )(q, k, v)


def _out_fwd(x, o, proj, ync, yns, g_m, w_out, tm, name):
    T = x.shape[0]

    def body(x_ref, o_ref, zm_ref, ync_ref, yns_ref, g_ref, w_ref, y_ref):
        ynm = _mla_out(o_ref[...], zm_ref[...].astype(f32), g_ref[...]).astype(bf16)
        y = x_ref[...]
        y = y + jnp.dot(ync_ref[...], w_ref[0:CONV_W, :], preferred_element_type=f32)
        y = y + jnp.dot(ynm, w_ref[CONV_W:CONV_W + MLA_W, :], preferred_element_type=f32)
        y = y + jnp.dot(yns_ref[...], w_ref[CONV_W + MLA_W:, :], preferred_element_type=f32)
        y_ref[...] = y

    row = lambda wd: pl.BlockSpec((tm, wd), lambda i: (i, 0))
    return pl.pallas_call(
        body, name=name, grid=(T // tm,),
        in_specs=[row(D_MODEL), row(MLA_W), row(MLA_W), row(CONV_W), row(SG_W), _const_spec((1, MLA_W)),
                  _const_spec((D_MODEL, D_MODEL))],
        out_specs=row(D_MODEL),
        out_shape=jax.ShapeDtypeStruct((T, D_MODEL), f32),
    )(x, o, proj, ync, yns, g_m, w_out)


def _loss_grad(y, tgt, tm, name):
    T = y.shape[0]
    nt = T // tm

    def body(y_ref, t_ref, dy_ref, loss_ref, acc_ref):
        i = pl.program_id(0)

        @pl.when(i == 0)
        def _():
            acc_ref[...] = jnp.zeros_like(acc_ref)

        err = y_ref[...] - t_ref[...]
        dy_ref[...] = err * (1.0 / D_MODEL)
        acc_ref[...] += jnp.sum(err * err, axis=0, keepdims=True)

        @pl.when(i == nt - 1)
        def _():
            loss_ref[...] = jnp.full((1, HEAD_PAD), 0.5 / D_MODEL, f32) * jnp.sum(acc_ref[...])

    row = pl.BlockSpec((tm, D_MODEL), lambda i: (i, 0))
    return pl.pallas_call(
        body, name=name, grid=(nt,), in_specs=[row, row],
        out_specs=[row, _const_spec((1, HEAD_PAD))],
        out_shape=[jax.ShapeDtypeStruct((T, D_MODEL), f32), jax.ShapeDtypeStruct((1, HEAD_PAD), f32)],
        scratch_shapes=[pltpu.VMEM((1, D_MODEL), f32)],
    )(y, tgt)


def _out_bwd(dout, o, proj, ync, yns, g_m, w_out, tm, name):
    T = dout.shape[0]

    def body(dy_ref, o_ref, zm_ref, ync_ref, yns_ref, g_ref, w_ref,
             do_ref, dl_ref, dzm_ref, dync_ref, dyns_ref, dw_ref, dg_ref):
        i = pl.program_id(0)

        @pl.when(i == 0)
        def _():
            dw_ref[...] = jnp.zeros_like(dw_ref)
            dg_ref[...] = jnp.zeros_like(dg_ref)

        dyb = dy_ref[...].astype(bf16)
        nt = (((1,), (1,)), ((), ()))
        tn = (((0,), (0,)), ((), ()))
        d_c = lax.dot_general(dyb, w_ref[0:CONV_W, :], nt, preferred_element_type=f32)
        d_m = lax.dot_general(dyb, w_ref[CONV_W:CONV_W + MLA_W, :], nt, preferred_element_type=f32)
        d_s = lax.dot_general(dyb, w_ref[CONV_W + MLA_W:, :], nt, preferred_element_type=f32)
        o = o_ref[...]
        ynm, vjp = jax.vjp(_mla_out, o, zm_ref[...].astype(f32), g_ref[...])
        do, dzm, dg = vjp(d_m)
        do_ref[...] = do.astype(bf16)
        dzm_ref[...] = dzm.astype(bf16)
        dync_ref[...] = d_c.astype(bf16)
        dyns_ref[...] = d_s.astype(bf16)
        dg_ref[...] += dg
        dw_ref[0:CONV_W, :] += lax.dot_general(ync_ref[...], dyb, tn, preferred_element_type=f32)
        dw_ref[CONV_W:CONV_W + MLA_W, :] += lax.dot_general(ynm.astype(bf16), dyb, tn, preferred_element_type=f32)
        dw_ref[CONV_W + MLA_W:, :] += lax.dot_general(yns_ref[...], dyb, tn, preferred_element_type=f32)
        prod = do * o
        hi = prod.astype(bf16)
        lo = (prod - hi.astype(f32)).astype(bf16)
        sel = (lax.broadcasted_iota(jnp.int32, (HEADS, MLA_W), 1) // VDIM
               == lax.broadcasted_iota(jnp.int32, (HEADS, MLA_W), 0)).astype(bf16)
        dl_ref[...] = (lax.dot_general(sel, hi, nt, preferred_element_type=f32)
                       + lax.dot_general(sel, lo, nt, preferred_element_type=f32))

    row = lambda wd: pl.BlockSpec((tm, wd), lambda i: (i, 0))
    return pl.pallas_call(
        body, name=name, grid=(T // tm,),
        in_specs=[row(D_MODEL), row(MLA_W), row(MLA_W), row(CONV_W), row(SG_W), _const_spec((1, MLA_W)),
                  _const_spec((D_MODEL, D_MODEL))],
        out_specs=[row(MLA_W), pl.BlockSpec((HEADS, tm), lambda i: (0, i)), row(MLA_W), row(CONV_W), row(SG_W),
                   _const_spec((D_MODEL, D_MODEL)), _const_spec((1, MLA_W))],
        out_shape=[jax.ShapeDtypeStruct((T, MLA_W), bf16), jax.ShapeDtypeStruct((HEADS, T), f32),
                   jax.ShapeDtypeStruct((T, MLA_W), bf16), jax.ShapeDtypeStruct((T, CONV_W), bf16),
                   jax.ShapeDtypeStruct((T, SG_W), bf16), jax.ShapeDtypeStruct((D_MODEL, D_MODEL), f32),
                   jax.ShapeDtypeStruct((1, MLA_W), f32)],
    )(dout, o, proj, ync, yns, g_m, w_out)


def _attn_bwd(q, k, v, do, lse, delta, tb, name):
    T = q.shape[0]
    nb = T // tb
    pw = 2 * HEAD_PAD
    nt = (((1,), (1,)), ((), ()))
    tn = (((0,), (0,)), ((), ()))

    def body(q_ref, do_ref, lse_ref, dl_ref, k_ref, v_ref, dq_ref, dk_ref, dv_ref):
        kj = pl.program_id(1)

        @pl.when(kj == 0)
        def _():
            dq_ref[...] = jnp.zeros_like(dq_ref)

        krow = lax.broadcasted_iota(jnp.int32, (tb, tb), 0)
        qcol = lax.broadcasted_iota(jnp.int32, (tb, tb), 1)
        heads = [slice(e * HEAD_PAD, (e + 1) * HEAD_PAD) for e in range(2)]
        kb = [k_ref[:, hs] for hs in heads]
        vb = [v_ref[:, hs] for hs in heads]

        def step(off, carry, masked):
            dob = do_ref[pl.ds(off, tb), :]
            out = []
            for e, hs in enumerate(heads):
                dk, dv = carry[e]
                qb = q_ref[pl.ds(off, tb), hs]
                s_t = lax.dot_general(kb[e], qb, nt, preferred_element_type=f32)
                p_t = jnp.exp2(s_t - lse_ref[e, :, pl.ds(off, tb)])
                if masked:
                    p_t = jnp.where(krow <= qcol, p_t, 0.0)
                dv = dv + jnp.dot(p_t.astype(bf16), dob, preferred_element_type=f32)
                dp_t = lax.dot_general(vb[e], dob, nt, preferred_element_type=f32)
                ds_t = (p_t * (dp_t - dl_ref[e, :, pl.ds(off, tb)])).astype(bf16)
                dk = dk + jnp.dot(ds_t, qb, preferred_element_type=f32)
                dq_ref[pl.ds(off, tb), hs] += lax.dot_general(ds_t, kb[e], tn, preferred_element_type=f32)
                out.append((dk, dv))
            return tuple(out)

        zero = jnp.zeros((tb, HEAD_PAD), f32)
        carry = step(pl.multiple_of(kj * tb, tb), ((zero, zero), (zero, zero)), True)
        carry = lax.fori_loop(kj + 1, nb, lambda qi, cr: step(pl.multiple_of(qi * tb, tb), cr, False), carry)
        for e, hs in enumerate(heads):
            dk_ref[:, hs] = carry[e][0]
            dv_ref[:, hs] = carry[e][1]

    wide = HEADS * HEAD_PAD
    return pl.pallas_call(
        body, name=name, grid=(HEADS // 2, nb),
        in_specs=[pl.BlockSpec((T, pw), lambda h, j: (0, h)), pl.BlockSpec((T, HEAD_PAD), lambda h, j: (0, h)),
                  pl.BlockSpec((2, 1, T), lambda h, j: (h, 0, 0)), pl.BlockSpec((2, 1, T), lambda h, j: (h, 0, 0)),
                  pl.BlockSpec((tb, pw), lambda h, j: (j, h)), pl.BlockSpec((tb, pw), lambda h, j: (j, h))],
        out_specs=[pl.BlockSpec((T, pw), lambda h, j: (0, h)), pl.BlockSpec((tb, pw), lambda h, j: (j, h)),
                   pl.BlockSpec((tb, pw), lambda h, j: (j, h))],
        out_shape=[jax.ShapeDtypeStruct((T, wide), f32)] * 3,
    )(q, do, lse, delta, k, v)


def _branch_bwd(proj, tabs, bw, dq, dk, dv, dync, dyns, tm, name):
    T = proj.shape[0]
    nt = T // tm
    hb = tm // HALO
    nw = len(BR_W)

    def body(proj_ref, halo_ref, rc_ref, rs1_ref, rs2_ref, *rest):
        wrefs = rest[:nw]
        dq_ref, dk_ref, dv_ref, dync_ref, dyns_ref = rest[nw:nw + 5]
        dp_ref = rest[nw + 5]
        gwrefs = rest[nw + 6:2 * nw + 6]
        carry_ref = rest[2 * nw + 6]
        i = pl.program_id(0)
        r = nt - 1 - i

        @pl.when(i == 0)
        def _():
            carry_ref[...] = jnp.zeros_like(carry_ref)
            for g in gwrefs:
                g[...] = jnp.zeros_like(g)

        c = {"hmask": (r > 0).astype(f32), "rc": rc_ref[...], "rs1": rs1_ref[...], "rs2": rs2_ref[...]}
        _, vjp = jax.vjp(lambda p_, w_: _branch(p_, w_, c), _load_branch_inputs(proj_ref, halo_ref),
                         _load_branch_weights(wrefs))
        cts = (dq_ref[...] * LN2, dk_ref[...] * LN2, dv_ref[...], dync_ref[...].astype(f32),
               dyns_ref[...].astype(f32))
        dp, dw = vjp(cts)
        for n in SEG:
            if n in ("zm", "a", "glu"):
                continue
            dp_ref[:, SEG[n][0] - 512:SEG[n][1] - 512] = dp[n].astype(bf16)
        for n, hn, lo in (("a", "ha", 0), ("glu", "hglu", CONV_W)):
            d = dp[n]
            tail = d[tm - HALO:, :] + carry_ref[:, lo:lo + CONV_W]
            s0 = SEG[n][0] - 512
            dp_ref[0:tm - HALO, s0:s0 + CONV_W] = d[:tm - HALO, :].astype(bf16)
            dp_ref[tm - HALO:tm, s0:s0 + CONV_W] = tail.astype(bf16)
        carry_ref[:, 0:CONV_W] = dp["ha"]
        carry_ref[:, CONV_W:] = dp["hglu"]
        for (n, _), g in zip(BR_W, gwrefs):
            if n in ("sg_w", "sg_b"):
                for gi in range(SG_HEADS):
                    g[gi] += dw[n][gi]
            else:
                g[...] += dw[n]

    row = lambda wd: pl.BlockSpec((tm, wd), lambda i: (nt - 1 - i, 0))
    wide = HEADS * HEAD_PAD
    return pl.pallas_call(
        body, name=name, grid=(nt,),
        in_specs=[row(PROJ_PAD),
                  pl.BlockSpec((HALO, 2 * CONV_W), lambda i: (jnp.maximum((nt - 1 - i) * hb - 1, 0), 1)),
                  row(HEAD_PAD), row(HEAD_PAD), row(HEAD_PAD)] + [_const_spec(s) for _, s in BR_W]
                 + [row(wide), row(wide), row(wide), row(CONV_W), row(SG_W)],
        out_specs=[row(REST)] + [_const_spec(s) for _, s in BR_W],
        out_shape=[jax.ShapeDtypeStruct((T, REST), bf16)] + [jax.ShapeDtypeStruct(s, f32) for _, s in BR_W],
        scratch_shapes=[pltpu.VMEM((HALO, 2 * CONV_W), f32)],
    )(proj, proj, *tabs, *[bw[n] for n, _ in BR_W], dq, dk, dv, dync, dyns)


def _inproj_bwd_dx(x, g, w, dzm, dprest, dout, tm, name):
    T = x.shape[0]
    nt_dims = (((1,), (1,)), ((), ()))

    def body(x_ref, g_ref, w_ref, dzm_ref, dpr_ref, dout_ref, dx_ref, dg_ref, h_ref):
        i = pl.program_id(0)

        @pl.when(i == 0)
        def _():
            dg_ref[...] = jnp.zeros_like(dg_ref)

        dh = lax.dot_general(dzm_ref[...], w_ref[:, 0:512], nt_dims, preferred_element_type=f32)
        dh = dh + lax.dot_general(dpr_ref[...], w_ref[:, 512:], nt_dims, preferred_element_type=f32)
        h, vjp = jax.vjp(_rms, x_ref[...], g_ref[...])
        dx, dg = vjp(dh)
        dx_ref[...] = dout_ref[...] + dx
        dg_ref[...] += dg
        h_ref[...] = h.astype(bf16)

    row = lambda wd: pl.BlockSpec((tm, wd), lambda i: (i, 0))
    return pl.pallas_call(
        body, name=name, grid=(T // tm,),
        in_specs=[row(D_MODEL), _const_spec((1, D_MODEL)), _const_spec((D_MODEL, PROJ_PAD)), row(512), row(REST),
                  row(D_MODEL)],
        out_specs=[row(D_MODEL), _const_spec((1, D_MODEL)), row(D_MODEL)],
        out_shape=[jax.ShapeDtypeStruct((T, D_MODEL), f32), jax.ShapeDtypeStruct((1, D_MODEL), f32),
                   jax.ShapeDtypeStruct((T, D_MODEL), bf16)],
    )(x, g, w, dzm, dprest, dout)


def _inproj_bwd_dw(h, dzm, dprest, tm, name):
    T = h.shape[0]
    tm = min(4 * tm, T)
    nt = T // tm
    cw = REST // 3
    tn = (((0,), (0,)), ((), ()))

    def body_for(nsteps):
        def body(h_ref, d_ref, o_ref):
            i = pl.program_id(1)

            @pl.when(i == 0)
            def _():
                o_ref[...] = jnp.zeros_like(o_ref)

            o_ref[...] += lax.dot_general(h_ref[...], d_ref[...], tn, preferred_element_type=f32)
        return body

    def run(d, width, cwid, nm):
        return pl.pallas_call(
            body_for(nt), name=nm, grid=(width // cwid, nt),
            in_specs=[pl.BlockSpec((tm, D_MODEL), lambda j, i: (i, 0)), pl.BlockSpec((tm, cwid), lambda j, i: (i, j))],
            out_specs=pl.BlockSpec((D_MODEL, cwid), lambda j, i: (0, j)),
            out_shape=jax.ShapeDtypeStruct((D_MODEL, width), f32),
        )(h, d)

    return jnp.concatenate([run(dzm, 512, 512, name + "_zm"), run(dprest, REST, cw, name + "_rest")], axis=1)


def _row_tile(rows):
    return ROW_ALIGN if rows > ROW_ALIGN and rows % ROW_ALIGN == 0 else rows


def _sum4(buf, name):
    _, R, C = buf.shape
    tr = _row_tile(R)

    def body(b_ref, o_ref):
        o_ref[...] = ((b_ref[0] + b_ref[1]) + b_ref[2]) + b_ref[3]

    return pl.pallas_call(
        body, name=name, grid=(R // tr,),
        in_specs=[pl.BlockSpec((4, tr, C), lambda i: (0, i, 0))],
        out_specs=pl.BlockSpec((tr, C), lambda i: (i, 0)),
        out_shape=jax.ShapeDtypeStruct((R, C), f32),
    )(buf)


def _adamw(w, m, v, p_a, p_b, name):
    R, C = w.shape
    tr = _row_tile(R)

    def body(w_ref, m_ref, v_ref, a_ref, b_ref, g_ref, d_ref, nm_ref, nv_ref):
        g = a_ref[...] + b_ref[...]
        m_new = ADAM_B1 * m_ref[...] + (1.0 - ADAM_B1) * g
        v_new = ADAM_B2 * v_ref[...] + (1.0 - ADAM_B2) * (g * g)
        m_hat = m_new / (1.0 - ADAM_B1 ** ADAM_STEP)
        v_hat = v_new / (1.0 - ADAM_B2 ** ADAM_STEP)
        g_ref[...] = g
        d_ref[...] = -ADAM_LR * (m_hat / (jnp.sqrt(v_hat) + ADAM_EPS) + ADAM_WD * w_ref[...])
        nm_ref[...] = m_new
        nv_ref[...] = v_new

    spec = pl.BlockSpec((tr, C), lambda i: (i, 0))
    return pl.pallas_call(
        body, name=name, grid=(R // tr,), in_specs=[spec] * 5, out_specs=[spec] * 4,
        out_shape=[jax.ShapeDtypeStruct((R, C), f32)] * 4,
    )(w, m, v, p_a, p_b)


def _chip_exchange(srcs, per_target, name):
    n = len(srcs)

    def body(*refs):
        src_refs, out_refs = refs[:n], refs[n:2 * n]
        send_sems, recv_sems, local_sems = refs[2 * n:]
        x, y, c = lax.axis_index("x"), lax.axis_index("y"), lax.axis_index("c")
        me = 2 * x + y
        chips = [(1 - x, y), (x, 1 - y), (1 - x, 1 - y)]

        def block_for(i, t):
            return src_refs[i].at[t] if per_target[i] else src_refs[i]

        def remote(i, k, block_t, slot):
            tx, ty = chips[k]
            return pltpu.make_async_remote_copy(
                src_ref=block_for(i, block_t), dst_ref=out_refs[i].at[slot], send_sem=send_sems.at[3 * i + k],
                recv_sem=recv_sems.at[3 * i + k], device_id=(tx, ty, c), device_id_type=pl.DeviceIdType.MESH)

        sends = [remote(i, k, 2 * chips[k][0] + chips[k][1], me) for i in range(n) for k in range(3)]
        for cp in sends:
            cp.start()
        mine = [pltpu.make_async_copy(block_for(i, me), out_refs[i].at[me], local_sems.at[i]) for i in range(n)]
        for cp in mine:
            cp.start()
        for i in range(n):
            for k in range(3):
                remote(i, k, me, 2 * chips[k][0] + chips[k][1]).wait_recv()
        for cp in sends:
            cp.wait_send()
        for cp in mine:
            cp.wait()

    blocks = [s.shape[1:] if p else s.shape for s, p in zip(srcs, per_target)]
    return pl.pallas_call(
        body, name=name,
        in_specs=[pl.BlockSpec(memory_space=pl.ANY)] * n, out_specs=[pl.BlockSpec(memory_space=pl.ANY)] * n,
        out_shape=[jax.ShapeDtypeStruct((4,) + tuple(b), s.dtype) for b, s in zip(blocks, srcs)],
        scratch_shapes=[pltpu.SemaphoreType.DMA((3 * n,)), pltpu.SemaphoreType.DMA((3 * n,)),
                        pltpu.SemaphoreType.DMA((n,))],
    )(*srcs)


def _sibling_exchange(srcs, name):
    n = len(srcs)

    def body(*refs):
        src_refs, out_refs = refs[:n], refs[n:2 * n]
        send_sems, recv_sems = refs[2 * n:]
        x, y, c = lax.axis_index("x"), lax.axis_index("y"), lax.axis_index("c")
        copies = [pltpu.make_async_remote_copy(
            src_ref=src_refs[i], dst_ref=out_refs[i], send_sem=send_sems.at[i], recv_sem=recv_sems.at[i],
            device_id=(x, y, 1 - c), device_id_type=pl.DeviceIdType.MESH) for i in range(n)]
        for cp in copies:
            cp.start()
        for cp in copies:
            cp.wait()

    return pl.pallas_call(
        body, name=name,
        in_specs=[pl.BlockSpec(memory_space=pl.ANY)] * n, out_specs=[pl.BlockSpec(memory_space=pl.ANY)] * n,
        out_shape=[jax.ShapeDtypeStruct(s.shape, s.dtype) for s in srcs],
        scratch_shapes=[pltpu.SemaphoreType.DMA((n,)), pltpu.SemaphoreType.DMA((n,))],
    )(*srcs)


def _pack(arrs, dtype):
    flat = jnp.concatenate([a.reshape(-1).astype(dtype) for a in arrs])
    n = flat.shape[0]
    rows = -(-n // (LANES * ROW_ALIGN)) * ROW_ALIGN
    return jnp.pad(flat, (0, rows * LANES - n)).reshape(rows, LANES)


def _unpack(packed, shapes):
    flat = packed.reshape(-1)
    out, off = [], 0
    for s in shapes:
        n = int(np.prod(s))
        out.append(flat[off:off + n].reshape(s))
        off += n
    return out


def _pad_w_in(w):
    z = lambda n: jnp.zeros((D_MODEL, n), w.dtype)
    return jnp.concatenate([w[:, 1824:2336], w[:, 0:1792], z(64), w[:, 1792:1824], z(32), w[:, 2336:]], axis=1)


def _unpad_w_in(d):
    return jnp.concatenate([d[:, 512:2304], d[:, 2368:2400], d[:, 0:512], d[:, 2432:]], axis=1)


def _v_cols():
    return [(h % 2) * VDIM for h in range(HEADS)]


def _layer_weights(full, small, l):
    w_uq = full["w_uq"][l].astype(f32).reshape(Q_LORA, HEADS, QK)
    w_uq = jnp.pad(w_uq, ((0, 0), (0, 0), (0, HEAD_PAD - QK))).reshape(Q_LORA, HEADS * HEAD_PAD)
    ukv = full["w_ukv"][l].astype(f32).reshape(KV_LORA, HEADS, NOPE + VDIM)
    w_k = jnp.pad(ukv[:, :, :NOPE], ((0, 0), (0, 0), (0, HEAD_PAD - NOPE))).reshape(KV_LORA, HEADS * HEAD_PAD)
    zeros = jnp.zeros((KV_LORA, VDIM), f32)
    w_v = jnp.concatenate(
        [jnp.concatenate([ukv[:, h, NOPE:], zeros] if h % 2 == 0 else [zeros, ukv[:, h, NOPE:]], axis=1)
         for h in range(HEADS)], axis=1)
    row = lambda a: a.reshape(1, -1)
    bn = small["branch_norm_g"][l]
    bw = {
        "conv_w": jnp.pad(full["conv_w"][l], ((0, 1), (0, 0))), "conv_b": row(small["conv_b"][l]),
        "cln_g": row(small["conv_ln_g"][l]), "cln_b": row(small["conv_ln_b"][l]),
        "pw_w": full["conv_pw_w"][l].astype(f32), "pw_b": row(small["conv_pw_b"][l]),
        "qn_g": row(small["q_norm_g"][l]), "w_uq": w_uq, "kvn_g": row(small["kv_norm_g"][l]),
        "w_k": w_k, "w_v": w_v,
        "qkq_g": jnp.pad(row(small["qk_q_g"][l]), ((0, 0), (0, HEAD_PAD - QK))),
        "qkk_g": jnp.pad(row(small["qk_k_g"][l]), ((0, 0), (0, HEAD_PAD - QK))),
        "sln_g": row(small["sg_ln_g"][l]), "sln_b": row(small["sg_ln_b"][l]),
        "sg_w": small["sg_w"][l], "sg_b": small["sg_b"][l].reshape(SG_HEADS, SG_CHUNK, 1),
        "bn_c": row(bn[:CONV_W]), "bn_s": row(bn[CONV_W + MLA_W:]),
    }
    return {
        "norm_g": row(small["norm_g"][l]), "w_in": _pad_w_in(full["w_in"][l]), "bw": bw,
        "bn_m": row(bn[CONV_W:CONV_W + MLA_W]), "w_out": full["w_out"][l],
    }


def _layer_grads(gb, dw_in_pad, dnorm_g, dw_out, dbn_m):
    duq = gb["w_uq"].reshape(Q_LORA, HEADS, HEAD_PAD)[:, :, :QK].reshape(Q_LORA, HEADS * QK)
    dk = gb["w_k"].reshape(KV_LORA, HEADS, HEAD_PAD)[:, :, :NOPE]
    dv = gb["w_v"].reshape(KV_LORA, HEADS, HEAD_PAD)
    dv = jnp.stack([dv[:, h, c0:c0 + VDIM] for h, c0 in enumerate(_v_cols())], axis=1)
    dukv = jnp.concatenate([dk, dv], axis=2).reshape(KV_LORA, HEADS * (NOPE + VDIM))
    return {
        "norm_g": dnorm_g[0], "w_in": _unpad_w_in(dw_in_pad), "conv_w": gb["conv_w"][:CONV_K],
        "conv_b": gb["conv_b"][0], "conv_ln_g": gb["cln_g"][0], "conv_ln_b": gb["cln_b"][0],
        "conv_pw_w": gb["pw_w"], "conv_pw_b": gb["pw_b"][0], "q_norm_g": gb["qn_g"][0], "w_uq": duq,
        "kv_norm_g": gb["kvn_g"][0], "w_ukv": dukv, "qk_q_g": gb["qkq_g"][0, :QK], "qk_k_g": gb["qkk_g"][0, :QK],
        "sg_ln_g": gb["sln_g"][0], "sg_ln_b": gb["sln_b"][0], "sg_w": gb["sg_w"], "sg_b": gb["sg_b"][:, :, 0],
        "branch_norm_g": jnp.concatenate([gb["bn_c"][0], dbn_m[0], gb["bn_s"][0]]), "w_out": dw_out,
    }


def _rope_tables(T):
    half = ROPE // 2
    inv_freq = ROPE_THETA ** (-jnp.arange(half, dtype=f32) / half)
    ang = jnp.arange(T, dtype=f32)[:, None] * inv_freq[None, :]
    cos, sin = jnp.cos(ang), jnp.sin(ang)
    one = jnp.ones((T, NOPE), f32)
    z = lambda n: jnp.zeros((T, n), f32)
    rc = jnp.concatenate([one, cos, cos, jnp.ones((T, HEAD_PAD - QK), f32)], axis=1)
    rs1 = jnp.concatenate([z(NOPE), -sin, z(half), z(HEAD_PAD - QK)], axis=1)
    rs2 = jnp.concatenate([z(NOPE), z(half), sin, z(HEAD_PAD - QK)], axis=1)
    return rc, rs1, rs2


def kernel(x, norm_g, w_in, conv_w, conv_b, conv_ln_g, conv_ln_b, conv_pw_w, conv_pw_b, q_norm_g, w_uq, kv_norm_g, w_ukv, qk_q_g, qk_k_g, sg_ln_g, sg_ln_b, sg_w, sg_b, branch_norm_g, w_out, loss_target, m_norm_g, m_w_in, m_conv_w, m_conv_b, m_conv_ln_g, m_conv_ln_b, m_conv_pw_w, m_conv_pw_b, m_q_norm_g, m_w_uq, m_kv_norm_g, m_w_ukv, m_qk_q_g, m_qk_k_g, m_sg_ln_g, m_sg_ln_b, m_sg_w, m_sg_b, m_branch_norm_g, m_w_out, v_norm_g, v_w_in, v_conv_w, v_conv_b, v_conv_ln_g, v_conv_ln_b, v_conv_pw_w, v_conv_pw_b, v_q_norm_g, v_w_uq, v_kv_norm_g, v_w_ukv, v_qk_q_g, v_qk_k_g, v_sg_ln_g, v_sg_ln_b, v_sg_w, v_sg_b, v_branch_norm_g, v_w_out):
    wts = dict(zip(WEIGHTS, [norm_g, w_in, conv_w, conv_b, conv_ln_g, conv_ln_b, conv_pw_w, conv_pw_b, q_norm_g,
                             w_uq, kv_norm_g, w_ukv, qk_q_g, qk_k_g, sg_ln_g, sg_ln_b, sg_w, sg_b, branch_norm_g,
                             w_out]))
    mom_m = dict(zip(WEIGHTS, [m_norm_g, m_w_in, m_conv_w, m_conv_b, m_conv_ln_g, m_conv_ln_b, m_conv_pw_w,
                               m_conv_pw_b, m_q_norm_g, m_w_uq, m_kv_norm_g, m_w_ukv, m_qk_q_g, m_qk_k_g,
                               m_sg_ln_g, m_sg_ln_b, m_sg_w, m_sg_b, m_branch_norm_g, m_w_out]))
    mom_v = dict(zip(WEIGHTS, [v_norm_g, v_w_in, v_conv_w, v_conv_b, v_conv_ln_g, v_conv_ln_b, v_conv_pw_w,
                               v_conv_pw_b, v_q_norm_g, v_w_uq, v_kv_norm_g, v_w_ukv, v_qk_q_g, v_qk_k_g,
                               v_sg_ln_g, v_sg_ln_b, v_sg_w, v_sg_b, v_branch_norm_g, v_w_out]))
    xs = x[0]
    tgt = loss_target[0]
    T = xs.shape[0]
    tm = min(256, T)
    tb = min(512, T // 2)

    big = list(BIG)
    shards = [wts[n] if n == "conv_w" else wts[n].astype(bf16) for n in big]
    gathered = _chip_exchange(shards, [False] * len(big), "gather_weights")
    full = {n: jnp.concatenate([g[j] for j in range(4)], axis=BIG[n]) for n, g in zip(big, gathered)}
    small = {n: wts[n] for n in SMALL}

    tabs = _rope_tables(T)
    lw = [_layer_weights(full, small, l) for l in range(DEPTH)]

    acts = []
    h_in = xs
    for l in range(DEPTH):
        W = lw[l]
        proj = _inproj_fwd(h_in, W["norm_g"], W["w_in"], tm, f"inproj_fwd_{l}")
        q, k, v, ync, yns = _branch_fwd(proj, tabs, W["bw"], tm, f"branch_fwd_{l}")
        o, lse = _attn_fwd(q, k, v, tb, f"attn_fwd_{l}")
        y = _out_fwd(h_in, o, proj, ync, yns, W["bn_m"], W["w_out"], tm, f"out_fwd_{l}")
        acts.append((h_in, proj, q, k, v, ync, yns, o, lse))
        h_in = y

    dy, loss_row = _loss_grad(h_in, tgt, tm, "loss_grad")
    loss = lax.psum(loss_row[0, 0], ("x", "y", "c"))

    grads = [None] * DEPTH
    for l in reversed(range(DEPTH)):
        W = lw[l]
        xin, proj, q, k, v, ync, yns, o, lse = acts[l]
        do, delta, dzm, dync, dyns, dw_out, dbn_m = _out_bwd(
            dy, o, proj, ync, yns, W["bn_m"], W["w_out"], tm, f"out_bwd_{l}")
        dq, dk, dv = _attn_bwd(q, k, v, do, lse, delta.reshape(HEADS, 1, T), tb, f"attn_bwd_{l}")
        outs = _branch_bwd(proj, tabs, W["bw"], dq, dk, dv, dync, dyns, tm, f"branch_bwd_{l}")
        dprest, gb = outs[0], dict(zip([n for n, _ in BR_W], outs[1:]))
        dx, dnorm_g, hb = _inproj_bwd_dx(xin, W["norm_g"], W["w_in"], dzm, dprest, dy, tm, f"inproj_bwd_dx_{l}")
        dw_in_pad = _inproj_bwd_dw(hb, dzm, dprest, tm, f"inproj_bwd_dw_{l}")
        grads[l] = _layer_grads(gb, dw_in_pad, dnorm_g, dw_out, dbn_m)
        dy = dx
    grad_x = dy[None]
    g_full = {n: jnp.stack([grads[l][n] for l in range(DEPTH)]) for n in WEIGHTS}

    view = lambda a: a.reshape(-1, a.shape[-1])
    per_chip = [jnp.stack([view(s) for s in jnp.split(g_full[n], 4, axis=BIG[n])]) for n in big]
    contrib = _chip_exchange(per_chip + [_pack([g_full[n] for n in SMALL], f32)], [True] * len(big) + [False],
                             "exchange_grads")
    parts = [_sum4(cb, f"sum_chips_{i}") for i, cb in enumerate(contrib)]
    others = _sibling_exchange(parts, "exchange_cores")

    res = {kind: {} for kind in ("grad", "delta", "new_m", "new_v")}
    small_shapes = [wts[n].shape for n in SMALL]
    for i, n in enumerate(big + ["small"]):
        if n == "small":
            w3 = [_pack([d[s] for s in SMALL], f32) for d in (wts, mom_m, mom_v)]
        else:
            w3 = [view(d[n]) for d in (wts, mom_m, mom_v)]
        outs = _adamw(*w3, parts[i], others[i], f"adamw_{n}")
        for kind, o in zip(res, outs):
            if n == "small":
                res[kind].update(zip(SMALL, _unpack(o, small_shapes)))
            else:
                res[kind][n] = o.reshape(wts[n].shape)
    out = [loss, grad_x]
    for kind in ("grad", "delta", "new_m", "new_v"):
        out += [res[kind][n] for n in WEIGHTS]
    return tuple(out)
```

```python
import jax
import jax.numpy as jnp
from jax import lax
from jax.experimental import pallas as pl
from jax.experimental.pallas import tpu as pltpu

f32 = jnp.float32
bf16 = jnp.bfloat16

DEPTH = 2
D_MODEL = 1024
CONV_W = 256
CONV_K = 31
HEADS = 8
NOPE = 64
ROPE = 32
QK = NOPE + ROPE
VDIM = 64
MLA_W = HEADS * VDIM
Q_LORA = 768
KV_LORA = 256
SG_W = 256
SG_HEADS = 4
SG_CHUNK = 128
IN_COLS = 3104
ROPE_THETA = 10000.0
EPS = 1e-6
HEAD_PAD = 128
LOG2E = 1.4426950408889634
LN2 = 0.6931471805599453
Q_SCALE = QK ** -0.5 * LOG2E
HALO = 32
ROW_ALIGN = 256

ADAM_LR = 0.001
ADAM_B1 = 0.9
ADAM_B2 = 0.999
ADAM_EPS = 1e-08
ADAM_WD = 0.01
ADAM_STEP = 10

PROJ_PAD = 3200
SEG = {
    "zm": (0, 512), "a": (512, 768), "glu": (768, 1024), "zc": (1024, 1280), "cq": (1280, 2048),
    "ckv": (2048, 2304), "kr": (2304, 2432), "us": (2432, 2688), "vs": (2688, 2944), "zs": (2944, 3200),
}
REST = PROJ_PAD - 512

WEIGHTS = ["norm_g", "w_in", "conv_w", "conv_b", "conv_ln_g", "conv_ln_b", "conv_pw_w", "conv_pw_b", "q_norm_g",
           "w_uq", "kv_norm_g", "w_ukv", "qk_q_g", "qk_k_g", "sg_ln_g", "sg_ln_b", "sg_w", "sg_b",
           "branch_norm_g", "w_out"]
BIG = {"w_in": 2, "conv_w": 2, "conv_pw_w": 1, "w_uq": 1, "w_ukv": 2, "w_out": 1}
SMALL = [n for n in WEIGHTS if n not in BIG]

BR_W = [("conv_w", (32, 256)), ("conv_b", (1, 256)), ("cln_g", (1, 256)), ("cln_b", (1, 256)),
        ("pw_w", (256, 256)), ("pw_b", (1, 256)), ("qn_g", (1, 768)), ("w_uq", (768, 1024)),
        ("kvn_g", (1, 256)), ("w_k", (256, 1024)), ("w_v", (256, 1024)), ("qkq_g", (1, 128)),
        ("qkk_g", (1, 128)), ("sln_g", (1, 256)), ("sln_b", (1, 256)), ("sg_w", (4, 128, 128)),
        ("sg_b", (4, 128, 1)), ("bn_c", (1, 256)), ("bn_s", (1, 256))]


@jax.custom_vjp
def _mm(a, w):
    return jnp.dot(a.astype(bf16), w.astype(bf16), preferred_element_type=f32)


def _mm_fwd(a, w):
    return _mm(a, w), (a, w)


def _mm_bwd(res, ct):
    a, w = res
    ctb = ct.astype(bf16)
    da = lax.dot_general(ctb, w.astype(bf16), (((1,), (1,)), ((), ())), preferred_element_type=f32)
    dw = lax.dot_general(a.astype(bf16), ctb, (((0,), (0,)), ((), ())), preferred_element_type=f32)
    return da.astype(a.dtype), dw.astype(w.dtype)


_mm.defvjp(_mm_fwd, _mm_bwd)


@jax.custom_vjp
def _rope(x, c, s1, s2):
    return x * c + pltpu.roll(x, HEAD_PAD - 16, 1) * s1 + pltpu.roll(x, 16, 1) * s2


def _rope_fwd(x, c, s1, s2):
    return _rope(x, c, s1, s2), (c, s1, s2)


def _rope_bwd(res, ct):
    c, s1, s2 = res
    dx = ct * c + pltpu.roll(ct * s1, 16, 1) + pltpu.roll(ct * s2, HEAD_PAD - 16, 1)
    return dx, jnp.zeros_like(c), jnp.zeros_like(s1), jnp.zeros_like(s2)


_rope.defvjp(_rope_fwd, _rope_bwd)


def _rms(x, g):
    return x * lax.rsqrt(jnp.mean(x * x, axis=-1, keepdims=True) + EPS) * g


def _head_rms(x, g):
    return x * lax.rsqrt(jnp.sum(x * x, axis=-1, keepdims=True) * (1.0 / QK) + EPS) * g


def _ln(x, g, b):
    mu = jnp.mean(x, axis=-1, keepdims=True)
    xc = x - mu
    var = jnp.mean(xc * xc, axis=-1, keepdims=True)
    return xc * lax.rsqrt(var + EPS) * g + b


def _branch(p, w, c):
    tm = p["a"].shape[0]
    yg = p["a"] * jax.nn.sigmoid(p["glu"])
    yh = p["ha"] * jax.nn.sigmoid(p["hglu"]) * c["hmask"]
    ycat = jnp.concatenate([yh, yg], axis=0)
    shifted = [ycat] + [ycat[b:b + tm + HALO - 8, :] for b in range(1, 8)]
    acc = jnp.zeros((tm, CONV_W), f32)
    for k in range(CONV_K):
        off = HALO - (CONV_K - 1) + k
        a8 = off - off % 8
        acc = acc + shifted[off % 8][a8:a8 + tm, :] * w["conv_w"][k:k + 1, :]
    yl = jax.nn.silu(_ln(acc + w["conv_b"], w["cln_g"], w["cln_b"]))
    ypw = _mm(yl, w["pw_w"]) + w["pw_b"]
    ync = _rms(ypw * jax.nn.silu(p["zc"]), w["bn_c"])
    cqn = _rms(p["cq"], w["qn_g"])
    qf = _mm(cqn, w["w_uq"])
    ckvn = _rms(p["ckv"], w["kvn_g"])
    kf = _mm(ckvn, w["w_k"])
    v = _mm(ckvn, w["w_v"])
    qs, ks = [], []
    for h in range(HEADS):
        sl = slice(h * HEAD_PAD, (h + 1) * HEAD_PAD)
        qh = _head_rms(qf[:, sl], w["qkq_g"])
        qs.append(_rope(qh, c["rc"], c["rs1"], c["rs2"]) * Q_SCALE)
        kh = _head_rms(kf[:, sl] + p["kr"], w["qkk_g"])
        ks.append(_rope(kh, c["rc"], c["rs1"], c["rs2"]))
    q = jnp.concatenate(qs, axis=-1)
    k = jnp.concatenate(ks, axis=-1)
    u = jax.nn.gelu(p["us"])
    v2 = _ln(jax.nn.gelu(p["vs"]), w["sln_g"], w["sln_b"])
    r = lax.broadcasted_iota(jnp.int32, (SG_CHUNK, SG_CHUNK), 0)
    cc = lax.broadcasted_iota(jnp.int32, (SG_CHUNK, SG_CHUNK), 1)
    lane = lax.broadcasted_iota(jnp.int32, (1, SG_W), 1) // (SG_W // SG_HEADS)
    wm = [jnp.where(r >= cc, w["sg_w"][g], 0.0) for g in range(SG_HEADS)]
    rows = []
    for ci in range(tm // SG_CHUNK):
        vc = v2[ci * SG_CHUNK:(ci + 1) * SG_CHUNK, :]
        mixed = jnp.zeros((SG_CHUNK, SG_W), f32)
        for g in range(SG_HEADS):
            mixed = mixed + jnp.where(lane == g, _mm(wm[g], vc) + w["sg_b"][g], 0.0)
        rows.append(mixed)
    mixed = jnp.concatenate(rows, axis=0) if len(rows) > 1 else rows[0]
    yns = _rms(u * mixed * jax.nn.silu(p["zs"]), w["bn_s"])
    return q, k, v, ync, yns


def _mla_out(o, zm, g):
    return _rms(o * jax.nn.silu(zm), g)


def _load_branch_inputs(proj_ref, halo_ref):
    p = {n: proj_ref[:, SEG[n][0]:SEG[n][1]].astype(f32) for n in SEG if n != "zm"}
    p["ha"] = halo_ref[:, 0:CONV_W].astype(f32)
    p["hglu"] = halo_ref[:, CONV_W:2 * CONV_W].astype(f32)
    return p


def _load_branch_weights(refs):
    w = {}
    for (n, _), r in zip(BR_W, refs):
        if n in ("sg_w", "sg_b"):
            w[n] = [r[g] for g in range(SG_HEADS)]
        else:
            w[n] = r[...]
    return w


def _const_spec(shape):
    nd = len(shape)
    return pl.BlockSpec(shape, lambda *_: (0,) * nd)


def _inproj_fwd(x, g, w, tm, name):
    T = x.shape[0]
    nc = 5
    cw = PROJ_PAD // nc

    def body(x_ref, g_ref, w_ref, o_ref):
        h = _rms(x_ref[...], g_ref[...]).astype(bf16)
        for j in range(nc):
            o_ref[:, j * cw:(j + 1) * cw] = jnp.dot(
                h, w_ref[:, j * cw:(j + 1) * cw], preferred_element_type=f32).astype(bf16)

    return pl.pallas_call(
        body, name=name, grid=(T // tm,),
        in_specs=[pl.BlockSpec((tm, D_MODEL), lambda i: (i, 0)), _const_spec((1, D_MODEL)),
                  _const_spec((D_MODEL, PROJ_PAD))],
        out_specs=pl.BlockSpec((tm, PROJ_PAD), lambda i: (i, 0)),
        out_shape=jax.ShapeDtypeStruct((T, PROJ_PAD), bf16),
    )(x, g, w)


def _branch_fwd(proj, tabs, bw, tm, name):
    T = proj.shape[0]
    hb = tm // HALO

    def body(proj_ref, halo_ref, rc_ref, rs1_ref, rs2_ref, *rest):
        wrefs, (q_ref, k_ref, v_ref, ync_ref, yns_ref) = rest[:len(BR_W)], rest[len(BR_W):]
        i = pl.program_id(0)
        c = {"hmask": (i > 0).astype(f32), "rc": rc_ref[...], "rs1": rs1_ref[...], "rs2": rs2_ref[...]}
        q, k, v, ync, yns = _branch(_load_branch_inputs(proj_ref, halo_ref), _load_branch_weights(wrefs), c)
        q_ref[...] = q.astype(bf16)
        k_ref[...] = k.astype(bf16)
        v_ref[...] = v.astype(bf16)
        ync_ref[...] = ync.astype(bf16)
        yns_ref[...] = yns.astype(bf16)

    row = lambda wd: pl.BlockSpec((tm, wd), lambda i: (i, 0))
    wide = HEADS * HEAD_PAD
    return pl.pallas_call(
        body, name=name, grid=(T // tm,),
        in_specs=[row(PROJ_PAD), pl.BlockSpec((HALO, 2 * CONV_W), lambda i: (jnp.maximum(i * hb - 1, 0), 1)),
                  row(HEAD_PAD), row(HEAD_PAD), row(HEAD_PAD)] + [_const_spec(s) for _, s in BR_W],
        out_specs=[row(wide), row(wide), row(wide), row(CONV_W), row(SG_W)],
        out_shape=[jax.ShapeDtypeStruct((T, wide), bf16)] * 3 + [jax.ShapeDtypeStruct((T, CONV_W), bf16)] * 2,
    )(proj, proj, *tabs, *[bw[n] for n, _ in BR_W])


def _attn_fwd(q, k, v, tb, name):
    T = q.shape[0]
    nb = T // tb
    pw = 2 * HEAD_PAD

    def body(q_ref, k_ref, v_ref, o_ref, lse_ref):
        qi = pl.program_id(1)
        krow = lax.broadcasted_iota(jnp.int32, (tb, tb), 0)
        qcol = lax.broadcasted_iota(jnp.int32, (tb, tb), 1)
        heads = [slice(e * HEAD_PAD, (e + 1) * HEAD_PAD) for e in range(2)]
        qh = [q_ref[:, hs] for hs in heads]

        def step(off, carry, masked):
            out = []
            for e, hs in enumerate(heads):
                m, l, acc = carry[e]
                kb = k_ref[pl.ds(off, tb), hs]
                vb = v_ref[pl.ds(off, tb), hs]
                s_t = lax.dot_general(kb, qh[e], (((1,), (1,)), ((), ())), preferred_element_type=f32)
                if masked:
                    s_t = jnp.where(krow <= qcol, s_t, -1e30)
                m_new = jnp.maximum(m, jnp.max(s_t, axis=0, keepdims=True))
                p_t = jnp.exp2(s_t - m_new)
                alpha = jnp.exp2(m - m_new)
                l = alpha * l + jnp.sum(p_t, axis=0, keepdims=True)
                acc = alpha * acc + lax.dot_general(vb, p_t.astype(bf16), (((0,), (0,)), ((), ())),
                                                    preferred_element_type=f32)
                out.append((m_new, l, acc))
            return tuple(out)

        one = (jnp.full((1, tb), -1e30, f32), jnp.zeros((1, tb), f32), jnp.zeros((HEAD_PAD, tb), f32))
        carry = lax.fori_loop(0, qi, lambda ki, cr: step(pl.multiple_of(ki * tb, tb), cr, False), (one, one))
        carry = step(pl.multiple_of(qi * tb, tb), carry, True)
        o_t = jnp.zeros((HEAD_PAD, tb), f32)
        for e in range(2):
            m, l, acc = carry[e]
            o_t = o_t + acc / l
            lse_ref[e] = m + jnp.log2(l)
        o_ref[...] = o_t.T

    return pl.pallas_call(
        body, name=name, grid=(HEADS // 2, nb),
        in_specs=[pl.BlockSpec((tb, pw), lambda h, i: (i, h)), pl.BlockSpec((T, pw), lambda h, i: (0, h)),
                  pl.BlockSpec((T, pw), lambda h, i: (0, h))],
        out_specs=[pl.BlockSpec((tb, HEAD_PAD), lambda h, i: (i, h)), pl.BlockSpec((2, 1, tb), lambda h, i: (h, 0, i))],
        out_shape=[jax.ShapeDtypeStruct((T, MLA_W), f32), jax.ShapeDtypeStruct((HEADS, 1, T), f32)],
    )(q, k, v)


def _out_fwd(x, o, proj, ync, yns, g_m, w_out, tm, name):
    T = x.shape[0]

    def body(x_ref, o_ref, zm_ref, ync_ref, yns_ref, g_ref, w_ref, y_ref):
        ynm = _mla_out(o_ref[...], zm_ref[...].astype(f32), g_ref[...]).astype(bf16)
        y = x_ref[...]
        y = y + jnp.dot(ync_ref[...], w_ref[0:CONV_W, :], preferred_element_type=f32)
        y = y + jnp.dot(ynm, w_ref[CONV_W:CONV_W + MLA_W, :], preferred_element_type=f32)
        y = y + jnp.dot(yns_ref[...], w_ref[CONV_W + MLA_W:, :], preferred_element_type=f32)
        y_ref[...] = y

    row = lambda wd: pl.BlockSpec((tm, wd), lambda i: (i, 0))
    return pl.pallas_call(
        body, name=name, grid=(T // tm,),
        in_specs=[row(D_MODEL), row(MLA_W), row(MLA_W), row(CONV_W), row(SG_W), _const_spec((1, MLA_W)),
                  _const_spec((D_MODEL, D_MODEL))],
        out_specs=row(D_MODEL),
        out_shape=jax.ShapeDtypeStruct((T, D_MODEL), f32),
    )(x, o, proj, ync, yns, g_m, w_out)


def _loss_grad(y, tgt, tm, name):
    T = y.shape[0]
    nt = T // tm

    def body(y_ref, t_ref, dy_ref, loss_ref, acc_ref):
        i = pl.program_id(0)

        @pl.when(i == 0)
        def _():
            acc_ref[...] = jnp.zeros_like(acc_ref)

        err = y_ref[...] - t_ref[...]
        dy_ref[...] = err * (1.0 / D_MODEL)
        acc_ref[...] += jnp.sum(err * err, axis=0, keepdims=True)

        @pl.when(i == nt - 1)
        def _():
            loss_ref[...] = jnp.full((1, HEAD_PAD), 0.5 / D_MODEL, f32) * jnp.sum(acc_ref[...])

    row = pl.BlockSpec((tm, D_MODEL), lambda i: (i, 0))
    return pl.pallas_call(
        body, name=name, grid=(nt,), in_specs=[row, row],
        out_specs=[row, _const_spec((1, HEAD_PAD))],
        out_shape=[jax.ShapeDtypeStruct((T, D_MODEL), f32), jax.ShapeDtypeStruct((1, HEAD_PAD), f32)],
        scratch_shapes=[pltpu.VMEM((1, D_MODEL), f32)],
    )(y, tgt)


def _out_bwd(dout, o, proj, ync, yns, g_m, w_out, tm, name):
    T = dout.shape[0]

    def body(dy_ref, o_ref, zm_ref, ync_ref, yns_ref, g_ref, w_ref,
             do_ref, dl_ref, dzm_ref, dync_ref, dyns_ref, dw_ref, dg_ref):
        i = pl.program_id(0)

        @pl.when(i == 0)
        def _():
            dw_ref[...] = jnp.zeros_like(dw_ref)
            dg_ref[...] = jnp.zeros_like(dg_ref)

        dyb = dy_ref[...].astype(bf16)
        nt = (((1,), (1,)), ((), ()))
        tn = (((0,), (0,)), ((), ()))
        d_c = lax.dot_general(dyb, w_ref[0:CONV_W, :], nt, preferred_element_type=f32)
        d_m = lax.dot_general(dyb, w_ref[CONV_W:CONV_W + MLA_W, :], nt, preferred_element_type=f32)
        d_s = lax.dot_general(dyb, w_ref[CONV_W + MLA_W:, :], nt, preferred_element_type=f32)
        o = o_ref[...]
        ynm, vjp = jax.vjp(_mla_out, o, zm_ref[...].astype(f32), g_ref[...])
        do, dzm, dg = vjp(d_m)
        do_ref[...] = do.astype(bf16)
        dzm_ref[...] = dzm.astype(bf16)
        dync_ref[...] = d_c.astype(bf16)
        dyns_ref[...] = d_s.astype(bf16)
        dg_ref[...] += dg
        dw_ref[0:CONV_W, :] += lax.dot_general(ync_ref[...], dyb, tn, preferred_element_type=f32)
        dw_ref[CONV_W:CONV_W + MLA_W, :] += lax.dot_general(ynm.astype(bf16), dyb, tn, preferred_element_type=f32)
        dw_ref[CONV_W + MLA_W:, :] += lax.dot_general(yns_ref[...], dyb, tn, preferred_element_type=f32)
        prod = do * o
        hi = prod.astype(bf16)
        lo = (prod - hi.astype(f32)).astype(bf16)
        sel = (lax.broadcasted_iota(jnp.int32, (HEADS, MLA_W), 1) // VDIM
               == lax.broadcasted_iota(jnp.int32, (HEADS, MLA_W), 0)).astype(bf16)
        dl_ref[...] = (lax.dot_general(sel, hi, nt, preferred_element_type=f32)
                       + lax.dot_general(sel, lo, nt, preferred_element_type=f32))

    row = lambda wd: pl.BlockSpec((tm, wd), lambda i: (i, 0))
    return pl.pallas_call(
        body, name=name, grid=(T // tm,),
        in_specs=[row(D_MODEL), row(MLA_W), row(MLA_W), row(CONV_W), row(SG_W), _const_spec((1, MLA_W)),
                  _const_spec((D_MODEL, D_MODEL))],
        out_specs=[row(MLA_W), pl.BlockSpec((HEADS, tm), lambda i: (0, i)), row(MLA_W), row(CONV_W), row(SG_W),
                   _const_spec((D_MODEL, D_MODEL)), _const_spec((1, MLA_W))],
        out_shape=[jax.ShapeDtypeStruct((T, MLA_W), bf16), jax.ShapeDtypeStruct((HEADS, T), f32),
                   jax.ShapeDtypeStruct((T, MLA_W), bf16), jax.ShapeDtypeStruct((T, CONV_W), bf16),
                   jax.ShapeDtypeStruct((T, SG_W), bf16), jax.ShapeDtypeStruct((D_MODEL, D_MODEL), f32),
                   jax.ShapeDtypeStruct((1, MLA_W), f32)],
    )(dout, o, proj, ync, yns, g_m, w_out)


def _attn_bwd(q, k, v, do, lse, delta, tb, name):
    T = q.shape[0]
    nb = T // tb
    pw = 2 * HEAD_PAD
    nt = (((1,), (1,)), ((), ()))
    tn = (((0,), (0,)), ((), ()))

    def body(q_ref, do_ref, lse_ref, dl_ref, k_ref, v_ref, dq_ref, dk_ref, dv_ref):
        kj = pl.program_id(1)

        @pl.when(kj == 0)
        def _():
            dq_ref[...] = jnp.zeros_like(dq_ref)

        krow = lax.broadcasted_iota(jnp.int32, (tb, tb), 0)
        qcol = lax.broadcasted_iota(jnp.int32, (tb, tb), 1)
        heads = [slice(e * HEAD_PAD, (e + 1) * HEAD_PAD) for e in range(2)]
        kb = [k_ref[:, hs] for hs in heads]
        vb = [v_ref[:, hs] for hs in heads]

        def step(off, carry, masked):
            dob = do_ref[pl.ds(off, tb), :]
            out = []
            for e, hs in enumerate(heads):
                dk, dv = carry[e]
                qb = q_ref[pl.ds(off, tb), hs]
                s_t = lax.dot_general(kb[e], qb, nt, preferred_element_type=f32)
                p_t = jnp.exp2(s_t - lse_ref[e, :, pl.ds(off, tb)])
                if masked:
                    p_t = jnp.where(krow <= qcol, p_t, 0.0)
                dv = dv + jnp.dot(p_t.astype(bf16), dob, preferred_element_type=f32)
                dp_t = lax.dot_general(vb[e], dob, nt, preferred_element_type=f32)
                ds_t = (p_t * (dp_t - dl_ref[e, :, pl.ds(off, tb)])).astype(bf16)
                dk = dk + jnp.dot(ds_t, qb, preferred_element_type=f32)
                dq_ref[pl.ds(off, tb), hs] += lax.dot_general(ds_t, kb[e], tn, preferred_element_type=f32)
                out.append((dk, dv))
            return tuple(out)

        zero = jnp.zeros((tb, HEAD_PAD), f32)
        carry = step(pl.multiple_of(kj * tb, tb), ((zero, zero), (zero, zero)), True)
        carry = lax.fori_loop(kj + 1, nb, lambda qi, cr: step(pl.multiple_of(qi * tb, tb), cr, False), carry)
        for e, hs in enumerate(heads):
            dk_ref[:, hs] = carry[e][0]
            dv_ref[:, hs] = carry[e][1]

    wide = HEADS * HEAD_PAD
    return pl.pallas_call(
        body, name=name, grid=(HEADS // 2, nb),
        in_specs=[pl.BlockSpec((T, pw), lambda h, j: (0, h)), pl.BlockSpec((T, HEAD_PAD), lambda h, j: (0, h)),
                  pl.BlockSpec((2, 1, T), lambda h, j: (h, 0, 0)), pl.BlockSpec((2, 1, T), lambda h, j: (h, 0, 0)),
                  pl.BlockSpec((tb, pw), lambda h, j: (j, h)), pl.BlockSpec((tb, pw), lambda h, j: (j, h))],
        out_specs=[pl.BlockSpec((T, pw), lambda h, j: (0, h)), pl.BlockSpec((tb, pw), lambda h, j: (j, h)),
                   pl.BlockSpec((tb, pw), lambda h, j: (j, h))],
        out_shape=[jax.ShapeDtypeStruct((T, wide), f32)] * 3,
    )(q, do, lse, delta, k, v)


def _branch_bwd(proj, tabs, bw, dq, dk, dv, dync, dyns, tm, name):
    T = proj.shape[0]
    nt = T // tm
    hb = tm // HALO
    nw = len(BR_W)

    def body(proj_ref, halo_ref, rc_ref, rs1_ref, rs2_ref, *rest):
        wrefs = rest[:nw]
        dq_ref, dk_ref, dv_ref, dync_ref, dyns_ref = rest[nw:nw + 5]
        dp_ref = rest[nw + 5]
        gwrefs = rest[nw + 6:2 * nw + 6]
        carry_ref = rest[2 * nw + 6]
        i = pl.program_id(0)
        r = nt - 1 - i

        @pl.when(i == 0)
        def _():
            carry_ref[...] = jnp.zeros_like(carry_ref)
            for g in gwrefs:
                g[...] = jnp.zeros_like(g)

        c = {"hmask": (r > 0).astype(f32), "rc": rc_ref[...], "rs1": rs1_ref[...], "rs2": rs2_ref[...]}
        _, vjp = jax.vjp(lambda p_, w_: _branch(p_, w_, c), _load_branch_inputs(proj_ref, halo_ref),
                         _load_branch_weights(wrefs))
        cts = (dq_ref[...] * LN2, dk_ref[...] * LN2, dv_ref[...], dync_ref[...].astype(f32),
               dyns_ref[...].astype(f32))
        dp, dw = vjp(cts)
        for n in SEG:
            if n in ("zm", "a", "glu"):
                continue
            dp_ref[:, SEG[n][0] - 512:SEG[n][1] - 512] = dp[n].astype(bf16)
        for n, hn, lo in (("a", "ha", 0), ("glu", "hglu", CONV_W)):
            d = dp[n]
            tail = d[tm - HALO:, :] + carry_ref[:, lo:lo + CONV_W]
            s0 = SEG[n][0] - 512
            dp_ref[0:tm - HALO, s0:s0 + CONV_W] = d[:tm - HALO, :].astype(bf16)
            dp_ref[tm - HALO:tm, s0:s0 + CONV_W] = tail.astype(bf16)
        carry_ref[:, 0:CONV_W] = dp["ha"]
        carry_ref[:, CONV_W:] = dp["hglu"]
        for (n, _), g in zip(BR_W, gwrefs):
            if n in ("sg_w", "sg_b"):
                for gi in range(SG_HEADS):
                    g[gi] += dw[n][gi]
            else:
                g[...] += dw[n]

    row = lambda wd: pl.BlockSpec((tm, wd), lambda i: (nt - 1 - i, 0))
    wide = HEADS * HEAD_PAD
    return pl.pallas_call(
        body, name=name, grid=(nt,),
        in_specs=[row(PROJ_PAD),
                  pl.BlockSpec((HALO, 2 * CONV_W), lambda i: (jnp.maximum((nt - 1 - i) * hb - 1, 0), 1)),
                  row(HEAD_PAD), row(HEAD_PAD), row(HEAD_PAD)] + [_const_spec(s) for _, s in BR_W]
                 + [row(wide), row(wide), row(wide), row(CONV_W), row(SG_W)],
        out_specs=[row(REST)] + [_const_spec(s) for _, s in BR_W],
        out_shape=[jax.ShapeDtypeStruct((T, REST), bf16)] + [jax.ShapeDtypeStruct(s, f32) for _, s in BR_W],
        scratch_shapes=[pltpu.VMEM((HALO, 2 * CONV_W), f32)],
    )(proj, proj, *tabs, *[bw[n] for n, _ in BR_W], dq, dk, dv, dync, dyns)


def _inproj_bwd_dx(x, g, w, dzm, dprest, dout, tm, name):
    T = x.shape[0]
    nt_dims = (((1,), (1,)), ((), ()))

    def body(x_ref, g_ref, w_ref, dzm_ref, dpr_ref, dout_ref, dx_ref, dg_ref, h_ref):
        i = pl.program_id(0)

        @pl.when(i == 0)
        def _():
            dg_ref[...] = jnp.zeros_like(dg_ref)

        dh = lax.dot_general(dzm_ref[...], w_ref[:, 0:512], nt_dims, preferred_element_type=f32)
        dh = dh + lax.dot_general(dpr_ref[...], w_ref[:, 512:], nt_dims, preferred_element_type=f32)
        h, vjp = jax.vjp(_rms, x_ref[...], g_ref[...])
        dx, dg = vjp(dh)
        dx_ref[...] = dout_ref[...] + dx
        dg_ref[...] += dg
        h_ref[...] = h.astype(bf16)

    row = lambda wd: pl.BlockSpec((tm, wd), lambda i: (i, 0))
    return pl.pallas_call(
        body, name=name, grid=(T // tm,),
        in_specs=[row(D_MODEL), _const_spec((1, D_MODEL)), _const_spec((D_MODEL, PROJ_PAD)), row(512), row(REST),
                  row(D_MODEL)],
        out_specs=[row(D_MODEL), _const_spec((1, D_MODEL)), row(D_MODEL)],
        out_shape=[jax.ShapeDtypeStruct((T, D_MODEL), f32), jax.ShapeDtypeStruct((1, D_MODEL), f32),
                   jax.ShapeDtypeStruct((T, D_MODEL), bf16)],
    )(x, g, w, dzm, dprest, dout)


def _inproj_bwd_dw(h, dzm, dprest, tm, name):
    T = h.shape[0]
    tm = min(4 * tm, T)
    nt = T // tm
    cw = REST // 3
    tn = (((0,), (0,)), ((), ()))

    def body_for(nsteps):
        def body(h_ref, d_ref, o_ref):
            i = pl.program_id(1)

            @pl.when(i == 0)
            def _():
                o_ref[...] = jnp.zeros_like(o_ref)

            o_ref[...] += lax.dot_general(h_ref[...], d_ref[...], tn, preferred_element_type=f32)
        return body

    def run(d, width, cwid, nm):
        return pl.pallas_call(
            body_for(nt), name=nm, grid=(width // cwid, nt),
            in_specs=[pl.BlockSpec((tm, D_MODEL), lambda j, i: (i, 0)), pl.BlockSpec((tm, cwid), lambda j, i: (i, j))],
            out_specs=pl.BlockSpec((D_MODEL, cwid), lambda j, i: (0, j)),
            out_shape=jax.ShapeDtypeStruct((D_MODEL, width), f32),
        )(h, d)

    return run(dzm, 512, 512, name + "_zm"), run(dprest, REST, cw, name + "_rest")


def _row_tile(rows):
    return ROW_ALIGN if rows > ROW_ALIGN and rows % ROW_ALIGN == 0 else rows


def _sum4(buf, name):
    _, R, C = buf.shape
    tr = _row_tile(R)

    def body(b_ref, o_ref):
        b = [b_ref[s].astype(f32) for s in range(4)]
        o_ref[...] = ((b[0] + b[1]) + b[2]) + b[3]

    return pl.pallas_call(
        body, name=name, grid=(R // tr,),
        in_specs=[pl.BlockSpec((4, tr, C), lambda i: (0, i, 0))],
        out_specs=pl.BlockSpec((tr, C), lambda i: (i, 0)),
        out_shape=jax.ShapeDtypeStruct((R, C), f32),
    )(buf)


def _sum4_whole(bufs, name):
    n = len(bufs)

    def body(*refs):
        for b_ref, o_ref in zip(refs[:n], refs[n:]):
            o_ref[...] = ((b_ref[0] + b_ref[1]) + b_ref[2]) + b_ref[3]

    return pl.pallas_call(
        body, name=name, out_shape=[jax.ShapeDtypeStruct(b.shape[1:], f32) for b in bufs])(*bufs)


def _adamw_math(w, m, v, g):
    m_new = ADAM_B1 * m + (1.0 - ADAM_B1) * g
    v_new = ADAM_B2 * v + (1.0 - ADAM_B2) * (g * g)
    m_hat = m_new / (1.0 - ADAM_B1 ** ADAM_STEP)
    v_hat = v_new / (1.0 - ADAM_B2 ** ADAM_STEP)
    return -ADAM_LR * (m_hat / (jnp.sqrt(v_hat) + ADAM_EPS) + ADAM_WD * w), m_new, v_new


def _adamw(w, m, v, p_a, p_b, name):
    R, C = w.shape
    tr = _row_tile(R)

    def body(w_ref, m_ref, v_ref, a_ref, b_ref, g_ref, d_ref, nm_ref, nv_ref):
        g = a_ref[...] + b_ref[...]
        g_ref[...] = g
        d_ref[...], nm_ref[...], nv_ref[...] = _adamw_math(w_ref[...], m_ref[...], v_ref[...], g)

    spec = pl.BlockSpec((tr, C), lambda i: (i, 0))
    return pl.pallas_call(
        body, name=name, grid=(R // tr,), in_specs=[spec] * 5, out_specs=[spec] * 4,
        out_shape=[jax.ShapeDtypeStruct((R, C), f32)] * 4,
    )(w, m, v, p_a, p_b)


def _adamw_whole(ws, ms, vs, p_a, p_b, name):
    n = len(ws)

    def body(*refs):
        ins, outs = refs[:5 * n], refs[5 * n:]
        for i in range(n):
            w, m, v, a, b = (ins[k * n + i][...] for k in range(5))
            g = a + b
            outs[i][...] = g
            outs[n + i][...], outs[2 * n + i][...], outs[3 * n + i][...] = _adamw_math(w, m, v, g)

    res = pl.pallas_call(
        body, name=name, out_shape=[jax.ShapeDtypeStruct(w.shape, f32) for w in ws] * 4,
    )(*ws, *ms, *vs, *p_a, *p_b)
    return [res[k * n:(k + 1) * n] for k in range(4)]


def _chip_exchange(srcs, per_target, name):
    n = len(srcs)

    def body(*refs):
        src_refs, out_refs = refs[:n], refs[n:2 * n]
        send_sems, recv_sems, local_sems = refs[2 * n:]
        x, y, c = lax.axis_index("x"), lax.axis_index("y"), lax.axis_index("c")
        me = 2 * x + y
        chips = [(1 - x, y), (x, 1 - y), (1 - x, 1 - y)]

        def block_for(i, t):
            return src_refs[i].at[t] if per_target[i] else src_refs[i]

        def remote(i, k, block_t, slot):
            tx, ty = chips[k]
            return pltpu.make_async_remote_copy(
                src_ref=block_for(i, block_t), dst_ref=out_refs[i].at[slot], send_sem=send_sems.at[3 * i + k],
                recv_sem=recv_sems.at[3 * i + k], device_id=(tx, ty, c), device_id_type=pl.DeviceIdType.MESH)

        sends = [remote(i, k, 2 * chips[k][0] + chips[k][1], me) for i in range(n) for k in range(3)]
        for cp in sends:
            cp.start()
        mine = [pltpu.make_async_copy(block_for(i, me), out_refs[i].at[me], local_sems.at[i]) for i in range(n)]
        for cp in mine:
            cp.start()
        for i in range(n):
            for k in range(3):
                remote(i, k, me, 2 * chips[k][0] + chips[k][1]).wait_recv()
        for cp in sends:
            cp.wait_send()
        for cp in mine:
            cp.wait()

    blocks = [s.shape[1:] if p else s.shape for s, p in zip(srcs, per_target)]
    return pl.pallas_call(
        body, name=name,
        in_specs=[pl.BlockSpec(memory_space=pl.ANY)] * n, out_specs=[pl.BlockSpec(memory_space=pl.ANY)] * n,
        out_shape=[jax.ShapeDtypeStruct((4,) + tuple(b), s.dtype) for b, s in zip(blocks, srcs)],
        scratch_shapes=[pltpu.SemaphoreType.DMA((3 * n,)), pltpu.SemaphoreType.DMA((3 * n,)),
                        pltpu.SemaphoreType.DMA((n,))],
    )(*srcs)


def _sibling_exchange(srcs, name):
    n = len(srcs)

    def body(*refs):
        src_refs, out_refs = refs[:n], refs[n:2 * n]
        send_sems, recv_sems = refs[2 * n:]
        x, y, c = lax.axis_index("x"), lax.axis_index("y"), lax.axis_index("c")
        copies = [pltpu.make_async_remote_copy(
            src_ref=src_refs[i], dst_ref=out_refs[i], send_sem=send_sems.at[i], recv_sem=recv_sems.at[i],
            device_id=(x, y, 1 - c), device_id_type=pl.DeviceIdType.MESH) for i in range(n)]
        for cp in copies:
            cp.start()
        for cp in copies:
            cp.wait()

    return pl.pallas_call(
        body, name=name,
        in_specs=[pl.BlockSpec(memory_space=pl.ANY)] * n, out_specs=[pl.BlockSpec(memory_space=pl.ANY)] * n,
        out_shape=[jax.ShapeDtypeStruct(s.shape, s.dtype) for s in srcs],
        scratch_shapes=[pltpu.SemaphoreType.DMA((n,)), pltpu.SemaphoreType.DMA((n,))],
    )(*srcs)


W_IN_SHARD = IN_COLS // 4
W_IN_MAP = [(0, 1792, 512), (1792, 1824, 2368), (1824, 2336, 0), (2336, IN_COLS, 2432)]


def _pad_w_in(shards):
    def cols(a, b):
        out = []
        while a < b:
            j = a // W_IN_SHARD
            e = min(b, (j + 1) * W_IN_SHARD)
            out.append(shards[j][:, a - j * W_IN_SHARD:e - j * W_IN_SHARD])
            a = e
        return out

    z = lambda n: [jnp.zeros((D_MODEL, n), shards.dtype)]
    return jnp.concatenate(cols(1824, 2336) + cols(0, 1792) + z(64) + cols(1792, 1824) + z(32) + cols(2336, IN_COLS),
                           axis=1)


def _w_in_grad_shard(dw_zm, dw_rest, j):
    lo, hi = j * W_IN_SHARD, (j + 1) * W_IN_SHARD
    out = []
    for a, b, p in W_IN_MAP:
        a2, b2 = max(a, lo), min(b, hi)
        if a2 < b2:
            p0 = p + a2 - a
            src, off = (dw_zm, 0) if p0 < 512 else (dw_rest, 512)
            out.append(src[:, p0 - off:p0 - off + b2 - a2])
    return jnp.concatenate(out, axis=1)


def _v_cols():
    return [(h % 2) * VDIM for h in range(HEADS)]


def _layer_weights(full, small, l):
    w_uq = full["w_uq"][l].astype(f32).reshape(Q_LORA, HEADS, QK)
    w_uq = jnp.pad(w_uq, ((0, 0), (0, 0), (0, HEAD_PAD - QK))).reshape(Q_LORA, HEADS * HEAD_PAD)
    ukv = full["w_ukv"][l].astype(f32).reshape(KV_LORA, HEADS, NOPE + VDIM)
    w_k = jnp.pad(ukv[:, :, :NOPE], ((0, 0), (0, 0), (0, HEAD_PAD - NOPE))).reshape(KV_LORA, HEADS * HEAD_PAD)
    zeros = jnp.zeros((KV_LORA, VDIM), f32)
    w_v = jnp.concatenate(
        [jnp.concatenate([ukv[:, h, NOPE:], zeros] if h % 2 == 0 else [zeros, ukv[:, h, NOPE:]], axis=1)
         for h in range(HEADS)], axis=1)
    row = lambda a: a.reshape(1, -1)
    bn = small["branch_norm_g"][l]
    bw = {
        "conv_w": jnp.pad(full["conv_w"][l], ((0, 1), (0, 0))), "conv_b": row(small["conv_b"][l]),
        "cln_g": row(small["conv_ln_g"][l]), "cln_b": row(small["conv_ln_b"][l]),
        "pw_w": full["conv_pw_w"][l].astype(f32), "pw_b": row(small["conv_pw_b"][l]),
        "qn_g": row(small["q_norm_g"][l]), "w_uq": w_uq, "kvn_g": row(small["kv_norm_g"][l]),
        "w_k": w_k, "w_v": w_v,
        "qkq_g": jnp.pad(row(small["qk_q_g"][l]), ((0, 0), (0, HEAD_PAD - QK))),
        "qkk_g": jnp.pad(row(small["qk_k_g"][l]), ((0, 0), (0, HEAD_PAD - QK))),
        "sln_g": row(small["sg_ln_g"][l]), "sln_b": row(small["sg_ln_b"][l]),
        "sg_w": small["sg_w"][l], "sg_b": small["sg_b"][l].reshape(SG_HEADS, SG_CHUNK, 1),
        "bn_c": row(bn[:CONV_W]), "bn_s": row(bn[CONV_W + MLA_W:]),
    }
    return {
        "norm_g": row(small["norm_g"][l]), "w_in": _pad_w_in(full["w_in_shards"][:, l]), "bw": bw,
        "bn_m": row(bn[CONV_W:CONV_W + MLA_W]), "w_out": full["w_out"][l],
    }


def _layer_grads(gb, dw_in_pieces, dnorm_g, dw_out, dbn_m):
    duq = gb["w_uq"].reshape(Q_LORA, HEADS, HEAD_PAD)[:, :, :QK].reshape(Q_LORA, HEADS * QK)
    dk = gb["w_k"].reshape(KV_LORA, HEADS, HEAD_PAD)[:, :, :NOPE]
    dv = gb["w_v"].reshape(KV_LORA, HEADS, HEAD_PAD)
    dv = jnp.stack([dv[:, h, c0:c0 + VDIM] for h, c0 in enumerate(_v_cols())], axis=1)
    dukv = jnp.concatenate([dk, dv], axis=2).reshape(KV_LORA, HEADS * (NOPE + VDIM))
    return {
        "norm_g": dnorm_g[0], "w_in": dw_in_pieces, "conv_w": gb["conv_w"][:CONV_K],
        "conv_b": gb["conv_b"][0], "conv_ln_g": gb["cln_g"][0], "conv_ln_b": gb["cln_b"][0],
        "conv_pw_w": gb["pw_w"], "conv_pw_b": gb["pw_b"][0], "q_norm_g": gb["qn_g"][0], "w_uq": duq,
        "kv_norm_g": gb["kvn_g"][0], "w_ukv": dukv, "qk_q_g": gb["qkq_g"][0, :QK], "qk_k_g": gb["qkk_g"][0, :QK],
        "sg_ln_g": gb["sln_g"][0], "sg_ln_b": gb["sln_b"][0], "sg_w": gb["sg_w"], "sg_b": gb["sg_b"][:, :, 0],
        "branch_norm_g": jnp.concatenate([gb["bn_c"][0], dbn_m[0], gb["bn_s"][0]]), "w_out": dw_out,
    }


def _rope_tables(T):
    half = ROPE // 2
    inv_freq = ROPE_THETA ** (-jnp.arange(half, dtype=f32) / half)
    ang = jnp.arange(T, dtype=f32)[:, None] * inv_freq[None, :]
    cos, sin = jnp.cos(ang), jnp.sin(ang)
    one = jnp.ones((T, NOPE), f32)
    z = lambda n: jnp.zeros((T, n), f32)
    rc = jnp.concatenate([one, cos, cos, jnp.ones((T, HEAD_PAD - QK), f32)], axis=1)
    rs1 = jnp.concatenate([z(NOPE), -sin, z(half), z(HEAD_PAD - QK)], axis=1)
    rs2 = jnp.concatenate([z(NOPE), z(half), sin, z(HEAD_PAD - QK)], axis=1)
    return rc, rs1, rs2


def kernel(x, norm_g, w_in, conv_w, conv_b, conv_ln_g, conv_ln_b, conv_pw_w, conv_pw_b, q_norm_g, w_uq, kv_norm_g, w_ukv, qk_q_g, qk_k_g, sg_ln_g, sg_ln_b, sg_w, sg_b, branch_norm_g, w_out, loss_target, m_norm_g, m_w_in, m_conv_w, m_conv_b, m_conv_ln_g, m_conv_ln_b, m_conv_pw_w, m_conv_pw_b, m_q_norm_g, m_w_uq, m_kv_norm_g, m_w_ukv, m_qk_q_g, m_qk_k_g, m_sg_ln_g, m_sg_ln_b, m_sg_w, m_sg_b, m_branch_norm_g, m_w_out, v_norm_g, v_w_in, v_conv_w, v_conv_b, v_conv_ln_g, v_conv_ln_b, v_conv_pw_w, v_conv_pw_b, v_q_norm_g, v_w_uq, v_kv_norm_g, v_w_ukv, v_qk_q_g, v_qk_k_g, v_sg_ln_g, v_sg_ln_b, v_sg_w, v_sg_b, v_branch_norm_g, v_w_out):
    wts = dict(zip(WEIGHTS, [norm_g, w_in, conv_w, conv_b, conv_ln_g, conv_ln_b, conv_pw_w, conv_pw_b, q_norm_g,
                             w_uq, kv_norm_g, w_ukv, qk_q_g, qk_k_g, sg_ln_g, sg_ln_b, sg_w, sg_b, branch_norm_g,
                             w_out]))
    mom_m = dict(zip(WEIGHTS, [m_norm_g, m_w_in, m_conv_w, m_conv_b, m_conv_ln_g, m_conv_ln_b, m_conv_pw_w,
                               m_conv_pw_b, m_q_norm_g, m_w_uq, m_kv_norm_g, m_w_ukv, m_qk_q_g, m_qk_k_g,
                               m_sg_ln_g, m_sg_ln_b, m_sg_w, m_sg_b, m_branch_norm_g, m_w_out]))
    mom_v = dict(zip(WEIGHTS, [v_norm_g, v_w_in, v_conv_w, v_conv_b, v_conv_ln_g, v_conv_ln_b, v_conv_pw_w,
                               v_conv_pw_b, v_q_norm_g, v_w_uq, v_kv_norm_g, v_w_ukv, v_qk_q_g, v_qk_k_g,
                               v_sg_ln_g, v_sg_ln_b, v_sg_w, v_sg_b, v_branch_norm_g, v_w_out]))
    xs = x[0]
    tgt = loss_target[0]
    T = xs.shape[0]
    tm = min(256, T)
    tb = min(512, T // 2)

    big = list(BIG)
    shards = [wts[n] if n == "conv_w" else wts[n].astype(bf16) for n in big]
    gathered = _chip_exchange(shards, [False] * len(big), "gather_weights")
    full = {n: jnp.concatenate([g[j] for j in range(4)], axis=BIG[n]) for n, g in zip(big, gathered)
            if n != "w_in"}
    full["w_in_shards"] = gathered[big.index("w_in")]
    small = {n: wts[n] for n in SMALL}

    tabs = _rope_tables(T)
    lw = [_layer_weights(full, small, l) for l in range(DEPTH)]

    acts = []
    h_in = xs
    for l in range(DEPTH):
        W = lw[l]
        proj = _inproj_fwd(h_in, W["norm_g"], W["w_in"], tm, f"inproj_fwd_{l}")
        q, k, v, ync, yns = _branch_fwd(proj, tabs, W["bw"], tm, f"branch_fwd_{l}")
        o, lse = _attn_fwd(q, k, v, tb, f"attn_fwd_{l}")
        y = _out_fwd(h_in, o, proj, ync, yns, W["bn_m"], W["w_out"], tm, f"out_fwd_{l}")
        acts.append((h_in, proj, q, k, v, ync, yns, o, lse))
        h_in = y

    dy, loss_row = _loss_grad(h_in, tgt, tm, "loss_grad")
    loss = lax.psum(loss_row[0, 0], ("x", "y", "c"))

    grads = [None] * DEPTH
    for l in reversed(range(DEPTH)):
        W = lw[l]
        xin, proj, q, k, v, ync, yns, o, lse = acts[l]
        do, delta, dzm, dync, dyns, dw_out, dbn_m = _out_bwd(
            dy, o, proj, ync, yns, W["bn_m"], W["w_out"], tm, f"out_bwd_{l}")
        dq, dk, dv = _attn_bwd(q, k, v, do, lse, delta.reshape(HEADS, 1, T), tb, f"attn_bwd_{l}")
        outs = _branch_bwd(proj, tabs, W["bw"], dq, dk, dv, dync, dyns, tm, f"branch_bwd_{l}")
        dprest, gb = outs[0], dict(zip([n for n, _ in BR_W], outs[1:]))
        dx, dnorm_g, hb = _inproj_bwd_dx(xin, W["norm_g"], W["w_in"], dzm, dprest, dy, tm, f"inproj_bwd_dx_{l}")
        dw_in_pieces = _inproj_bwd_dw(hb, dzm, dprest, tm, f"inproj_bwd_dw_{l}")
        grads[l] = _layer_grads(gb, dw_in_pieces, dnorm_g, dw_out, dbn_m)
        dy = dx
    grad_x = dy[None]
    g_full = {n: jnp.stack([grads[l][n] for l in range(DEPTH)]) for n in WEIGHTS if n != "w_in"}

    view = lambda a: a.reshape(-1, a.shape[-1])
    nb = len(big)
    def blocks_for_chips(n):
        if n == "w_in":
            return [jnp.concatenate([_w_in_grad_shard(*grads[l][n], j) for l in range(DEPTH)]) for j in range(4)]
        return [view(s) for s in jnp.split(g_full[n], 4, axis=BIG[n])]

    per_chip = [jnp.stack(blocks_for_chips(n)).astype(bf16) for n in big]
    contrib = _chip_exchange(per_chip + [g_full[n] for n in SMALL], [True] * nb + [False] * len(SMALL),
                             "exchange_grads")
    parts = [_sum4(cb, f"sum_chips_{n}") for n, cb in zip(big, contrib)]
    parts += list(_sum4_whole(contrib[nb:], "sum_chips_small"))
    others = _sibling_exchange(parts, "exchange_cores")

    res = {kind: {} for kind in ("grad", "delta", "new_m", "new_v")}
    for i, n in enumerate(big):
        outs = _adamw(*[view(d[n]) for d in (wts, mom_m, mom_v)], parts[i], others[i], f"adamw_{n}")
        for kind, o in zip(res, outs):
            res[kind][n] = o.reshape(wts[n].shape)
    outs = _adamw_whole(*[[d[n] for n in SMALL] for d in (wts, mom_m, mom_v)], parts[nb:], others[nb:],
                        "adamw_small")
    for kind, o in zip(res, outs):
        res[kind].update(zip(SMALL, o))
    out = [loss, grad_x]
    for kind in ("grad", "delta", "new_m", "new_v"):
        out += [res[kind][n] for n in WEIGHTS]
    return tuple(out)
```

```python
import functools

import jax
import jax.numpy as jnp
from jax import lax
from jax.experimental import pallas as pl
from jax.experimental.pallas import tpu as pltpu

f32 = jnp.float32
bf16 = jnp.bfloat16

DEPTH = 2
D_MODEL = 1024
CONV_W = 256
CONV_K = 31
HEADS = 8
NOPE = 64
ROPE = 32
QK = NOPE + ROPE
VDIM = 64
MLA_W = HEADS * VDIM
Q_LORA = 768
KV_LORA = 256
SG_W = 256
SG_HEADS = 4
SG_CHUNK = 128
IN_COLS = 3104
ROPE_THETA = 10000.0
EPS = 1e-6
HEAD_PAD = 128
LOG2E = 1.4426950408889634
LN2 = 0.6931471805599453
Q_SCALE = QK ** -0.5 * LOG2E
HALO = 32
ROW_ALIGN = 256

ADAM_LR = 0.001
ADAM_B1 = 0.9
ADAM_B2 = 0.999
ADAM_EPS = 1e-08
ADAM_WD = 0.01
ADAM_STEP = 10

PROJ_PAD = 3200
SEG = {
    "zm": (0, 512), "a": (512, 768), "glu": (768, 1024), "zc": (1024, 1280), "cq": (1280, 2048),
    "ckv": (2048, 2304), "kr": (2304, 2432), "us": (2432, 2688), "vs": (2688, 2944), "zs": (2944, 3200),
}
REST = PROJ_PAD - 512

WEIGHTS = ["norm_g", "w_in", "conv_w", "conv_b", "conv_ln_g", "conv_ln_b", "conv_pw_w", "conv_pw_b", "q_norm_g",
           "w_uq", "kv_norm_g", "w_ukv", "qk_q_g", "qk_k_g", "sg_ln_g", "sg_ln_b", "sg_w", "sg_b",
           "branch_norm_g", "w_out"]
BIG = {"w_in": 2, "conv_w": 2, "conv_pw_w": 1, "w_uq": 1, "w_ukv": 2, "w_out": 1}
SMALL = [n for n in WEIGHTS if n not in BIG]

BR_W = [("conv_w", (32, 256)), ("conv_b", (1, 256)), ("cln_g", (1, 256)), ("cln_b", (1, 256)),
        ("pw_w", (256, 256)), ("pw_b", (1, 256)), ("qn_g", (1, 768)), ("w_uq", (768, 1024)),
        ("kvn_g", (1, 256)), ("w_k", (256, 1024)), ("w_v", (256, 1024)), ("qkq_g", (1, 128)),
        ("qkk_g", (1, 128)), ("sln_g", (1, 256)), ("sln_b", (1, 256)), ("sg_w", (4, 128, 128)),
        ("sg_b", (4, 128, 1)), ("bn_c", (1, 256)), ("bn_s", (1, 256))]


@jax.custom_vjp
def _mm(a, w):
    return jnp.dot(a.astype(bf16), w.astype(bf16), preferred_element_type=f32)


def _mm_fwd(a, w):
    return _mm(a, w), (a, w)


def _mm_bwd(res, ct):
    a, w = res
    ctb = ct.astype(bf16)
    da = lax.dot_general(ctb, w.astype(bf16), (((1,), (1,)), ((), ())), preferred_element_type=f32)
    dw = lax.dot_general(a.astype(bf16), ctb, (((0,), (0,)), ((), ())), preferred_element_type=f32)
    return da.astype(a.dtype), dw.astype(w.dtype)


_mm.defvjp(_mm_fwd, _mm_bwd)


@jax.custom_vjp
def _rope(x, c, s1, s2):
    return x * c + pltpu.roll(x, HEAD_PAD - 16, 1) * s1 + pltpu.roll(x, 16, 1) * s2


def _rope_fwd(x, c, s1, s2):
    return _rope(x, c, s1, s2), (c, s1, s2)


def _rope_bwd(res, ct):
    c, s1, s2 = res
    dx = ct * c + pltpu.roll(ct * s1, 16, 1) + pltpu.roll(ct * s2, HEAD_PAD - 16, 1)
    return dx, jnp.zeros_like(c), jnp.zeros_like(s1), jnp.zeros_like(s2)


_rope.defvjp(_rope_fwd, _rope_bwd)


def _rms(x, g):
    return x * lax.rsqrt(jnp.mean(x * x, axis=-1, keepdims=True) + EPS) * g


def _head_rms(x, g):
    return x * lax.rsqrt(jnp.sum(x * x, axis=-1, keepdims=True) * (1.0 / QK) + EPS) * g


def _ln(x, g, b):
    mu = jnp.mean(x, axis=-1, keepdims=True)
    xc = x - mu
    var = jnp.mean(xc * xc, axis=-1, keepdims=True)
    return xc * lax.rsqrt(var + EPS) * g + b


def _branch(p, w, c):
    tm = p["a"].shape[0]
    yg = p["a"] * jax.nn.sigmoid(p["glu"])
    yh = p["ha"] * jax.nn.sigmoid(p["hglu"]) * c["hmask"]
    ycat = jnp.concatenate([yh, yg], axis=0)
    shifted = [ycat] + [ycat[b:b + tm + HALO - 8, :] for b in range(1, 8)]
    acc = jnp.zeros((tm, CONV_W), f32)
    for k in range(CONV_K):
        off = HALO - (CONV_K - 1) + k
        a8 = off - off % 8
        acc = acc + shifted[off % 8][a8:a8 + tm, :] * w["conv_w"][k:k + 1, :]
    yl = jax.nn.silu(_ln(acc + w["conv_b"], w["cln_g"], w["cln_b"]))
    ypw = _mm(yl, w["pw_w"]) + w["pw_b"]
    ync = _rms(ypw * jax.nn.silu(p["zc"]), w["bn_c"])
    cqn = _rms(p["cq"], w["qn_g"])
    qf = _mm(cqn, w["w_uq"])
    ckvn = _rms(p["ckv"], w["kvn_g"])
    kf = _mm(ckvn, w["w_k"])
    v = _mm(ckvn, w["w_v"])
    qs, ks = [], []
    for h in range(HEADS):
        sl = slice(h * HEAD_PAD, (h + 1) * HEAD_PAD)
        qh = _head_rms(qf[:, sl], w["qkq_g"])
        qs.append(_rope(qh, c["rc"], c["rs1"], c["rs2"]) * Q_SCALE)
        kh = _head_rms(kf[:, sl] + p["kr"], w["qkk_g"])
        ks.append(_rope(kh, c["rc"], c["rs1"], c["rs2"]))
    q = jnp.concatenate(qs, axis=-1)
    k = jnp.concatenate(ks, axis=-1)
    u = jax.nn.gelu(p["us"])
    v2 = _ln(jax.nn.gelu(p["vs"]), w["sln_g"], w["sln_b"])
    r = lax.broadcasted_iota(jnp.int32, (SG_CHUNK, SG_CHUNK), 0)
    cc = lax.broadcasted_iota(jnp.int32, (SG_CHUNK, SG_CHUNK), 1)
    lane = lax.broadcasted_iota(jnp.int32, (1, SG_W), 1) // (SG_W // SG_HEADS)
    wm = [jnp.where(r >= cc, w["sg_w"][g], 0.0) for g in range(SG_HEADS)]
    rows = []
    for ci in range(tm // SG_CHUNK):
        vc = v2[ci * SG_CHUNK:(ci + 1) * SG_CHUNK, :]
        mixed = jnp.zeros((SG_CHUNK, SG_W), f32)
        for g in range(SG_HEADS):
            mixed = mixed + jnp.where(lane == g, _mm(wm[g], vc) + w["sg_b"][g], 0.0)
        rows.append(mixed)
    mixed = jnp.concatenate(rows, axis=0) if len(rows) > 1 else rows[0]
    yns = _rms(u * mixed * jax.nn.silu(p["zs"]), w["bn_s"])
    return q, k, v, ync, yns


def _mla_out(o, zm, g):
    return _rms(o * jax.nn.silu(zm), g)


def _load_branch_inputs(proj_ref, halo_ref):
    p = {n: proj_ref[:, SEG[n][0]:SEG[n][1]].astype(f32) for n in SEG if n != "zm"}
    p["ha"] = halo_ref[:, 0:CONV_W].astype(f32)
    p["hglu"] = halo_ref[:, CONV_W:2 * CONV_W].astype(f32)
    return p


def _load_branch_weights(refs):
    w = {}
    for (n, _), r in zip(BR_W, refs):
        if n in ("sg_w", "sg_b"):
            w[n] = [r[g] for g in range(SG_HEADS)]
        else:
            w[n] = r[...]
    return w


def _const_spec(shape):
    nd = len(shape)
    return pl.BlockSpec(shape, lambda *_: (0,) * nd)


def _inproj_fwd(x, g, w, tm, name):
    T = x.shape[0]
    nc = 5
    cw = PROJ_PAD // nc

    def body(x_ref, g_ref, w_ref, o_ref):
        h = _rms(x_ref[...], g_ref[...]).astype(bf16)
        for j in range(nc):
            o_ref[:, j * cw:(j + 1) * cw] = jnp.dot(
                h, w_ref[:, j * cw:(j + 1) * cw], preferred_element_type=f32).astype(bf16)

    return pl.pallas_call(
        body, name=name, grid=(T // tm,),
        in_specs=[pl.BlockSpec((tm, D_MODEL), lambda i: (i, 0)), _const_spec((1, D_MODEL)),
                  _const_spec((D_MODEL, PROJ_PAD))],
        out_specs=pl.BlockSpec((tm, PROJ_PAD), lambda i: (i, 0)),
        out_shape=jax.ShapeDtypeStruct((T, PROJ_PAD), bf16),
    )(x, g, w)


def _branch_fwd(proj, tabs, bw, tm, name):
    T = proj.shape[0]
    hb = tm // HALO

    def body(proj_ref, halo_ref, rc_ref, rs1_ref, rs2_ref, *rest):
        wrefs, (q_ref, k_ref, v_ref, ync_ref, yns_ref) = rest[:len(BR_W)], rest[len(BR_W):]
        i = pl.program_id(0)
        c = {"hmask": (i > 0).astype(f32), "rc": rc_ref[...], "rs1": rs1_ref[...], "rs2": rs2_ref[...]}
        q, k, v, ync, yns = _branch(_load_branch_inputs(proj_ref, halo_ref), _load_branch_weights(wrefs), c)
        q_ref[...] = q.astype(bf16)
        k_ref[...] = k.astype(bf16)
        v_ref[...] = v.astype(bf16)
        ync_ref[...] = ync.astype(bf16)
        yns_ref[...] = yns.astype(bf16)

    row = lambda wd: pl.BlockSpec((tm, wd), lambda i: (i, 0))
    wide = HEADS * HEAD_PAD
    return pl.pallas_call(
        body, name=name, grid=(T // tm,),
        in_specs=[row(PROJ_PAD), pl.BlockSpec((HALO, 2 * CONV_W), lambda i: (jnp.maximum(i * hb - 1, 0), 1)),
                  row(HEAD_PAD), row(HEAD_PAD), row(HEAD_PAD)] + [_const_spec(s) for _, s in BR_W],
        out_specs=[row(wide), row(wide), row(wide), row(CONV_W), row(SG_W)],
        out_shape=[jax.ShapeDtypeStruct((T, wide), bf16)] * 3 + [jax.ShapeDtypeStruct((T, CONV_W), bf16)] * 2,
    )(proj, proj, *tabs, *[bw[n] for n, _ in BR_W])


def _attn_fwd(q, k, v, tb, name, ride=None):
    T = q.shape[0]
    nb = T // tb
    pw = 2 * HEAD_PAD

    def body(q_ref, k_ref, v_ref, o_ref, lse_ref):
        qi = pl.program_id(1)
        krow = lax.broadcasted_iota(jnp.int32, (tb, tb), 0)
        qcol = lax.broadcasted_iota(jnp.int32, (tb, tb), 1)
        heads = [slice(e * HEAD_PAD, (e + 1) * HEAD_PAD) for e in range(2)]
        qh = [q_ref[:, hs] for hs in heads]

        def step(off, carry, masked):
            out = []
            for e, hs in enumerate(heads):
                m, l, acc = carry[e]
                kb = k_ref[pl.ds(off, tb), hs]
                vb = v_ref[pl.ds(off, tb), hs]
                s_t = lax.dot_general(kb, qh[e], (((1,), (1,)), ((), ())), preferred_element_type=f32)
                if masked:
                    s_t = jnp.where(krow <= qcol, s_t, -1e30)
                m_new = jnp.maximum(m, jnp.max(s_t, axis=0, keepdims=True))
                p_t = jnp.exp2(s_t - m_new)
                alpha = jnp.exp2(m - m_new)
                l = alpha * l + jnp.sum(p_t, axis=0, keepdims=True)
                acc = alpha * acc + lax.dot_general(vb, p_t.astype(bf16), (((0,), (0,)), ((), ())),
                                                    preferred_element_type=f32)
                out.append((m_new, l, acc))
            return tuple(out)

        one = (jnp.full((1, tb), -1e30, f32), jnp.zeros((1, tb), f32), jnp.zeros((HEAD_PAD, tb), f32))
        carry = lax.fori_loop(0, qi, lambda ki, cr: step(pl.multiple_of(ki * tb, tb), cr, False), (one, one))
        carry = step(pl.multiple_of(qi * tb, tb), carry, True)
        o_t = jnp.zeros((HEAD_PAD, tb), f32)
        for e in range(2):
            m, l, acc = carry[e]
            o_t = o_t + acc / l
            lse_ref[e] = m + jnp.log2(l)
        o_ref[...] = o_t.T

    return _call_with_rider(
        body, 3, 2, ride, (HEADS // 2, nb),
        [pl.BlockSpec((tb, pw), lambda h, i: (i, h)), pl.BlockSpec((T, pw), lambda h, i: (0, h)),
         pl.BlockSpec((T, pw), lambda h, i: (0, h))],
        [pl.BlockSpec((tb, HEAD_PAD), lambda h, i: (i, h)), pl.BlockSpec((2, 1, tb), lambda h, i: (h, 0, i))],
        [jax.ShapeDtypeStruct((T, MLA_W), f32), jax.ShapeDtypeStruct((HEADS, 1, T), f32)],
        (q, k, v), name)


def _out_fwd(x, o, proj, ync, yns, g_m, w_out, tm, name):
    T = x.shape[0]

    def body(x_ref, o_ref, zm_ref, ync_ref, yns_ref, g_ref, w_ref, y_ref):
        ynm = _mla_out(o_ref[...], zm_ref[...].astype(f32), g_ref[...]).astype(bf16)
        y = x_ref[...]
        y = y + jnp.dot(ync_ref[...], w_ref[0:CONV_W, :], preferred_element_type=f32)
        y = y + jnp.dot(ynm, w_ref[CONV_W:CONV_W + MLA_W, :], preferred_element_type=f32)
        y = y + jnp.dot(yns_ref[...], w_ref[CONV_W + MLA_W:, :], preferred_element_type=f32)
        y_ref[...] = y

    row = lambda wd: pl.BlockSpec((tm, wd), lambda i: (i, 0))
    return pl.pallas_call(
        body, name=name, grid=(T // tm,),
        in_specs=[row(D_MODEL), row(MLA_W), row(MLA_W), row(CONV_W), row(SG_W), _const_spec((1, MLA_W)),
                  _const_spec((D_MODEL, D_MODEL))],
        out_specs=row(D_MODEL),
        out_shape=jax.ShapeDtypeStruct((T, D_MODEL), f32),
    )(x, o, proj, ync, yns, g_m, w_out)


def _loss_grad(y, tgt, tm, name):
    T = y.shape[0]
    nt = T // tm

    def body(y_ref, t_ref, dy_ref, loss_ref, acc_ref):
        i = pl.program_id(0)

        @pl.when(i == 0)
        def _():
            acc_ref[...] = jnp.zeros_like(acc_ref)

        err = y_ref[...] - t_ref[...]
        dy_ref[...] = err * (1.0 / D_MODEL)
        acc_ref[...] += jnp.sum(err * err, axis=0, keepdims=True)

        @pl.when(i == nt - 1)
        def _():
            loss_ref[...] = jnp.full((1, HEAD_PAD), 0.5 / D_MODEL, f32) * jnp.sum(acc_ref[...])

    row = pl.BlockSpec((tm, D_MODEL), lambda i: (i, 0))
    return pl.pallas_call(
        body, name=name, grid=(nt,), in_specs=[row, row],
        out_specs=[row, _const_spec((1, HEAD_PAD))],
        out_shape=[jax.ShapeDtypeStruct((T, D_MODEL), f32), jax.ShapeDtypeStruct((1, HEAD_PAD), f32)],
        scratch_shapes=[pltpu.VMEM((1, D_MODEL), f32)],
    )(y, tgt)


def _out_bwd(dout, o, proj, ync, yns, g_m, w_out, tm, name):
    T = dout.shape[0]

    def body(dy_ref, o_ref, zm_ref, ync_ref, yns_ref, g_ref, w_ref,
             do_ref, dl_ref, dzm_ref, dync_ref, dyns_ref, dw_ref, dg_ref):
        i = pl.program_id(0)

        @pl.when(i == 0)
        def _():
            dw_ref[...] = jnp.zeros_like(dw_ref)
            dg_ref[...] = jnp.zeros_like(dg_ref)

        dyb = dy_ref[...].astype(bf16)
        nt = (((1,), (1,)), ((), ()))
        tn = (((0,), (0,)), ((), ()))
        d_c = lax.dot_general(dyb, w_ref[0:CONV_W, :], nt, preferred_element_type=f32)
        d_m = lax.dot_general(dyb, w_ref[CONV_W:CONV_W + MLA_W, :], nt, preferred_element_type=f32)
        d_s = lax.dot_general(dyb, w_ref[CONV_W + MLA_W:, :], nt, preferred_element_type=f32)
        o = o_ref[...]
        ynm, vjp = jax.vjp(_mla_out, o, zm_ref[...].astype(f32), g_ref[...])
        do, dzm, dg = vjp(d_m)
        do_ref[...] = do.astype(bf16)
        dzm_ref[...] = dzm.astype(bf16)
        dync_ref[...] = d_c.astype(bf16)
        dyns_ref[...] = d_s.astype(bf16)
        dg_ref[...] += dg
        dw_ref[0:CONV_W, :] += lax.dot_general(ync_ref[...], dyb, tn, preferred_element_type=f32)
        dw_ref[CONV_W:CONV_W + MLA_W, :] += lax.dot_general(ynm.astype(bf16), dyb, tn, preferred_element_type=f32)
        dw_ref[CONV_W + MLA_W:, :] += lax.dot_general(yns_ref[...], dyb, tn, preferred_element_type=f32)
        prod = do * o
        hi = prod.astype(bf16)
        lo = (prod - hi.astype(f32)).astype(bf16)
        sel = (lax.broadcasted_iota(jnp.int32, (HEADS, MLA_W), 1) // VDIM
               == lax.broadcasted_iota(jnp.int32, (HEADS, MLA_W), 0)).astype(bf16)
        dl_ref[...] = (lax.dot_general(sel, hi, nt, preferred_element_type=f32)
                       + lax.dot_general(sel, lo, nt, preferred_element_type=f32))

    row = lambda wd: pl.BlockSpec((tm, wd), lambda i: (i, 0))
    return pl.pallas_call(
        body, name=name, grid=(T // tm,),
        in_specs=[row(D_MODEL), row(MLA_W), row(MLA_W), row(CONV_W), row(SG_W), _const_spec((1, MLA_W)),
                  _const_spec((D_MODEL, D_MODEL))],
        out_specs=[row(MLA_W), pl.BlockSpec((HEADS, tm), lambda i: (0, i)), row(MLA_W), row(CONV_W), row(SG_W),
                   _const_spec((D_MODEL, D_MODEL)), _const_spec((1, MLA_W))],
        out_shape=[jax.ShapeDtypeStruct((T, MLA_W), bf16), jax.ShapeDtypeStruct((HEADS, T), f32),
                   jax.ShapeDtypeStruct((T, MLA_W), bf16), jax.ShapeDtypeStruct((T, CONV_W), bf16),
                   jax.ShapeDtypeStruct((T, SG_W), bf16), jax.ShapeDtypeStruct((D_MODEL, D_MODEL), f32),
                   jax.ShapeDtypeStruct((1, MLA_W), f32)],
    )(dout, o, proj, ync, yns, g_m, w_out)


def _attn_bwd(q, k, v, do, lse, delta, tb, name, ride=None):
    T = q.shape[0]
    nb = T // tb
    pw = 2 * HEAD_PAD
    nt = (((1,), (1,)), ((), ()))
    tn = (((0,), (0,)), ((), ()))

    def body(q_ref, do_ref, lse_ref, dl_ref, k_ref, v_ref, dq_ref, dk_ref, dv_ref):
        kj = pl.program_id(1)

        @pl.when(kj == 0)
        def _():
            dq_ref[...] = jnp.zeros_like(dq_ref)

        krow = lax.broadcasted_iota(jnp.int32, (tb, tb), 0)
        qcol = lax.broadcasted_iota(jnp.int32, (tb, tb), 1)
        heads = [slice(e * HEAD_PAD, (e + 1) * HEAD_PAD) for e in range(2)]
        kb = [k_ref[:, hs] for hs in heads]
        vb = [v_ref[:, hs] for hs in heads]

        def step(off, carry, masked):
            dob = do_ref[pl.ds(off, tb), :]
            out = []
            for e, hs in enumerate(heads):
                dk, dv = carry[e]
                qb = q_ref[pl.ds(off, tb), hs]
                s_t = lax.dot_general(kb[e], qb, nt, preferred_element_type=f32)
                p_t = jnp.exp2(s_t - lse_ref[e, :, pl.ds(off, tb)])
                if masked:
                    p_t = jnp.where(krow <= qcol, p_t, 0.0)
                dv = dv + jnp.dot(p_t.astype(bf16), dob, preferred_element_type=f32)
                dp_t = lax.dot_general(vb[e], dob, nt, preferred_element_type=f32)
                ds_t = (p_t * (dp_t - dl_ref[e, :, pl.ds(off, tb)])).astype(bf16)
                dk = dk + jnp.dot(ds_t, qb, preferred_element_type=f32)
                dq_ref[pl.ds(off, tb), hs] += lax.dot_general(ds_t, kb[e], tn, preferred_element_type=f32)
                out.append((dk, dv))
            return tuple(out)

        zero = jnp.zeros((tb, HEAD_PAD), f32)
        carry = step(pl.multiple_of(kj * tb, tb), ((zero, zero), (zero, zero)), True)
        carry = lax.fori_loop(kj + 1, nb, lambda qi, cr: step(pl.multiple_of(qi * tb, tb), cr, False), carry)
        for e, hs in enumerate(heads):
            dk_ref[:, hs] = carry[e][0]
            dv_ref[:, hs] = carry[e][1]

    wide = HEADS * HEAD_PAD
    return _call_with_rider(
        body, 6, 3, ride, (HEADS // 2, nb),
        [pl.BlockSpec((T, pw), lambda h, j: (0, h)), pl.BlockSpec((T, HEAD_PAD), lambda h, j: (0, h)),
         pl.BlockSpec((2, 1, T), lambda h, j: (h, 0, 0)), pl.BlockSpec((2, 1, T), lambda h, j: (h, 0, 0)),
         pl.BlockSpec((tb, pw), lambda h, j: (j, h)), pl.BlockSpec((tb, pw), lambda h, j: (j, h))],
        [pl.BlockSpec((T, pw), lambda h, j: (0, h)), pl.BlockSpec((tb, pw), lambda h, j: (j, h)),
         pl.BlockSpec((tb, pw), lambda h, j: (j, h))],
        [jax.ShapeDtypeStruct((T, wide), f32)] * 3,
        (q, do, lse, delta, k, v), name)


def _branch_bwd(proj, tabs, bw, dq, dk, dv, dync, dyns, tm, name):
    T = proj.shape[0]
    nt = T // tm
    hb = tm // HALO
    nw = len(BR_W)

    def body(proj_ref, halo_ref, rc_ref, rs1_ref, rs2_ref, *rest):
        wrefs = rest[:nw]
        dq_ref, dk_ref, dv_ref, dync_ref, dyns_ref = rest[nw:nw + 5]
        dp_ref = rest[nw + 5]
        gwrefs = rest[nw + 6:2 * nw + 6]
        carry_ref = rest[2 * nw + 6]
        i = pl.program_id(0)
        r = nt - 1 - i

        @pl.when(i == 0)
        def _():
            carry_ref[...] = jnp.zeros_like(carry_ref)
            for g in gwrefs:
                g[...] = jnp.zeros_like(g)

        c = {"hmask": (r > 0).astype(f32), "rc": rc_ref[...], "rs1": rs1_ref[...], "rs2": rs2_ref[...]}
        _, vjp = jax.vjp(lambda p_, w_: _branch(p_, w_, c), _load_branch_inputs(proj_ref, halo_ref),
                         _load_branch_weights(wrefs))
        cts = (dq_ref[...] * LN2, dk_ref[...] * LN2, dv_ref[...], dync_ref[...].astype(f32),
               dyns_ref[...].astype(f32))
        dp, dw = vjp(cts)
        for n in SEG:
            if n in ("zm", "a", "glu"):
                continue
            dp_ref[:, SEG[n][0] - 512:SEG[n][1] - 512] = dp[n].astype(bf16)
        for n, hn, lo in (("a", "ha", 0), ("glu", "hglu", CONV_W)):
            d = dp[n]
            tail = d[tm - HALO:, :] + carry_ref[:, lo:lo + CONV_W]
            s0 = SEG[n][0] - 512
            dp_ref[0:tm - HALO, s0:s0 + CONV_W] = d[:tm - HALO, :].astype(bf16)
            dp_ref[tm - HALO:tm, s0:s0 + CONV_W] = tail.astype(bf16)
        carry_ref[:, 0:CONV_W] = dp["ha"]
        carry_ref[:, CONV_W:] = dp["hglu"]
        for (n, _), g in zip(BR_W, gwrefs):
            if n in ("sg_w", "sg_b"):
                for gi in range(SG_HEADS):
                    g[gi] += dw[n][gi]
            else:
                g[...] += dw[n]

    row = lambda wd: pl.BlockSpec((tm, wd), lambda i: (nt - 1 - i, 0))
    wide = HEADS * HEAD_PAD
    return pl.pallas_call(
        body, name=name, grid=(nt,),
        in_specs=[row(PROJ_PAD),
                  pl.BlockSpec((HALO, 2 * CONV_W), lambda i: (jnp.maximum((nt - 1 - i) * hb - 1, 0), 1)),
                  row(HEAD_PAD), row(HEAD_PAD), row(HEAD_PAD)] + [_const_spec(s) for _, s in BR_W]
                 + [row(wide), row(wide), row(wide), row(CONV_W), row(SG_W)],
        out_specs=[row(REST)] + [_const_spec(s) for _, s in BR_W],
        out_shape=[jax.ShapeDtypeStruct((T, REST), bf16)] + [jax.ShapeDtypeStruct(s, f32) for _, s in BR_W],
        scratch_shapes=[pltpu.VMEM((HALO, 2 * CONV_W), f32)],
    )(proj, proj, *tabs, *[bw[n] for n, _ in BR_W], dq, dk, dv, dync, dyns)


def _inproj_bwd_dx(x, g, w, dzm, dprest, dout, tm, name):
    T = x.shape[0]
    nt_dims = (((1,), (1,)), ((), ()))

    def body(x_ref, g_ref, w_ref, dzm_ref, dpr_ref, dout_ref, dx_ref, dg_ref, h_ref):
        i = pl.program_id(0)

        @pl.when(i == 0)
        def _():
            dg_ref[...] = jnp.zeros_like(dg_ref)

        dh = lax.dot_general(dzm_ref[...], w_ref[:, 0:512], nt_dims, preferred_element_type=f32)
        dh = dh + lax.dot_general(dpr_ref[...], w_ref[:, 512:], nt_dims, preferred_element_type=f32)
        h, vjp = jax.vjp(_rms, x_ref[...], g_ref[...])
        dx, dg = vjp(dh)
        dx_ref[...] = dout_ref[...] + dx
        dg_ref[...] += dg
        h_ref[...] = h.astype(bf16)

    row = lambda wd: pl.BlockSpec((tm, wd), lambda i: (i, 0))
    return pl.pallas_call(
        body, name=name, grid=(T // tm,),
        in_specs=[row(D_MODEL), _const_spec((1, D_MODEL)), _const_spec((D_MODEL, PROJ_PAD)), row(512), row(REST),
                  row(D_MODEL)],
        out_specs=[row(D_MODEL), _const_spec((1, D_MODEL)), row(D_MODEL)],
        out_shape=[jax.ShapeDtypeStruct((T, D_MODEL), f32), jax.ShapeDtypeStruct((1, D_MODEL), f32),
                   jax.ShapeDtypeStruct((T, D_MODEL), bf16)],
    )(x, g, w, dzm, dprest, dout)


def _inproj_bwd_dw(h, dzm, dprest, tm, name):
    T = h.shape[0]
    tm = min(4 * tm, T)
    nt = T // tm
    cw = REST // 3
    tn = (((0,), (0,)), ((), ()))

    def body_for(nsteps):
        def body(h_ref, d_ref, o_ref):
            i = pl.program_id(1)

            @pl.when(i == 0)
            def _():
                o_ref[...] = jnp.zeros_like(o_ref)

            o_ref[...] += lax.dot_general(h_ref[...], d_ref[...], tn, preferred_element_type=f32)
        return body

    def run(d, width, cwid, nm):
        return pl.pallas_call(
            body_for(nt), name=nm, grid=(width // cwid, nt),
            in_specs=[pl.BlockSpec((tm, D_MODEL), lambda j, i: (i, 0)), pl.BlockSpec((tm, cwid), lambda j, i: (i, j))],
            out_specs=pl.BlockSpec((D_MODEL, cwid), lambda j, i: (0, j)),
            out_shape=jax.ShapeDtypeStruct((D_MODEL, width), f32),
        )(h, d)

    return run(dzm, 512, 512, name + "_zm"), run(dprest, REST, cw, name + "_rest")


def _row_tile(rows):
    return ROW_ALIGN if rows > ROW_ALIGN and rows % ROW_ALIGN == 0 else rows


def _sum4(buf, name):
    _, R, C = buf.shape
    tr = _row_tile(R)

    def body(b_ref, o_ref):
        b = [b_ref[s].astype(f32) for s in range(4)]
        o_ref[...] = ((b[0] + b[1]) + b[2]) + b[3]

    return pl.pallas_call(
        body, name=name, grid=(R // tr,),
        in_specs=[pl.BlockSpec((4, tr, C), lambda i: (0, i, 0))],
        out_specs=pl.BlockSpec((tr, C), lambda i: (i, 0)),
        out_shape=jax.ShapeDtypeStruct((R, C), f32),
    )(buf)


def _sum4_whole(bufs, name):
    n = len(bufs)

    def body(*refs):
        for b_ref, o_ref in zip(refs[:n], refs[n:]):
            o_ref[...] = ((b_ref[0] + b_ref[1]) + b_ref[2]) + b_ref[3]

    return pl.pallas_call(
        body, name=name, out_shape=[jax.ShapeDtypeStruct(b.shape[1:], f32) for b in bufs])(*bufs)


def _adamw_math(w, m, v, g):
    m_new = ADAM_B1 * m + (1.0 - ADAM_B1) * g
    v_new = ADAM_B2 * v + (1.0 - ADAM_B2) * (g * g)
    m_hat = m_new / (1.0 - ADAM_B1 ** ADAM_STEP)
    v_hat = v_new / (1.0 - ADAM_B2 ** ADAM_STEP)
    return -ADAM_LR * (m_hat / (jnp.sqrt(v_hat) + ADAM_EPS) + ADAM_WD * w), m_new, v_new


def _adamw(w, m, v, p_a, p_b, name):
    R, C = w.shape
    tr = _row_tile(R)

    def body(w_ref, m_ref, v_ref, a_ref, b_ref, g_ref, d_ref, nm_ref, nv_ref):
        g = a_ref[...] + b_ref[...]
        g_ref[...] = g
        d_ref[...], nm_ref[...], nv_ref[...] = _adamw_math(w_ref[...], m_ref[...], v_ref[...], g)

    spec = pl.BlockSpec((tr, C), lambda i: (i, 0))
    return pl.pallas_call(
        body, name=name, grid=(R // tr,), in_specs=[spec] * 5, out_specs=[spec] * 4,
        out_shape=[jax.ShapeDtypeStruct((R, C), f32)] * 4,
    )(w, m, v, p_a, p_b)


def _adamw_whole(ws, ms, vs, p_a, p_b, name):
    n = len(ws)

    def body(*refs):
        ins, outs = refs[:5 * n], refs[5 * n:]
        for i in range(n):
            w, m, v, a, b = (ins[k * n + i][...] for k in range(5))
            g = a + b
            outs[i][...] = g
            outs[n + i][...], outs[2 * n + i][...], outs[3 * n + i][...] = _adamw_math(w, m, v, g)

    res = pl.pallas_call(
        body, name=name, out_shape=[jax.ShapeDtypeStruct(w.shape, f32) for w in ws] * 4,
    )(*ws, *ms, *vs, *p_a, *p_b)
    return [res[k * n:(k + 1) * n] for k in range(4)]


class _ChipExchange:
    def __init__(self, srcs, per_target):
        self.n = len(srcs)
        self.per_target = per_target
        blocks = [s.shape[1:] if p else s.shape for s, p in zip(srcs, per_target)]
        self.out_shape = [jax.ShapeDtypeStruct((4,) + tuple(b), s.dtype) for b, s in zip(blocks, srcs)]
        self.scratch_shapes = [pltpu.SemaphoreType.DMA((3 * self.n,)), pltpu.SemaphoreType.DMA((3 * self.n,)),
                               pltpu.SemaphoreType.DMA((self.n,))]
        self.any_specs = [pl.BlockSpec(memory_space=pl.ANY)] * self.n

    def _copies(self, src_refs, out_refs, sems):
        send_sems, recv_sems, local_sems = sems
        x, y, c = lax.axis_index("x"), lax.axis_index("y"), lax.axis_index("c")
        me = 2 * x + y
        chips = [(1 - x, y), (x, 1 - y), (1 - x, 1 - y)]

        def block_for(i, t):
            return src_refs[i].at[t] if self.per_target[i] else src_refs[i]

        def remote(i, k, block_t, slot):
            tx, ty = chips[k]
            return pltpu.make_async_remote_copy(
                src_ref=block_for(i, block_t), dst_ref=out_refs[i].at[slot], send_sem=send_sems.at[3 * i + k],
                recv_sem=recv_sems.at[3 * i + k], device_id=(tx, ty, c), device_id_type=pl.DeviceIdType.MESH)

        pairs = [(i, k) for i in range(self.n) for k in range(3)]
        sends = [remote(i, k, 2 * chips[k][0] + chips[k][1], me) for i, k in pairs]
        recvs = [remote(i, k, me, 2 * chips[k][0] + chips[k][1]) for i, k in pairs]
        mine = [pltpu.make_async_copy(block_for(i, me), out_refs[i].at[me], local_sems.at[i])
                for i in range(self.n)]
        return sends, recvs, mine

    def start(self, src_refs, out_refs, sems):
        sends, _, mine = self._copies(src_refs, out_refs, sems)
        for cp in sends + mine:
            cp.start()

    def wait(self, src_refs, out_refs, sems):
        sends, recvs, mine = self._copies(src_refs, out_refs, sems)
        for cp in recvs:
            cp.wait_recv()
        for cp in sends:
            cp.wait_send()
        for cp in mine:
            cp.wait()


def _call_with_rider(body, n_in, n_out, ride, grid, in_specs, out_specs, out_shape, args, name):
    if ride is None:
        return pl.pallas_call(body, name=name, grid=grid, in_specs=in_specs, out_specs=out_specs,
                              out_shape=out_shape)(*args)
    ex = _ChipExchange(*ride)
    ne = ex.n

    def wrapped(*refs):
        ins, src_refs = refs[:n_in], refs[n_in:n_in + ne]
        outs, ex_outs = refs[n_in + ne:n_in + ne + n_out], refs[n_in + ne + n_out:n_in + 2 * ne + n_out]
        sems = refs[n_in + 2 * ne + n_out:]
        ids = [pl.program_id(a) for a in range(len(grid))]
        first = functools.reduce(jnp.logical_and, [i == 0 for i in ids])
        last = functools.reduce(jnp.logical_and, [i == g - 1 for i, g in zip(ids, grid)])

        @pl.when(first)
        def _():
            ex.start(src_refs, ex_outs, sems)

        body(*ins, *outs)

        @pl.when(last)
        def _():
            ex.wait(src_refs, ex_outs, sems)

    return pl.pallas_call(
        wrapped, name=name, grid=grid, in_specs=list(in_specs) + ex.any_specs,
        out_specs=list(out_specs) + ex.any_specs, out_shape=list(out_shape) + ex.out_shape,
        scratch_shapes=ex.scratch_shapes,
    )(*args, *ride[0])


def _chip_exchange(srcs, per_target, name):
    ex = _ChipExchange(srcs, per_target)
    n = ex.n

    def body(*refs):
        src_refs, out_refs, sems = refs[:n], refs[n:2 * n], refs[2 * n:]
        ex.start(src_refs, out_refs, sems)
        ex.wait(src_refs, out_refs, sems)

    return pl.pallas_call(
        body, name=name, in_specs=ex.any_specs, out_specs=ex.any_specs, out_shape=ex.out_shape,
        scratch_shapes=ex.scratch_shapes,
    )(*srcs)


def _sibling_exchange(srcs, name):
    n = len(srcs)

    def body(*refs):
        src_refs, out_refs = refs[:n], refs[n:2 * n]
        send_sems, recv_sems = refs[2 * n:]
        x, y, c = lax.axis_index("x"), lax.axis_index("y"), lax.axis_index("c")
        copies = [pltpu.make_async_remote_copy(
            src_ref=src_refs[i], dst_ref=out_refs[i], send_sem=send_sems.at[i], recv_sem=recv_sems.at[i],
            device_id=(x, y, 1 - c), device_id_type=pl.DeviceIdType.MESH) for i in range(n)]
        for cp in copies:
            cp.start()
        for cp in copies:
            cp.wait()

    return pl.pallas_call(
        body, name=name,
        in_specs=[pl.BlockSpec(memory_space=pl.ANY)] * n, out_specs=[pl.BlockSpec(memory_space=pl.ANY)] * n,
        out_shape=[jax.ShapeDtypeStruct(s.shape, s.dtype) for s in srcs],
        scratch_shapes=[pltpu.SemaphoreType.DMA((n,)), pltpu.SemaphoreType.DMA((n,))],
    )(*srcs)


W_IN_SHARD = IN_COLS // 4
W_IN_MAP = [(0, 1792, 512), (1792, 1824, 2368), (1824, 2336, 0), (2336, IN_COLS, 2432)]


def _pad_w_in(shards):
    def cols(a, b):
        out = []
        while a < b:
            j = a // W_IN_SHARD
            e = min(b, (j + 1) * W_IN_SHARD)
            out.append(shards[j][:, a - j * W_IN_SHARD:e - j * W_IN_SHARD])
            a = e
        return out

    z = lambda n: [jnp.zeros((D_MODEL, n), shards.dtype)]
    return jnp.concatenate(cols(1824, 2336) + cols(0, 1792) + z(64) + cols(1792, 1824) + z(32) + cols(2336, IN_COLS),
                           axis=1)


def _w_in_grad_shard(dw_zm, dw_rest, j):
    lo, hi = j * W_IN_SHARD, (j + 1) * W_IN_SHARD
    out = []
    for a, b, p in W_IN_MAP:
        a2, b2 = max(a, lo), min(b, hi)
        if a2 < b2:
            p0 = p + a2 - a
            src, off = (dw_zm, 0) if p0 < 512 else (dw_rest, 512)
            out.append(src[:, p0 - off:p0 - off + b2 - a2])
    return jnp.concatenate(out, axis=1)


def _v_cols():
    return [(h % 2) * VDIM for h in range(HEADS)]


def _layer_weights(gathered, small, l):
    full = {n: jnp.concatenate([g[j] for j in range(4)], axis=BIG[n] - 1) for n, g in gathered.items()
            if n != "w_in"}
    w_uq = full["w_uq"].astype(f32).reshape(Q_LORA, HEADS, QK)
    w_uq = jnp.pad(w_uq, ((0, 0), (0, 0), (0, HEAD_PAD - QK))).reshape(Q_LORA, HEADS * HEAD_PAD)
    ukv = full["w_ukv"].astype(f32).reshape(KV_LORA, HEADS, NOPE + VDIM)
    w_k = jnp.pad(ukv[:, :, :NOPE], ((0, 0), (0, 0), (0, HEAD_PAD - NOPE))).reshape(KV_LORA, HEADS * HEAD_PAD)
    zeros = jnp.zeros((KV_LORA, VDIM), f32)
    w_v = jnp.concatenate(
        [jnp.concatenate([ukv[:, h, NOPE:], zeros] if h % 2 == 0 else [zeros, ukv[:, h, NOPE:]], axis=1)
         for h in range(HEADS)], axis=1)
    row = lambda a: a.reshape(1, -1)
    bn = small["branch_norm_g"][l]
    bw = {
        "conv_w": jnp.pad(full["conv_w"], ((0, 1), (0, 0))), "conv_b": row(small["conv_b"][l]),
        "cln_g": row(small["conv_ln_g"][l]), "cln_b": row(small["conv_ln_b"][l]),
        "pw_w": full["conv_pw_w"].astype(f32), "pw_b": row(small["conv_pw_b"][l]),
        "qn_g": row(small["q_norm_g"][l]), "w_uq": w_uq, "kvn_g": row(small["kv_norm_g"][l]),
        "w_k": w_k, "w_v": w_v,
        "qkq_g": jnp.pad(row(small["qk_q_g"][l]), ((0, 0), (0, HEAD_PAD - QK))),
        "qkk_g": jnp.pad(row(small["qk_k_g"][l]), ((0, 0), (0, HEAD_PAD - QK))),
        "sln_g": row(small["sg_ln_g"][l]), "sln_b": row(small["sg_ln_b"][l]),
        "sg_w": small["sg_w"][l], "sg_b": small["sg_b"][l].reshape(SG_HEADS, SG_CHUNK, 1),
        "bn_c": row(bn[:CONV_W]), "bn_s": row(bn[CONV_W + MLA_W:]),
    }
    return {
        "norm_g": row(small["norm_g"][l]), "w_in": _pad_w_in(gathered["w_in"]), "bw": bw,
        "bn_m": row(bn[CONV_W:CONV_W + MLA_W]), "w_out": full["w_out"],
    }


def _layer_grads(gb, dw_in_pieces, dnorm_g, dw_out, dbn_m):
    duq = gb["w_uq"].reshape(Q_LORA, HEADS, HEAD_PAD)[:, :, :QK].reshape(Q_LORA, HEADS * QK)
    dk = gb["w_k"].reshape(KV_LORA, HEADS, HEAD_PAD)[:, :, :NOPE]
    dv = gb["w_v"].reshape(KV_LORA, HEADS, HEAD_PAD)
    dv = jnp.stack([dv[:, h, c0:c0 + VDIM] for h, c0 in enumerate(_v_cols())], axis=1)
    dukv = jnp.concatenate([dk, dv], axis=2).reshape(KV_LORA, HEADS * (NOPE + VDIM))
    return {
        "norm_g": dnorm_g[0], "w_in": dw_in_pieces, "conv_w": gb["conv_w"][:CONV_K],
        "conv_b": gb["conv_b"][0], "conv_ln_g": gb["cln_g"][0], "conv_ln_b": gb["cln_b"][0],
        "conv_pw_w": gb["pw_w"], "conv_pw_b": gb["pw_b"][0], "q_norm_g": gb["qn_g"][0], "w_uq": duq,
        "kv_norm_g": gb["kvn_g"][0], "w_ukv": dukv, "qk_q_g": gb["qkq_g"][0, :QK], "qk_k_g": gb["qkk_g"][0, :QK],
        "sg_ln_g": gb["sln_g"][0], "sg_ln_b": gb["sln_b"][0], "sg_w": gb["sg_w"], "sg_b": gb["sg_b"][:, :, 0],
        "branch_norm_g": jnp.concatenate([gb["bn_c"][0], dbn_m[0], gb["bn_s"][0]]), "w_out": dw_out,
    }


def _rope_tables(T):
    half = ROPE // 2
    inv_freq = ROPE_THETA ** (-jnp.arange(half, dtype=f32) / half)
    ang = jnp.arange(T, dtype=f32)[:, None] * inv_freq[None, :]
    cos, sin = jnp.cos(ang), jnp.sin(ang)
    one = jnp.ones((T, NOPE), f32)
    z = lambda n: jnp.zeros((T, n), f32)
    rc = jnp.concatenate([one, cos, cos, jnp.ones((T, HEAD_PAD - QK), f32)], axis=1)
    rs1 = jnp.concatenate([z(NOPE), -sin, z(half), z(HEAD_PAD - QK)], axis=1)
    rs2 = jnp.concatenate([z(NOPE), z(half), sin, z(HEAD_PAD - QK)], axis=1)
    return rc, rs1, rs2


def kernel(x, norm_g, w_in, conv_w, conv_b, conv_ln_g, conv_ln_b, conv_pw_w, conv_pw_b, q_norm_g, w_uq, kv_norm_g, w_ukv, qk_q_g, qk_k_g, sg_ln_g, sg_ln_b, sg_w, sg_b, branch_norm_g, w_out, loss_target, m_norm_g, m_w_in, m_conv_w, m_conv_b, m_conv_ln_g, m_conv_ln_b, m_conv_pw_w, m_conv_pw_b, m_q_norm_g, m_w_uq, m_kv_norm_g, m_w_ukv, m_qk_q_g, m_qk_k_g, m_sg_ln_g, m_sg_ln_b, m_sg_w, m_sg_b, m_branch_norm_g, m_w_out, v_norm_g, v_w_in, v_conv_w, v_conv_b, v_conv_ln_g, v_conv_ln_b, v_conv_pw_w, v_conv_pw_b, v_q_norm_g, v_w_uq, v_kv_norm_g, v_w_ukv, v_qk_q_g, v_qk_k_g, v_sg_ln_g, v_sg_ln_b, v_sg_w, v_sg_b, v_branch_norm_g, v_w_out):
    wts = dict(zip(WEIGHTS, [norm_g, w_in, conv_w, conv_b, conv_ln_g, conv_ln_b, conv_pw_w, conv_pw_b, q_norm_g,
                             w_uq, kv_norm_g, w_ukv, qk_q_g, qk_k_g, sg_ln_g, sg_ln_b, sg_w, sg_b, branch_norm_g,
                             w_out]))
    mom_m = dict(zip(WEIGHTS, [m_norm_g, m_w_in, m_conv_w, m_conv_b, m_conv_ln_g, m_conv_ln_b, m_conv_pw_w,
                               m_conv_pw_b, m_q_norm_g, m_w_uq, m_kv_norm_g, m_w_ukv, m_qk_q_g, m_qk_k_g,
                               m_sg_ln_g, m_sg_ln_b, m_sg_w, m_sg_b, m_branch_norm_g, m_w_out]))
    mom_v = dict(zip(WEIGHTS, [v_norm_g, v_w_in, v_conv_w, v_conv_b, v_conv_ln_g, v_conv_ln_b, v_conv_pw_w,
                               v_conv_pw_b, v_q_norm_g, v_w_uq, v_kv_norm_g, v_w_ukv, v_qk_q_g, v_qk_k_g,
                               v_sg_ln_g, v_sg_ln_b, v_sg_w, v_sg_b, v_branch_norm_g, v_w_out]))
    xs = x[0]
    tgt = loss_target[0]
    T = xs.shape[0]
    tm = min(256, T)
    tb = min(512, T // 2)

    big = list(BIG)
    nb = len(big)
    shards = [[wts[n][l] if n == "conv_w" else wts[n][l].astype(bf16) for n in big] for l in range(DEPTH)]
    small = {n: wts[n] for n in SMALL}
    tabs = _rope_tables(T)
    gathered = _chip_exchange(shards[0], [False] * nb, "gather_weights_0")

    acts, lw = [], []
    h_in = xs
    for l in range(DEPTH):
        W = _layer_weights(dict(zip(big, gathered)), small, l)
        lw.append(W)
        proj = _inproj_fwd(h_in, W["norm_g"], W["w_in"], tm, f"inproj_fwd_{l}")
        q, k, v, ync, yns = _branch_fwd(proj, tabs, W["bw"], tm, f"branch_fwd_{l}")
        ride = (shards[l + 1], [False] * nb) if l + 1 < DEPTH else None
        o, lse, *gathered = _attn_fwd(q, k, v, tb, f"attn_fwd_{l}", ride)
        y = _out_fwd(h_in, o, proj, ync, yns, W["bn_m"], W["w_out"], tm, f"out_fwd_{l}")
        acts.append((h_in, proj, q, k, v, ync, yns, o, lse))
        h_in = y

    dy, loss_row = _loss_grad(h_in, tgt, tm, "loss_grad")
    loss = lax.psum(loss_row[0, 0], ("x", "y", "c"))

    def blocks_for_chips(l):
        per_chip = []
        for n in big:
            if n == "w_in":
                blocks = [_w_in_grad_shard(*grads[l][n], j) for j in range(4)]
            else:
                blocks = jnp.split(grads[l][n], 4, axis=BIG[n] - 1)
            per_chip.append(jnp.stack(blocks).astype(bf16))
        return per_chip

    grads = [None] * DEPTH
    contrib = [None] * DEPTH
    for l in reversed(range(DEPTH)):
        W = lw[l]
        xin, proj, q, k, v, ync, yns, o, lse = acts[l]
        do, delta, dzm, dync, dyns, dw_out, dbn_m = _out_bwd(
            dy, o, proj, ync, yns, W["bn_m"], W["w_out"], tm, f"out_bwd_{l}")
        ride = (blocks_for_chips(l + 1), [True] * nb) if l + 1 < DEPTH else None
        dq, dk, dv, *sent = _attn_bwd(q, k, v, do, lse, delta.reshape(HEADS, 1, T), tb, f"attn_bwd_{l}", ride)
        if ride is not None:
            contrib[l + 1] = sent
        outs = _branch_bwd(proj, tabs, W["bw"], dq, dk, dv, dync, dyns, tm, f"branch_bwd_{l}")
        dprest, gb = outs[0], dict(zip([n for n, _ in BR_W], outs[1:]))
        dx, dnorm_g, hb = _inproj_bwd_dx(xin, W["norm_g"], W["w_in"], dzm, dprest, dy, tm, f"inproj_bwd_dx_{l}")
        dw_in_pieces = _inproj_bwd_dw(hb, dzm, dprest, tm, f"inproj_bwd_dw_{l}")
        grads[l] = _layer_grads(gb, dw_in_pieces, dnorm_g, dw_out, dbn_m)
        dy = dx
    grad_x = dy[None]
    g_small = [jnp.stack([grads[l][n] for l in range(DEPTH)]) for n in SMALL]
    last = _chip_exchange(blocks_for_chips(0) + g_small, [True] * nb + [False] * len(SMALL), "exchange_grads_0")
    contrib[0] = last[:nb]
    view = lambda a: a.reshape(-1, a.shape[-1])
    parts = [view(jnp.stack([_sum4(contrib[l][i], f"sum_chips_{n}_{l}") for l in range(DEPTH)]))
             for i, n in enumerate(big)]
    parts += list(_sum4_whole(last[nb:], "sum_chips_small"))
    others = _sibling_exchange(parts, "exchange_cores")

    res = {kind: {} for kind in ("grad", "delta", "new_m", "new_v")}
    for i, n in enumerate(big):
        outs = _adamw(*[view(d[n]) for d in (wts, mom_m, mom_v)], parts[i], others[i], f"adamw_{n}")
        for kind, o in zip(res, outs):
            res[kind][n] = o.reshape(wts[n].shape)
    outs = _adamw_whole(*[[d[n] for n in SMALL] for d in (wts, mom_m, mom_v)], parts[nb:], others[nb:],
                        "adamw_small")
    for kind, o in zip(res, outs):
        res[kind].update(zip(SMALL, o))
    out = [loss, grad_x]
    for kind in ("grad", "delta", "new_m", "new_v"):
        out += [res[kind][n] for n in WEIGHTS]
    return tuple(out)
```

```python
import functools

import jax
import jax.numpy as jnp
from jax import lax
from jax.experimental import pallas as pl
from jax.experimental.pallas import tpu as pltpu

f32 = jnp.float32
bf16 = jnp.bfloat16

DEPTH = 2
D_MODEL = 1024
CONV_W = 256
CONV_K = 31
HEADS = 8
NOPE = 64
ROPE = 32
QK = NOPE + ROPE
VDIM = 64
MLA_W = HEADS * VDIM
Q_LORA = 768
KV_LORA = 256
SG_W = 256
SG_HEADS = 4
SG_CHUNK = 128
IN_COLS = 3104
ROPE_THETA = 10000.0
EPS = 1e-6
HEAD_PAD = 128
LOG2E = 1.4426950408889634
LN2 = 0.6931471805599453
Q_SCALE = QK ** -0.5 * LOG2E
HALO = 32
ROW_ALIGN = 256

ADAM_LR = 0.001
ADAM_B1 = 0.9
ADAM_B2 = 0.999
ADAM_EPS = 1e-08
ADAM_WD = 0.01
ADAM_STEP = 10

PROJ_PAD = 3200
SEG = {
    "zm": (0, 512), "a": (512, 768), "glu": (768, 1024), "zc": (1024, 1280), "cq": (1280, 2048),
    "ckv": (2048, 2304), "kr": (2304, 2432), "us": (2432, 2688), "vs": (2688, 2944), "zs": (2944, 3200),
}
REST = PROJ_PAD - 512

WEIGHTS = ["norm_g", "w_in", "conv_w", "conv_b", "conv_ln_g", "conv_ln_b", "conv_pw_w", "conv_pw_b", "q_norm_g",
           "w_uq", "kv_norm_g", "w_ukv", "qk_q_g", "qk_k_g", "sg_ln_g", "sg_ln_b", "sg_w", "sg_b",
           "branch_norm_g", "w_out"]
BIG = {"w_in": 2, "conv_w": 2, "conv_pw_w": 1, "w_uq": 1, "w_ukv": 2, "w_out": 1}
SMALL = [n for n in WEIGHTS if n not in BIG]

BR_W = [("conv_w", (32, 256)), ("conv_b", (1, 256)), ("cln_g", (1, 256)), ("cln_b", (1, 256)),
        ("pw_w", (256, 256)), ("pw_b", (1, 256)), ("qn_g", (1, 768)), ("w_uq", (768, 1024)),
        ("kvn_g", (1, 256)), ("w_k", (256, 1024)), ("w_v", (256, 1024)), ("qkq_g", (1, 128)),
        ("qkk_g", (1, 128)), ("sln_g", (1, 256)), ("sln_b", (1, 256)), ("sg_w", (4, 128, 128)),
        ("sg_b", (4, 128, 1)), ("bn_c", (1, 256)), ("bn_s", (1, 256))]


@jax.custom_vjp
def _mm(a, w):
    return jnp.dot(a.astype(bf16), w.astype(bf16), preferred_element_type=f32)


def _mm_fwd(a, w):
    return _mm(a, w), (a, w)


def _mm_bwd(res, ct):
    a, w = res
    ctb = ct.astype(bf16)
    da = lax.dot_general(ctb, w.astype(bf16), (((1,), (1,)), ((), ())), preferred_element_type=f32)
    dw = lax.dot_general(a.astype(bf16), ctb, (((0,), (0,)), ((), ())), preferred_element_type=f32)
    return da.astype(a.dtype), dw.astype(w.dtype)


_mm.defvjp(_mm_fwd, _mm_bwd)


@jax.custom_vjp
def _rope(x, c, s1, s2):
    return x * c + pltpu.roll(x, HEAD_PAD - 16, 1) * s1 + pltpu.roll(x, 16, 1) * s2


def _rope_fwd(x, c, s1, s2):
    return _rope(x, c, s1, s2), (c, s1, s2)


def _rope_bwd(res, ct):
    c, s1, s2 = res
    dx = ct * c + pltpu.roll(ct * s1, 16, 1) + pltpu.roll(ct * s2, HEAD_PAD - 16, 1)
    return dx, jnp.zeros_like(c), jnp.zeros_like(s1), jnp.zeros_like(s2)


_rope.defvjp(_rope_fwd, _rope_bwd)


def _rms(x, g):
    return x * lax.rsqrt(jnp.mean(x * x, axis=-1, keepdims=True) + EPS) * g


def _head_rms(x, g):
    return x * lax.rsqrt(jnp.sum(x * x, axis=-1, keepdims=True) * (1.0 / QK) + EPS) * g


def _ln(x, g, b):
    mu = jnp.mean(x, axis=-1, keepdims=True)
    xc = x - mu
    var = jnp.mean(xc * xc, axis=-1, keepdims=True)
    return xc * lax.rsqrt(var + EPS) * g + b


def _branch(p, w, c):
    tm = p["a"].shape[0]
    yg = p["a"] * jax.nn.sigmoid(p["glu"])
    yh = p["ha"] * jax.nn.sigmoid(p["hglu"]) * c["hmask"]
    ycat = jnp.concatenate([yh, yg], axis=0)
    shifted = [ycat] + [ycat[b:b + tm + HALO - 8, :] for b in range(1, 8)]
    acc = jnp.zeros((tm, CONV_W), f32)
    for k in range(CONV_K):
        off = HALO - (CONV_K - 1) + k
        a8 = off - off % 8
        acc = acc + shifted[off % 8][a8:a8 + tm, :] * w["conv_w"][k:k + 1, :]
    yl = jax.nn.silu(_ln(acc + w["conv_b"], w["cln_g"], w["cln_b"]))
    ypw = _mm(yl, w["pw_w"]) + w["pw_b"]
    ync = _rms(ypw * jax.nn.silu(p["zc"]), w["bn_c"])
    cqn = _rms(p["cq"], w["qn_g"])
    qf = _mm(cqn, w["w_uq"])
    ckvn = _rms(p["ckv"], w["kvn_g"])
    kf = _mm(ckvn, w["w_k"])
    v = _mm(ckvn, w["w_v"])
    qs, ks = [], []
    for h in range(HEADS):
        sl = slice(h * HEAD_PAD, (h + 1) * HEAD_PAD)
        qh = _head_rms(qf[:, sl], w["qkq_g"])
        qs.append(_rope(qh, c["rc"], c["rs1"], c["rs2"]) * Q_SCALE)
        kh = _head_rms(kf[:, sl] + p["kr"], w["qkk_g"])
        ks.append(_rope(kh, c["rc"], c["rs1"], c["rs2"]))
    q = jnp.concatenate(qs, axis=-1)
    k = jnp.concatenate(ks, axis=-1)
    u = jax.nn.gelu(p["us"])
    v2 = _ln(jax.nn.gelu(p["vs"]), w["sln_g"], w["sln_b"])
    r = lax.broadcasted_iota(jnp.int32, (SG_CHUNK, SG_CHUNK), 0)
    cc = lax.broadcasted_iota(jnp.int32, (SG_CHUNK, SG_CHUNK), 1)
    lane = lax.broadcasted_iota(jnp.int32, (1, SG_W), 1) // (SG_W // SG_HEADS)
    wm = [jnp.where(r >= cc, w["sg_w"][g], 0.0) for g in range(SG_HEADS)]
    rows = []
    for ci in range(tm // SG_CHUNK):
        vc = v2[ci * SG_CHUNK:(ci + 1) * SG_CHUNK, :]
        mixed = jnp.zeros((SG_CHUNK, SG_W), f32)
        for g in range(SG_HEADS):
            mixed = mixed + jnp.where(lane == g, _mm(wm[g], vc) + w["sg_b"][g], 0.0)
        rows.append(mixed)
    mixed = jnp.concatenate(rows, axis=0) if len(rows) > 1 else rows[0]
    yns = _rms(u * mixed * jax.nn.silu(p["zs"]), w["bn_s"])
    return q, k, v, ync, yns


def _mla_out(o, zm, g):
    return _rms(o * jax.nn.silu(zm), g)


def _load_branch_inputs(proj_ref, halo_ref):
    p = {n: proj_ref[:, SEG[n][0]:SEG[n][1]].astype(f32) for n in SEG if n != "zm"}
    p["ha"] = halo_ref[:, 0:CONV_W].astype(f32)
    p["hglu"] = halo_ref[:, CONV_W:2 * CONV_W].astype(f32)
    return p


def _load_branch_weights(refs):
    w = {}
    for (n, _), r in zip(BR_W, refs):
        if n in ("sg_w", "sg_b"):
            w[n] = [r[g] for g in range(SG_HEADS)]
        else:
            w[n] = r[...]
    return w


def _const_spec(shape):
    nd = len(shape)
    return pl.BlockSpec(shape, lambda *_: (0,) * nd)


def _inproj_fwd(x, g, w, tm, name, ride=None):
    T = x.shape[0]
    nc = 5
    cw = PROJ_PAD // nc

    def body(x_ref, g_ref, w_ref, o_ref):
        h = _rms(x_ref[...], g_ref[...]).astype(bf16)
        for j in range(nc):
            o_ref[:, j * cw:(j + 1) * cw] = jnp.dot(
                h, w_ref[:, j * cw:(j + 1) * cw], preferred_element_type=f32).astype(bf16)

    return _call_with_rider(
        body, 3, 1, ride, (T // tm,),
        [pl.BlockSpec((tm, D_MODEL), lambda i: (i, 0)), _const_spec((1, D_MODEL)), _const_spec((D_MODEL, PROJ_PAD))],
        [pl.BlockSpec((tm, PROJ_PAD), lambda i: (i, 0))], [jax.ShapeDtypeStruct((T, PROJ_PAD), bf16)],
        (x, g, w), name)


def _branch_fwd(proj, tabs, bw, tm, name):
    T = proj.shape[0]
    hb = tm // HALO

    def body(proj_ref, halo_ref, rc_ref, rs1_ref, rs2_ref, *rest):
        wrefs, (q_ref, k_ref, v_ref, ync_ref, yns_ref) = rest[:len(BR_W)], rest[len(BR_W):]
        i = pl.program_id(0)
        c = {"hmask": (i > 0).astype(f32), "rc": rc_ref[...], "rs1": rs1_ref[...], "rs2": rs2_ref[...]}
        q, k, v, ync, yns = _branch(_load_branch_inputs(proj_ref, halo_ref), _load_branch_weights(wrefs), c)
        q_ref[...] = q.astype(bf16)
        k_ref[...] = k.astype(bf16)
        v_ref[...] = v.astype(bf16)
        ync_ref[...] = ync.astype(bf16)
        yns_ref[...] = yns.astype(bf16)

    row = lambda wd: pl.BlockSpec((tm, wd), lambda i: (i, 0))
    wide = HEADS * HEAD_PAD
    return pl.pallas_call(
        body, name=name, grid=(T // tm,),
        in_specs=[row(PROJ_PAD), pl.BlockSpec((HALO, 2 * CONV_W), lambda i: (jnp.maximum(i * hb - 1, 0), 1)),
                  row(HEAD_PAD), row(HEAD_PAD), row(HEAD_PAD)] + [_const_spec(s) for _, s in BR_W],
        out_specs=[row(wide), row(wide), row(wide), row(CONV_W), row(SG_W)],
        out_shape=[jax.ShapeDtypeStruct((T, wide), bf16)] * 3 + [jax.ShapeDtypeStruct((T, CONV_W), bf16)] * 2,
    )(proj, proj, *tabs, *[bw[n] for n, _ in BR_W])


def _attn_fwd(q, k, v, tb, name, ride=None):
    T = q.shape[0]
    nb = T // tb
    pw = 2 * HEAD_PAD

    def body(q_ref, k_ref, v_ref, o_ref, lse_ref):
        qi = pl.program_id(1)
        krow = lax.broadcasted_iota(jnp.int32, (tb, tb), 0)
        qcol = lax.broadcasted_iota(jnp.int32, (tb, tb), 1)
        heads = [slice(e * HEAD_PAD, (e + 1) * HEAD_PAD) for e in range(2)]
        qh = [q_ref[:, hs] for hs in heads]

        def step(off, carry, masked):
            out = []
            for e, hs in enumerate(heads):
                m, l, acc = carry[e]
                kb = k_ref[pl.ds(off, tb), hs]
                vb = v_ref[pl.ds(off, tb), hs]
                s_t = lax.dot_general(kb, qh[e], (((1,), (1,)), ((), ())), preferred_element_type=f32)
                if masked:
                    s_t = jnp.where(krow <= qcol, s_t, -1e30)
                m_new = jnp.maximum(m, jnp.max(s_t, axis=0, keepdims=True))
                p_t = jnp.exp2(s_t - m_new)
                alpha = jnp.exp2(m - m_new)
                l = alpha * l + jnp.sum(p_t, axis=0, keepdims=True)
                acc = alpha * acc + lax.dot_general(vb, p_t.astype(bf16), (((0,), (0,)), ((), ())),
                                                    preferred_element_type=f32)
                out.append((m_new, l, acc))
            return tuple(out)

        one = (jnp.full((1, tb), -1e30, f32), jnp.zeros((1, tb), f32), jnp.zeros((HEAD_PAD, tb), f32))
        carry = lax.fori_loop(0, qi, lambda ki, cr: step(pl.multiple_of(ki * tb, tb), cr, False), (one, one))
        carry = step(pl.multiple_of(qi * tb, tb), carry, True)
        o_t = jnp.zeros((HEAD_PAD, tb), f32)
        for e in range(2):
            m, l, acc = carry[e]
            o_t = o_t + acc / l
            lse_ref[e] = m + jnp.log2(l)
        o_ref[...] = o_t.T

    return _call_with_rider(
        body, 3, 2, ride, (HEADS // 2, nb),
        [pl.BlockSpec((tb, pw), lambda h, i: (i, h)), pl.BlockSpec((T, pw), lambda h, i: (0, h)),
         pl.BlockSpec((T, pw), lambda h, i: (0, h))],
        [pl.BlockSpec((tb, HEAD_PAD), lambda h, i: (i, h)), pl.BlockSpec((2, 1, tb), lambda h, i: (h, 0, i))],
        [jax.ShapeDtypeStruct((T, MLA_W), f32), jax.ShapeDtypeStruct((HEADS, 1, T), f32)],
        (q, k, v), name)


def _out_fwd(x, o, proj, ync, yns, g_m, w_out, tm, name):
    T = x.shape[0]

    def body(x_ref, o_ref, zm_ref, ync_ref, yns_ref, g_ref, w_ref, y_ref):
        ynm = _mla_out(o_ref[...], zm_ref[...].astype(f32), g_ref[...]).astype(bf16)
        y = x_ref[...]
        y = y + jnp.dot(ync_ref[...], w_ref[0:CONV_W, :], preferred_element_type=f32)
        y = y + jnp.dot(ynm, w_ref[CONV_W:CONV_W + MLA_W, :], preferred_element_type=f32)
        y = y + jnp.dot(yns_ref[...], w_ref[CONV_W + MLA_W:, :], preferred_element_type=f32)
        y_ref[...] = y

    row = lambda wd: pl.BlockSpec((tm, wd), lambda i: (i, 0))
    return pl.pallas_call(
        body, name=name, grid=(T // tm,),
        in_specs=[row(D_MODEL), row(MLA_W), row(MLA_W), row(CONV_W), row(SG_W), _const_spec((1, MLA_W)),
                  _const_spec((D_MODEL, D_MODEL))],
        out_specs=row(D_MODEL),
        out_shape=jax.ShapeDtypeStruct((T, D_MODEL), f32),
    )(x, o, proj, ync, yns, g_m, w_out)


def _loss_grad(y, tgt, tm, name):
    T = y.shape[0]
    nt = T // tm

    def body(y_ref, t_ref, dy_ref, loss_ref, acc_ref):
        i = pl.program_id(0)

        @pl.when(i == 0)
        def _():
            acc_ref[...] = jnp.zeros_like(acc_ref)

        err = y_ref[...] - t_ref[...]
        dy_ref[...] = err * (1.0 / D_MODEL)
        acc_ref[...] += jnp.sum(err * err, axis=0, keepdims=True)

        @pl.when(i == nt - 1)
        def _():
            loss_ref[...] = jnp.full((1, HEAD_PAD), 0.5 / D_MODEL, f32) * jnp.sum(acc_ref[...])

    row = pl.BlockSpec((tm, D_MODEL), lambda i: (i, 0))
    return pl.pallas_call(
        body, name=name, grid=(nt,), in_specs=[row, row],
        out_specs=[row, _const_spec((1, HEAD_PAD))],
        out_shape=[jax.ShapeDtypeStruct((T, D_MODEL), f32), jax.ShapeDtypeStruct((1, HEAD_PAD), f32)],
        scratch_shapes=[pltpu.VMEM((1, D_MODEL), f32)],
    )(y, tgt)


def _out_bwd(dout, o, proj, ync, yns, g_m, w_out, tm, name):
    T = dout.shape[0]

    def body(dy_ref, o_ref, zm_ref, ync_ref, yns_ref, g_ref, w_ref,
             do_ref, dl_ref, dzm_ref, dync_ref, dyns_ref, dw_ref, dg_ref):
        i = pl.program_id(0)

        @pl.when(i == 0)
        def _():
            dw_ref[...] = jnp.zeros_like(dw_ref)
            dg_ref[...] = jnp.zeros_like(dg_ref)

        dyb = dy_ref[...].astype(bf16)
        nt = (((1,), (1,)), ((), ()))
        tn = (((0,), (0,)), ((), ()))
        d_c = lax.dot_general(dyb, w_ref[0:CONV_W, :], nt, preferred_element_type=f32)
        d_m = lax.dot_general(dyb, w_ref[CONV_W:CONV_W + MLA_W, :], nt, preferred_element_type=f32)
        d_s = lax.dot_general(dyb, w_ref[CONV_W + MLA_W:, :], nt, preferred_element_type=f32)
        o = o_ref[...]
        ynm, vjp = jax.vjp(_mla_out, o, zm_ref[...].astype(f32), g_ref[...])
        do, dzm, dg = vjp(d_m)
        do_ref[...] = do.astype(bf16)
        dzm_ref[...] = dzm.astype(bf16)
        dync_ref[...] = d_c.astype(bf16)
        dyns_ref[...] = d_s.astype(bf16)
        dg_ref[...] += dg
        dw_ref[0:CONV_W, :] += lax.dot_general(ync_ref[...], dyb, tn, preferred_element_type=f32)
        dw_ref[CONV_W:CONV_W + MLA_W, :] += lax.dot_general(ynm.astype(bf16), dyb, tn, preferred_element_type=f32)
        dw_ref[CONV_W + MLA_W:, :] += lax.dot_general(yns_ref[...], dyb, tn, preferred_element_type=f32)
        prod = do * o
        hi = prod.astype(bf16)
        lo = (prod - hi.astype(f32)).astype(bf16)
        sel = (lax.broadcasted_iota(jnp.int32, (HEADS, MLA_W), 1) // VDIM
               == lax.broadcasted_iota(jnp.int32, (HEADS, MLA_W), 0)).astype(bf16)
        dl_ref[...] = (lax.dot_general(sel, hi, nt, preferred_element_type=f32)
                       + lax.dot_general(sel, lo, nt, preferred_element_type=f32))

    row = lambda wd: pl.BlockSpec((tm, wd), lambda i: (i, 0))
    return pl.pallas_call(
        body, name=name, grid=(T // tm,),
        in_specs=[row(D_MODEL), row(MLA_W), row(MLA_W), row(CONV_W), row(SG_W), _const_spec((1, MLA_W)),
                  _const_spec((D_MODEL, D_MODEL))],
        out_specs=[row(MLA_W), pl.BlockSpec((HEADS, tm), lambda i: (0, i)), row(MLA_W), row(CONV_W), row(SG_W),
                   _const_spec((D_MODEL, D_MODEL)), _const_spec((1, MLA_W))],
        out_shape=[jax.ShapeDtypeStruct((T, MLA_W), bf16), jax.ShapeDtypeStruct((HEADS, T), f32),
                   jax.ShapeDtypeStruct((T, MLA_W), bf16), jax.ShapeDtypeStruct((T, CONV_W), bf16),
                   jax.ShapeDtypeStruct((T, SG_W), bf16), jax.ShapeDtypeStruct((D_MODEL, D_MODEL), f32),
                   jax.ShapeDtypeStruct((1, MLA_W), f32)],
    )(dout, o, proj, ync, yns, g_m, w_out)


def _attn_bwd(q, k, v, do, lse, delta, tb, name, ride=None):
    T = q.shape[0]
    nb = T // tb
    pw = 2 * HEAD_PAD
    nt = (((1,), (1,)), ((), ()))
    tn = (((0,), (0,)), ((), ()))

    def body(q_ref, do_ref, lse_ref, dl_ref, k_ref, v_ref, dq_ref, dk_ref, dv_ref):
        kj = pl.program_id(1)

        @pl.when(kj == 0)
        def _():
            dq_ref[...] = jnp.zeros_like(dq_ref)

        krow = lax.broadcasted_iota(jnp.int32, (tb, tb), 0)
        qcol = lax.broadcasted_iota(jnp.int32, (tb, tb), 1)
        heads = [slice(e * HEAD_PAD, (e + 1) * HEAD_PAD) for e in range(2)]
        kb = [k_ref[:, hs] for hs in heads]
        vb = [v_ref[:, hs] for hs in heads]

        def step(off, carry, masked):
            dob = do_ref[pl.ds(off, tb), :]
            out = []
            for e, hs in enumerate(heads):
                dk, dv = carry[e]
                qb = q_ref[pl.ds(off, tb), hs]
                s_t = lax.dot_general(kb[e], qb, nt, preferred_element_type=f32)
                p_t = jnp.exp2(s_t - lse_ref[e, :, pl.ds(off, tb)])
                if masked:
                    p_t = jnp.where(krow <= qcol, p_t, 0.0)
                dv = dv + jnp.dot(p_t.astype(bf16), dob, preferred_element_type=f32)
                dp_t = lax.dot_general(vb[e], dob, nt, preferred_element_type=f32)
                ds_t = (p_t * (dp_t - dl_ref[e, :, pl.ds(off, tb)])).astype(bf16)
                dk = dk + jnp.dot(ds_t, qb, preferred_element_type=f32)
                dq_ref[pl.ds(off, tb), hs] += lax.dot_general(ds_t, kb[e], tn, preferred_element_type=f32)
                out.append((dk, dv))
            return tuple(out)

        zero = jnp.zeros((tb, HEAD_PAD), f32)
        carry = step(pl.multiple_of(kj * tb, tb), ((zero, zero), (zero, zero)), True)
        carry = lax.fori_loop(kj + 1, nb, lambda qi, cr: step(pl.multiple_of(qi * tb, tb), cr, False), carry)
        for e, hs in enumerate(heads):
            dk_ref[:, hs] = carry[e][0]
            dv_ref[:, hs] = carry[e][1]

    wide = HEADS * HEAD_PAD
    return _call_with_rider(
        body, 6, 3, ride, (HEADS // 2, nb),
        [pl.BlockSpec((T, pw), lambda h, j: (0, h)), pl.BlockSpec((T, HEAD_PAD), lambda h, j: (0, h)),
         pl.BlockSpec((2, 1, T), lambda h, j: (h, 0, 0)), pl.BlockSpec((2, 1, T), lambda h, j: (h, 0, 0)),
         pl.BlockSpec((tb, pw), lambda h, j: (j, h)), pl.BlockSpec((tb, pw), lambda h, j: (j, h))],
        [pl.BlockSpec((T, pw), lambda h, j: (0, h)), pl.BlockSpec((tb, pw), lambda h, j: (j, h)),
         pl.BlockSpec((tb, pw), lambda h, j: (j, h))],
        [jax.ShapeDtypeStruct((T, wide), f32)] * 3,
        (q, do, lse, delta, k, v), name)


def _branch_bwd(proj, tabs, bw, dq, dk, dv, dync, dyns, tm, name):
    T = proj.shape[0]
    nt = T // tm
    hb = tm // HALO
    nw = len(BR_W)

    def body(proj_ref, halo_ref, rc_ref, rs1_ref, rs2_ref, *rest):
        wrefs = rest[:nw]
        dq_ref, dk_ref, dv_ref, dync_ref, dyns_ref = rest[nw:nw + 5]
        dp_ref = rest[nw + 5]
        gwrefs = rest[nw + 6:2 * nw + 6]
        carry_ref = rest[2 * nw + 6]
        i = pl.program_id(0)
        r = nt - 1 - i

        @pl.when(i == 0)
        def _():
            carry_ref[...] = jnp.zeros_like(carry_ref)
            for g in gwrefs:
                g[...] = jnp.zeros_like(g)

        c = {"hmask": (r > 0).astype(f32), "rc": rc_ref[...], "rs1": rs1_ref[...], "rs2": rs2_ref[...]}
        _, vjp = jax.vjp(lambda p_, w_: _branch(p_, w_, c), _load_branch_inputs(proj_ref, halo_ref),
                         _load_branch_weights(wrefs))
        cts = (dq_ref[...] * LN2, dk_ref[...] * LN2, dv_ref[...], dync_ref[...].astype(f32),
               dyns_ref[...].astype(f32))
        dp, dw = vjp(cts)
        for n in SEG:
            if n in ("zm", "a", "glu"):
                continue
            dp_ref[:, SEG[n][0] - 512:SEG[n][1] - 512] = dp[n].astype(bf16)
        for n, hn, lo in (("a", "ha", 0), ("glu", "hglu", CONV_W)):
            d = dp[n]
            tail = d[tm - HALO:, :] + carry_ref[:, lo:lo + CONV_W]
            s0 = SEG[n][0] - 512
            dp_ref[0:tm - HALO, s0:s0 + CONV_W] = d[:tm - HALO, :].astype(bf16)
            dp_ref[tm - HALO:tm, s0:s0 + CONV_W] = tail.astype(bf16)
        carry_ref[:, 0:CONV_W] = dp["ha"]
        carry_ref[:, CONV_W:] = dp["hglu"]
        for (n, _), g in zip(BR_W, gwrefs):
            if n in ("sg_w", "sg_b"):
                for gi in range(SG_HEADS):
                    g[gi] += dw[n][gi]
            else:
                g[...] += dw[n]

    row = lambda wd: pl.BlockSpec((tm, wd), lambda i: (nt - 1 - i, 0))
    wide = HEADS * HEAD_PAD
    return pl.pallas_call(
        body, name=name, grid=(nt,),
        in_specs=[row(PROJ_PAD),
                  pl.BlockSpec((HALO, 2 * CONV_W), lambda i: (jnp.maximum((nt - 1 - i) * hb - 1, 0), 1)),
                  row(HEAD_PAD), row(HEAD_PAD), row(HEAD_PAD)] + [_const_spec(s) for _, s in BR_W]
                 + [row(wide), row(wide), row(wide), row(CONV_W), row(SG_W)],
        out_specs=[row(REST)] + [_const_spec(s) for _, s in BR_W],
        out_shape=[jax.ShapeDtypeStruct((T, REST), bf16)] + [jax.ShapeDtypeStruct(s, f32) for _, s in BR_W],
        scratch_shapes=[pltpu.VMEM((HALO, 2 * CONV_W), f32)],
    )(proj, proj, *tabs, *[bw[n] for n, _ in BR_W], dq, dk, dv, dync, dyns)


def _inproj_bwd_dx(x, g, w, dzm, dprest, dout, tm, name, ride=None):
    T = x.shape[0]
    nt_dims = (((1,), (1,)), ((), ()))

    def body(x_ref, g_ref, w_ref, dzm_ref, dpr_ref, dout_ref, dx_ref, dg_ref, h_ref):
        i = pl.program_id(0)

        @pl.when(i == 0)
        def _():
            dg_ref[...] = jnp.zeros_like(dg_ref)

        dh = lax.dot_general(dzm_ref[...], w_ref[:, 0:512], nt_dims, preferred_element_type=f32)
        dh = dh + lax.dot_general(dpr_ref[...], w_ref[:, 512:], nt_dims, preferred_element_type=f32)
        h, vjp = jax.vjp(_rms, x_ref[...], g_ref[...])
        dx, dg = vjp(dh)
        dx_ref[...] = dout_ref[...] + dx
        dg_ref[...] += dg
        h_ref[...] = h.astype(bf16)

    row = lambda wd: pl.BlockSpec((tm, wd), lambda i: (i, 0))
    return _call_with_rider(
        body, 6, 3, ride, (T // tm,),
        [row(D_MODEL), _const_spec((1, D_MODEL)), _const_spec((D_MODEL, PROJ_PAD)), row(512), row(REST),
         row(D_MODEL)],
        [row(D_MODEL), _const_spec((1, D_MODEL)), row(D_MODEL)],
        [jax.ShapeDtypeStruct((T, D_MODEL), f32), jax.ShapeDtypeStruct((1, D_MODEL), f32),
         jax.ShapeDtypeStruct((T, D_MODEL), bf16)],
        (x, g, w, dzm, dprest, dout), name)


def _inproj_bwd_dw(h, dzm, dprest, tm, name):
    T = h.shape[0]
    tm = min(4 * tm, T)
    nt = T // tm
    cw = REST // 3
    tn = (((0,), (0,)), ((), ()))

    def body_for(nsteps):
        def body(h_ref, d_ref, o_ref):
            i = pl.program_id(1)

            @pl.when(i == 0)
            def _():
                o_ref[...] = jnp.zeros_like(o_ref)

            o_ref[...] += lax.dot_general(h_ref[...], d_ref[...], tn, preferred_element_type=f32)
        return body

    def run(d, width, cwid, nm):
        return pl.pallas_call(
            body_for(nt), name=nm, grid=(width // cwid, nt),
            in_specs=[pl.BlockSpec((tm, D_MODEL), lambda j, i: (i, 0)), pl.BlockSpec((tm, cwid), lambda j, i: (i, j))],
            out_specs=pl.BlockSpec((D_MODEL, cwid), lambda j, i: (0, j)),
            out_shape=jax.ShapeDtypeStruct((D_MODEL, width), f32),
        )(h, d)

    return run(dzm, 512, 512, name + "_zm"), run(dprest, REST, cw, name + "_rest")


def _row_tile(rows):
    return ROW_ALIGN if rows > ROW_ALIGN and rows % ROW_ALIGN == 0 else rows


def _sum4(buf, name):
    _, R, C = buf.shape
    tr = _row_tile(R)

    def body(b_ref, o_ref):
        b = [b_ref[s].astype(f32) for s in range(4)]
        o_ref[...] = ((b[0] + b[1]) + b[2]) + b[3]

    return pl.pallas_call(
        body, name=name, grid=(R // tr,),
        in_specs=[pl.BlockSpec((4, tr, C), lambda i: (0, i, 0))],
        out_specs=pl.BlockSpec((tr, C), lambda i: (i, 0)),
        out_shape=jax.ShapeDtypeStruct((R, C), f32),
    )(buf)


def _sum4_whole(bufs, name):
    n = len(bufs)

    def body(*refs):
        for b_ref, o_ref in zip(refs[:n], refs[n:]):
            o_ref[...] = ((b_ref[0] + b_ref[1]) + b_ref[2]) + b_ref[3]

    return pl.pallas_call(
        body, name=name, out_shape=[jax.ShapeDtypeStruct(b.shape[1:], f32) for b in bufs])(*bufs)


def _adamw_math(w, m, v, g):
    m_new = ADAM_B1 * m + (1.0 - ADAM_B1) * g
    v_new = ADAM_B2 * v + (1.0 - ADAM_B2) * (g * g)
    m_hat = m_new / (1.0 - ADAM_B1 ** ADAM_STEP)
    v_hat = v_new / (1.0 - ADAM_B2 ** ADAM_STEP)
    return -ADAM_LR * (m_hat / (jnp.sqrt(v_hat) + ADAM_EPS) + ADAM_WD * w), m_new, v_new


def _adamw(w, m, v, p_a, p_b, name):
    R, C = w.shape
    tr = _row_tile(R)

    def body(w_ref, m_ref, v_ref, a_ref, b_ref, g_ref, d_ref, nm_ref, nv_ref):
        g = a_ref[...] + b_ref[...]
        g_ref[...] = g
        d_ref[...], nm_ref[...], nv_ref[...] = _adamw_math(w_ref[...], m_ref[...], v_ref[...], g)

    spec = pl.BlockSpec((tr, C), lambda i: (i, 0))
    return pl.pallas_call(
        body, name=name, grid=(R // tr,), in_specs=[spec] * 5, out_specs=[spec] * 4,
        out_shape=[jax.ShapeDtypeStruct((R, C), f32)] * 4,
    )(w, m, v, p_a, p_b)


def _adamw_whole(ws, ms, vs, p_a, p_b, name):
    n = len(ws)

    def body(*refs):
        ins, outs = refs[:5 * n], refs[5 * n:]
        for i in range(n):
            w, m, v, a, b = (ins[k * n + i][...] for k in range(5))
            g = a + b
            outs[i][...] = g
            outs[n + i][...], outs[2 * n + i][...], outs[3 * n + i][...] = _adamw_math(w, m, v, g)

    res = pl.pallas_call(
        body, name=name, out_shape=[jax.ShapeDtypeStruct(w.shape, f32) for w in ws] * 4,
    )(*ws, *ms, *vs, *p_a, *p_b)
    return [res[k * n:(k + 1) * n] for k in range(4)]


class _ChipExchange:
    def __init__(self, srcs, per_target):
        self.n = len(srcs)
        self.per_target = per_target
        blocks = [s.shape[1:] if p else s.shape for s, p in zip(srcs, per_target)]
        self.out_shape = [jax.ShapeDtypeStruct((4,) + tuple(b), s.dtype) for b, s in zip(blocks, srcs)]
        self.scratch_shapes = [pltpu.SemaphoreType.DMA((3 * self.n,)), pltpu.SemaphoreType.DMA((3 * self.n,)),
                               pltpu.SemaphoreType.DMA((self.n,))]
        self.any_specs = [pl.BlockSpec(memory_space=pl.ANY)] * self.n

    def _copies(self, src_refs, out_refs, sems):
        send_sems, recv_sems, local_sems = sems
        x, y, c = lax.axis_index("x"), lax.axis_index("y"), lax.axis_index("c")
        me = 2 * x + y
        chips = [(1 - x, y), (x, 1 - y), (1 - x, 1 - y)]

        def block_for(i, t):
            return src_refs[i].at[t] if self.per_target[i] else src_refs[i]

        def remote(i, k, block_t, slot):
            tx, ty = chips[k]
            return pltpu.make_async_remote_copy(
                src_ref=block_for(i, block_t), dst_ref=out_refs[i].at[slot], send_sem=send_sems.at[3 * i + k],
                recv_sem=recv_sems.at[3 * i + k], device_id=(tx, ty, c), device_id_type=pl.DeviceIdType.MESH)

        pairs = [(i, k) for i in range(self.n) for k in range(3)]
        sends = [remote(i, k, 2 * chips[k][0] + chips[k][1], me) for i, k in pairs]
        recvs = [remote(i, k, me, 2 * chips[k][0] + chips[k][1]) for i, k in pairs]
        mine = [pltpu.make_async_copy(block_for(i, me), out_refs[i].at[me], local_sems.at[i])
                for i in range(self.n)]
        return sends, recvs, mine

    def start(self, src_refs, out_refs, sems):
        sends, _, mine = self._copies(src_refs, out_refs, sems)
        for cp in sends + mine:
            cp.start()

    def wait(self, src_refs, out_refs, sems):
        sends, recvs, mine = self._copies(src_refs, out_refs, sems)
        for cp in recvs:
            cp.wait_recv()
        for cp in sends:
            cp.wait_send()
        for cp in mine:
            cp.wait()


def _call_with_rider(body, n_in, n_out, ride, grid, in_specs, out_specs, out_shape, args, name):
    if ride is None:
        return pl.pallas_call(body, name=name, grid=grid, in_specs=in_specs, out_specs=out_specs,
                              out_shape=out_shape)(*args)
    ex = _ChipExchange(*ride)
    ne = ex.n

    def wrapped(*refs):
        ins, src_refs = refs[:n_in], refs[n_in:n_in + ne]
        outs, ex_outs = refs[n_in + ne:n_in + ne + n_out], refs[n_in + ne + n_out:n_in + 2 * ne + n_out]
        sems = refs[n_in + 2 * ne + n_out:]
        ids = [pl.program_id(a) for a in range(len(grid))]
        first = functools.reduce(jnp.logical_and, [i == 0 for i in ids])
        last = functools.reduce(jnp.logical_and, [i == g - 1 for i, g in zip(ids, grid)])

        @pl.when(first)
        def _():
            ex.start(src_refs, ex_outs, sems)

        body(*ins, *outs)

        @pl.when(last)
        def _():
            ex.wait(src_refs, ex_outs, sems)

    return pl.pallas_call(
        wrapped, name=name, grid=grid, in_specs=list(in_specs) + ex.any_specs,
        out_specs=list(out_specs) + ex.any_specs, out_shape=list(out_shape) + ex.out_shape,
        scratch_shapes=ex.scratch_shapes,
    )(*args, *ride[0])


def _chip_exchange(srcs, per_target, name):
    ex = _ChipExchange(srcs, per_target)
    n = ex.n

    def body(*refs):
        src_refs, out_refs, sems = refs[:n], refs[n:2 * n], refs[2 * n:]
        ex.start(src_refs, out_refs, sems)
        ex.wait(src_refs, out_refs, sems)

    return pl.pallas_call(
        body, name=name, in_specs=ex.any_specs, out_specs=ex.any_specs, out_shape=ex.out_shape,
        scratch_shapes=ex.scratch_shapes,
    )(*srcs)


def _sibling_exchange(srcs, name):
    n = len(srcs)

    def body(*refs):
        src_refs, out_refs = refs[:n], refs[n:2 * n]
        send_sems, recv_sems = refs[2 * n:]
        x, y, c = lax.axis_index("x"), lax.axis_index("y"), lax.axis_index("c")
        copies = [pltpu.make_async_remote_copy(
            src_ref=src_refs[i], dst_ref=out_refs[i], send_sem=send_sems.at[i], recv_sem=recv_sems.at[i],
            device_id=(x, y, 1 - c), device_id_type=pl.DeviceIdType.MESH) for i in range(n)]
        for cp in copies:
            cp.start()
        for cp in copies:
            cp.wait()

    return pl.pallas_call(
        body, name=name,
        in_specs=[pl.BlockSpec(memory_space=pl.ANY)] * n, out_specs=[pl.BlockSpec(memory_space=pl.ANY)] * n,
        out_shape=[jax.ShapeDtypeStruct(s.shape, s.dtype) for s in srcs],
        scratch_shapes=[pltpu.SemaphoreType.DMA((n,)), pltpu.SemaphoreType.DMA((n,))],
    )(*srcs)


W_IN_SHARD = IN_COLS // 4
W_IN_MAP = [(0, 1792, 512), (1792, 1824, 2368), (1824, 2336, 0), (2336, IN_COLS, 2432)]


def _pad_w_in(shards):
    def cols(a, b):
        out = []
        while a < b:
            j = a // W_IN_SHARD
            e = min(b, (j + 1) * W_IN_SHARD)
            out.append(shards[j][:, a - j * W_IN_SHARD:e - j * W_IN_SHARD])
            a = e
        return out

    z = lambda n: [jnp.zeros((D_MODEL, n), shards.dtype)]
    return jnp.concatenate(cols(1824, 2336) + cols(0, 1792) + z(64) + cols(1792, 1824) + z(32) + cols(2336, IN_COLS),
                           axis=1)


def _w_in_grad_shard(dw_zm, dw_rest, j):
    lo, hi = j * W_IN_SHARD, (j + 1) * W_IN_SHARD
    out = []
    for a, b, p in W_IN_MAP:
        a2, b2 = max(a, lo), min(b, hi)
        if a2 < b2:
            p0 = p + a2 - a
            src, off = (dw_zm, 0) if p0 < 512 else (dw_rest, 512)
            out.append(src[:, p0 - off:p0 - off + b2 - a2])
    return jnp.concatenate(out, axis=1)


def _v_cols():
    return [(h % 2) * VDIM for h in range(HEADS)]


def _layer_weights(gathered, small, l):
    full = {n: jnp.concatenate([g[j] for j in range(4)], axis=BIG[n] - 1) for n, g in gathered.items()}
    w_uq = full["w_uq"].astype(f32).reshape(Q_LORA, HEADS, QK)
    w_uq = jnp.pad(w_uq, ((0, 0), (0, 0), (0, HEAD_PAD - QK))).reshape(Q_LORA, HEADS * HEAD_PAD)
    ukv = full["w_ukv"].astype(f32).reshape(KV_LORA, HEADS, NOPE + VDIM)
    w_k = jnp.pad(ukv[:, :, :NOPE], ((0, 0), (0, 0), (0, HEAD_PAD - NOPE))).reshape(KV_LORA, HEADS * HEAD_PAD)
    zeros = jnp.zeros((KV_LORA, VDIM), f32)
    w_v = jnp.concatenate(
        [jnp.concatenate([ukv[:, h, NOPE:], zeros] if h % 2 == 0 else [zeros, ukv[:, h, NOPE:]], axis=1)
         for h in range(HEADS)], axis=1)
    row = lambda a: a.reshape(1, -1)
    bn = small["branch_norm_g"][l]
    bw = {
        "conv_w": jnp.pad(full["conv_w"], ((0, 1), (0, 0))), "conv_b": row(small["conv_b"][l]),
        "cln_g": row(small["conv_ln_g"][l]), "cln_b": row(small["conv_ln_b"][l]),
        "pw_w": full["conv_pw_w"].astype(f32), "pw_b": row(small["conv_pw_b"][l]),
        "qn_g": row(small["q_norm_g"][l]), "w_uq": w_uq, "kvn_g": row(small["kv_norm_g"][l]),
        "w_k": w_k, "w_v": w_v,
        "qkq_g": jnp.pad(row(small["qk_q_g"][l]), ((0, 0), (0, HEAD_PAD - QK))),
        "qkk_g": jnp.pad(row(small["qk_k_g"][l]), ((0, 0), (0, HEAD_PAD - QK))),
        "sln_g": row(small["sg_ln_g"][l]), "sln_b": row(small["sg_ln_b"][l]),
        "sg_w": small["sg_w"][l], "sg_b": small["sg_b"][l].reshape(SG_HEADS, SG_CHUNK, 1),
        "bn_c": row(bn[:CONV_W]), "bn_s": row(bn[CONV_W + MLA_W:]),
    }
    return {"bw": bw, "bn_m": row(bn[CONV_W:CONV_W + MLA_W]), "w_out": full["w_out"]}


def _layer_grads(gb, dw_out, dbn_m):
    duq = gb["w_uq"].reshape(Q_LORA, HEADS, HEAD_PAD)[:, :, :QK].reshape(Q_LORA, HEADS * QK)
    dk = gb["w_k"].reshape(KV_LORA, HEADS, HEAD_PAD)[:, :, :NOPE]
    dv = gb["w_v"].reshape(KV_LORA, HEADS, HEAD_PAD)
    dv = jnp.stack([dv[:, h, c0:c0 + VDIM] for h, c0 in enumerate(_v_cols())], axis=1)
    dukv = jnp.concatenate([dk, dv], axis=2).reshape(KV_LORA, HEADS * (NOPE + VDIM))
    return {
        "conv_w": gb["conv_w"][:CONV_K],
        "conv_b": gb["conv_b"][0], "conv_ln_g": gb["cln_g"][0], "conv_ln_b": gb["cln_b"][0],
        "conv_pw_w": gb["pw_w"], "conv_pw_b": gb["pw_b"][0], "q_norm_g": gb["qn_g"][0], "w_uq": duq,
        "kv_norm_g": gb["kvn_g"][0], "w_ukv": dukv, "qk_q_g": gb["qkq_g"][0, :QK], "qk_k_g": gb["qkk_g"][0, :QK],
        "sg_ln_g": gb["sln_g"][0], "sg_ln_b": gb["sln_b"][0], "sg_w": gb["sg_w"], "sg_b": gb["sg_b"][:, :, 0],
        "branch_norm_g": jnp.concatenate([gb["bn_c"][0], dbn_m[0], gb["bn_s"][0]]), "w_out": dw_out,
    }


def _rope_tables(T):
    half = ROPE // 2
    inv_freq = ROPE_THETA ** (-jnp.arange(half, dtype=f32) / half)
    ang = jnp.arange(T, dtype=f32)[:, None] * inv_freq[None, :]
    cos, sin = jnp.cos(ang), jnp.sin(ang)
    one = jnp.ones((T, NOPE), f32)
    z = lambda n: jnp.zeros((T, n), f32)
    rc = jnp.concatenate([one, cos, cos, jnp.ones((T, HEAD_PAD - QK), f32)], axis=1)
    rs1 = jnp.concatenate([z(NOPE), -sin, z(half), z(HEAD_PAD - QK)], axis=1)
    rs2 = jnp.concatenate([z(NOPE), z(half), sin, z(HEAD_PAD - QK)], axis=1)
    return rc, rs1, rs2


def kernel(x, norm_g, w_in, conv_w, conv_b, conv_ln_g, conv_ln_b, conv_pw_w, conv_pw_b, q_norm_g, w_uq, kv_norm_g, w_ukv, qk_q_g, qk_k_g, sg_ln_g, sg_ln_b, sg_w, sg_b, branch_norm_g, w_out, loss_target, m_norm_g, m_w_in, m_conv_w, m_conv_b, m_conv_ln_g, m_conv_ln_b, m_conv_pw_w, m_conv_pw_b, m_q_norm_g, m_w_uq, m_kv_norm_g, m_w_ukv, m_qk_q_g, m_qk_k_g, m_sg_ln_g, m_sg_ln_b, m_sg_w, m_sg_b, m_branch_norm_g, m_w_out, v_norm_g, v_w_in, v_conv_w, v_conv_b, v_conv_ln_g, v_conv_ln_b, v_conv_pw_w, v_conv_pw_b, v_q_norm_g, v_w_uq, v_kv_norm_g, v_w_ukv, v_qk_q_g, v_qk_k_g, v_sg_ln_g, v_sg_ln_b, v_sg_w, v_sg_b, v_branch_norm_g, v_w_out):
    wts = dict(zip(WEIGHTS, [norm_g, w_in, conv_w, conv_b, conv_ln_g, conv_ln_b, conv_pw_w, conv_pw_b, q_norm_g,
                             w_uq, kv_norm_g, w_ukv, qk_q_g, qk_k_g, sg_ln_g, sg_ln_b, sg_w, sg_b, branch_norm_g,
                             w_out]))
    mom_m = dict(zip(WEIGHTS, [m_norm_g, m_w_in, m_conv_w, m_conv_b, m_conv_ln_g, m_conv_ln_b, m_conv_pw_w,
                               m_conv_pw_b, m_q_norm_g, m_w_uq, m_kv_norm_g, m_w_ukv, m_qk_q_g, m_qk_k_g,
                               m_sg_ln_g, m_sg_ln_b, m_sg_w, m_sg_b, m_branch_norm_g, m_w_out]))
    mom_v = dict(zip(WEIGHTS, [v_norm_g, v_w_in, v_conv_w, v_conv_b, v_conv_ln_g, v_conv_ln_b, v_conv_pw_w,
                               v_conv_pw_b, v_q_norm_g, v_w_uq, v_kv_norm_g, v_w_ukv, v_qk_q_g, v_qk_k_g,
                               v_sg_ln_g, v_sg_ln_b, v_sg_w, v_sg_b, v_branch_norm_g, v_w_out]))
    xs = x[0]
    tgt = loss_target[0]
    T = xs.shape[0]
    tm = min(256, T)
    tmm = min(512, T)
    tb = min(512, T // 2)

    big = list(BIG)
    rest = [n for n in big if n != "w_in"]
    nb = len(big)
    shards = [{n: wts[n][l] if n == "conv_w" else wts[n][l].astype(bf16) for n in big} for l in range(DEPTH)]
    small = {n: wts[n] for n in SMALL}
    tabs = _rope_tables(T)
    gathered = dict(zip(["w_in"], _chip_exchange([shards[0]["w_in"]], [False], "gather_w_in_0")))

    acts, lw = [], []
    h_in = xs
    for l in range(DEPTH):
        W = {"norm_g": small["norm_g"][l].reshape(1, -1), "w_in": _pad_w_in(gathered["w_in"])}
        ride = ([shards[l][n] for n in rest], [False] * len(rest)) if l == 0 else None
        proj, *late = _inproj_fwd(h_in, W["norm_g"], W["w_in"], tmm, f"inproj_fwd_{l}", ride)
        if ride is not None:
            gathered.update(zip(rest, late))
        W.update(_layer_weights({n: gathered[n] for n in rest}, small, l))
        lw.append(W)
        q, k, v, ync, yns = _branch_fwd(proj, tabs, W["bw"], tm, f"branch_fwd_{l}")
        ride = ([shards[l + 1][n] for n in big], [False] * nb) if l + 1 < DEPTH else None
        o, lse, *ahead = _attn_fwd(q, k, v, tb, f"attn_fwd_{l}", ride)
        gathered = dict(zip(big, ahead))
        y = _out_fwd(h_in, o, proj, ync, yns, W["bn_m"], W["w_out"], tmm, f"out_fwd_{l}")
        acts.append((h_in, proj, q, k, v, ync, yns, o, lse))
        h_in = y

    dy, loss_row = _loss_grad(h_in, tgt, tmm, "loss_grad")
    loss = lax.psum(loss_row[0, 0], ("x", "y", "c"))

    def for_chips(n, g):
        if n == "w_in":
            blocks = [_w_in_grad_shard(*g, j) for j in range(4)]
        else:
            blocks = jnp.split(g, 4, axis=BIG[n] - 1)
        return jnp.stack(blocks).astype(bf16)

    mid = [n for n in rest if n != "w_out"]
    grads = [None] * DEPTH
    contrib = {}
    for l in reversed(range(DEPTH)):
        W = lw[l]
        xin, proj, q, k, v, ync, yns, o, lse = acts[l]
        do, delta, dzm, dync, dyns, dw_out, dbn_m = _out_bwd(
            dy, o, proj, ync, yns, W["bn_m"], W["w_out"], tmm, f"out_bwd_{l}")
        keys = [(l, "w_out")] + ([(l + 1, "w_in")] if l + 1 < DEPTH else [])
        srcs = [for_chips("w_out", dw_out)] + ([for_chips("w_in", grads[l + 1]["w_in"])] if l + 1 < DEPTH else [])
        dq, dk, dv, *sent = _attn_bwd(q, k, v, do, lse, delta.reshape(HEADS, 1, T), tb, f"attn_bwd_{l}",
                                      (srcs, [True] * len(srcs)))
        contrib.update(zip(keys, sent))
        outs = _branch_bwd(proj, tabs, W["bw"], dq, dk, dv, dync, dyns, tm, f"branch_bwd_{l}")
        dprest, gb = outs[0], dict(zip([n for n, _ in BR_W], outs[1:]))
        grads[l] = _layer_grads(gb, dw_out, dbn_m)
        dx, dnorm_g, hb, *sent = _inproj_bwd_dx(
            xin, W["norm_g"], W["w_in"], dzm, dprest, dy, tmm, f"inproj_bwd_dx_{l}",
            ([for_chips(n, grads[l][n]) for n in mid], [True] * len(mid)))
        contrib.update(zip([(l, n) for n in mid], sent))
        grads[l]["norm_g"] = dnorm_g[0]
        grads[l]["w_in"] = _inproj_bwd_dw(hb, dzm, dprest, tm, f"inproj_bwd_dw_{l}")
        dy = dx
    grad_x = dy[None]
    g_small = [jnp.stack([grads[l][n] for l in range(DEPTH)]) for n in SMALL]
    last = _chip_exchange([for_chips("w_in", grads[0]["w_in"])] + g_small, [True] + [False] * len(SMALL),
                          "exchange_grads_last")
    contrib[(0, "w_in")] = last[0]
    view = lambda a: a.reshape(-1, a.shape[-1])
    parts = [view(jnp.stack([_sum4(contrib[(l, n)], f"sum_chips_{n}_{l}") for l in range(DEPTH)])) for n in big]
    parts += list(_sum4_whole(last[1:], "sum_chips_small"))
    others = _sibling_exchange(parts, "exchange_cores")

    res = {kind: {} for kind in ("grad", "delta", "new_m", "new_v")}
    for i, n in enumerate(big):
        outs = _adamw(*[view(d[n]) for d in (wts, mom_m, mom_v)], parts[i], others[i], f"adamw_{n}")
        for kind, o in zip(res, outs):
            res[kind][n] = o.reshape(wts[n].shape)
    outs = _adamw_whole(*[[d[n] for n in SMALL] for d in (wts, mom_m, mom_v)], parts[nb:], others[nb:],
                        "adamw_small")
    for kind, o in zip(res, outs):
        res[kind].update(zip(SMALL, o))
    out = [loss, grad_x]
    for kind in ("grad", "delta", "new_m", "new_v"):
        out += [res[kind][n] for n in WEIGHTS]
    return tuple(out)
```

```python
import functools

import jax
import jax.numpy as jnp
from jax import lax
from jax.experimental import pallas as pl
from jax.experimental.pallas import tpu as pltpu

f32 = jnp.float32
bf16 = jnp.bfloat16

DEPTH = 2
D_MODEL = 1024
CONV_W = 256
CONV_K = 31
HEADS = 8
NOPE = 64
ROPE = 32
QK = NOPE + ROPE
VDIM = 64
MLA_W = HEADS * VDIM
Q_LORA = 768
KV_LORA = 256
SG_W = 256
SG_HEADS = 4
SG_CHUNK = 128
IN_COLS = 3104
ROPE_THETA = 10000.0
EPS = 1e-6
HEAD_PAD = 128
LOG2E = 1.4426950408889634
LN2 = 0.6931471805599453
Q_SCALE = QK ** -0.5 * LOG2E
HALO = 32
ROW_ALIGN = 256

ADAM_LR = 0.001
ADAM_B1 = 0.9
ADAM_B2 = 0.999
ADAM_EPS = 1e-08
ADAM_WD = 0.01
ADAM_STEP = 10

PROJ_PAD = 3200
SEG = {
    "zm": (0, 512), "a": (512, 768), "glu": (768, 1024), "zc": (1024, 1280), "cq": (1280, 2048),
    "ckv": (2048, 2304), "kr": (2304, 2432), "us": (2432, 2688), "vs": (2688, 2944), "zs": (2944, 3200),
}
REST = PROJ_PAD - 512

WEIGHTS = ["norm_g", "w_in", "conv_w", "conv_b", "conv_ln_g", "conv_ln_b", "conv_pw_w", "conv_pw_b", "q_norm_g",
           "w_uq", "kv_norm_g", "w_ukv", "qk_q_g", "qk_k_g", "sg_ln_g", "sg_ln_b", "sg_w", "sg_b",
           "branch_norm_g", "w_out"]
BIG = {"w_in": 2, "conv_w": 2, "conv_pw_w": 1, "w_uq": 1, "w_ukv": 2, "w_out": 1}
SMALL = [n for n in WEIGHTS if n not in BIG]

BR_W = [("conv_w", (32, 256)), ("conv_b", (1, 256)), ("cln_g", (1, 256)), ("cln_b", (1, 256)),
        ("pw_w", (256, 256)), ("pw_b", (1, 256)), ("qn_g", (1, 768)), ("w_uq", (768, 1024)),
        ("kvn_g", (1, 256)), ("w_k", (256, 1024)), ("w_v", (256, 1024)), ("qkq_g", (1, 128)),
        ("qkk_g", (1, 128)), ("sln_g", (1, 256)), ("sln_b", (1, 256)), ("sg_w", (4, 128, 128)),
        ("sg_b", (4, 128, 1)), ("bn_c", (1, 256)), ("bn_s", (1, 256))]


@jax.custom_vjp
def _mm(a, w):
    return jnp.dot(a.astype(bf16), w.astype(bf16), preferred_element_type=f32)


def _mm_fwd(a, w):
    return _mm(a, w), (a, w)


def _mm_bwd(res, ct):
    a, w = res
    ctb = ct.astype(bf16)
    da = lax.dot_general(ctb, w.astype(bf16), (((1,), (1,)), ((), ())), preferred_element_type=f32)
    dw = lax.dot_general(a.astype(bf16), ctb, (((0,), (0,)), ((), ())), preferred_element_type=f32)
    return da.astype(a.dtype), dw.astype(w.dtype)


_mm.defvjp(_mm_fwd, _mm_bwd)


@jax.custom_vjp
def _rope(x, c, s1, s2):
    return x * c + pltpu.roll(x, HEAD_PAD - 16, 1) * s1 + pltpu.roll(x, 16, 1) * s2


def _rope_fwd(x, c, s1, s2):
    return _rope(x, c, s1, s2), (c, s1, s2)


def _rope_bwd(res, ct):
    c, s1, s2 = res
    dx = ct * c + pltpu.roll(ct * s1, 16, 1) + pltpu.roll(ct * s2, HEAD_PAD - 16, 1)
    return dx, jnp.zeros_like(c), jnp.zeros_like(s1), jnp.zeros_like(s2)


_rope.defvjp(_rope_fwd, _rope_bwd)


def _rms(x, g):
    return x * lax.rsqrt(jnp.mean(x * x, axis=-1, keepdims=True) + EPS) * g


def _head_rms(x, g):
    return x * lax.rsqrt(jnp.sum(x * x, axis=-1, keepdims=True) * (1.0 / QK) + EPS) * g


def _ln(x, g, b):
    mu = jnp.mean(x, axis=-1, keepdims=True)
    xc = x - mu
    var = jnp.mean(xc * xc, axis=-1, keepdims=True)
    return xc * lax.rsqrt(var + EPS) * g + b


def _branch(p, w, c):
    tm = p["a"].shape[0]
    yg = p["a"] * jax.nn.sigmoid(p["glu"])
    yh = p["ha"] * jax.nn.sigmoid(p["hglu"]) * c["hmask"]
    ycat = jnp.concatenate([yh, yg], axis=0)
    shifted = [ycat] + [ycat[b:b + tm + HALO - 8, :] for b in range(1, 8)]
    acc = jnp.zeros((tm, CONV_W), f32)
    for k in range(CONV_K):
        off = HALO - (CONV_K - 1) + k
        a8 = off - off % 8
        acc = acc + shifted[off % 8][a8:a8 + tm, :] * w["conv_w"][k:k + 1, :]
    yl = jax.nn.silu(_ln(acc + w["conv_b"], w["cln_g"], w["cln_b"]))
    ypw = _mm(yl, w["pw_w"]) + w["pw_b"]
    ync = _rms(ypw * jax.nn.silu(p["zc"]), w["bn_c"])
    cqn = _rms(p["cq"], w["qn_g"])
    qf = _mm(cqn, w["w_uq"])
    ckvn = _rms(p["ckv"], w["kvn_g"])
    kf = _mm(ckvn, w["w_k"])
    v = _mm(ckvn, w["w_v"])
    qs, ks = [], []
    for h in range(HEADS):
        sl = slice(h * HEAD_PAD, (h + 1) * HEAD_PAD)
        qh = _head_rms(qf[:, sl], w["qkq_g"])
        qs.append(_rope(qh, c["rc"], c["rs1"], c["rs2"]) * Q_SCALE)
        kh = _head_rms(kf[:, sl] + p["kr"], w["qkk_g"])
        ks.append(_rope(kh, c["rc"], c["rs1"], c["rs2"]))
    q = jnp.concatenate(qs, axis=-1)
    k = jnp.concatenate(ks, axis=-1)
    u = jax.nn.gelu(p["us"])
    v2 = _ln(jax.nn.gelu(p["vs"]), w["sln_g"], w["sln_b"])
    r = lax.broadcasted_iota(jnp.int32, (SG_CHUNK, SG_CHUNK), 0)
    cc = lax.broadcasted_iota(jnp.int32, (SG_CHUNK, SG_CHUNK), 1)
    lane = lax.broadcasted_iota(jnp.int32, (1, SG_W), 1) // (SG_W // SG_HEADS)
    wm = [jnp.where(r >= cc, w["sg_w"][g], 0.0) for g in range(SG_HEADS)]
    rows = []
    for ci in range(tm // SG_CHUNK):
        vc = v2[ci * SG_CHUNK:(ci + 1) * SG_CHUNK, :]
        mixed = jnp.zeros((SG_CHUNK, SG_W), f32)
        for g in range(SG_HEADS):
            mixed = mixed + jnp.where(lane == g, _mm(wm[g], vc) + w["sg_b"][g], 0.0)
        rows.append(mixed)
    mixed = jnp.concatenate(rows, axis=0) if len(rows) > 1 else rows[0]
    yns = _rms(u * mixed * jax.nn.silu(p["zs"]), w["bn_s"])
    return q, k, v, ync, yns


def _mla_out(o, zm, g):
    return _rms(o * jax.nn.silu(zm), g)


def _load_branch_inputs(proj_ref, halo_ref):
    p = {n: proj_ref[:, SEG[n][0]:SEG[n][1]].astype(f32) for n in SEG if n != "zm"}
    p["ha"] = halo_ref[:, 0:CONV_W].astype(f32)
    p["hglu"] = halo_ref[:, CONV_W:2 * CONV_W].astype(f32)
    return p


def _load_branch_weights(refs):
    w = {}
    for (n, _), r in zip(BR_W, refs):
        if n in ("sg_w", "sg_b"):
            w[n] = [r[g] for g in range(SG_HEADS)]
        else:
            w[n] = r[...]
    return w


def _const_spec(shape):
    nd = len(shape)
    return pl.BlockSpec(shape, lambda *_: (0,) * nd)


def _inproj_fwd(x, g, w, tm, name, ride=None):
    T = x.shape[0]
    nc = 5
    cw = PROJ_PAD // nc

    def body(x_ref, g_ref, w_ref, o_ref):
        h = _rms(x_ref[...], g_ref[...]).astype(bf16)
        for j in range(nc):
            o_ref[:, j * cw:(j + 1) * cw] = jnp.dot(
                h, w_ref[:, j * cw:(j + 1) * cw], preferred_element_type=f32).astype(bf16)

    return _call_with_rider(
        body, 3, 1, ride, (T // tm,),
        [pl.BlockSpec((tm, D_MODEL), lambda i: (i, 0)), _const_spec((1, D_MODEL)), _const_spec((D_MODEL, PROJ_PAD))],
        [pl.BlockSpec((tm, PROJ_PAD), lambda i: (i, 0))], [jax.ShapeDtypeStruct((T, PROJ_PAD), bf16)],
        (x, g, w), name)


def _branch_fwd(proj, tabs, bw, tm, name):
    T = proj.shape[0]
    hb = tm // HALO

    def body(proj_ref, halo_ref, rc_ref, rs1_ref, rs2_ref, *rest):
        wrefs, (q_ref, k_ref, v_ref, ync_ref, yns_ref) = rest[:len(BR_W)], rest[len(BR_W):]
        i = pl.program_id(0)
        c = {"hmask": (i > 0).astype(f32), "rc": rc_ref[...], "rs1": rs1_ref[...], "rs2": rs2_ref[...]}
        q, k, v, ync, yns = _branch(_load_branch_inputs(proj_ref, halo_ref), _load_branch_weights(wrefs), c)
        q_ref[...] = q.astype(bf16)
        k_ref[...] = k.astype(bf16)
        v_ref[...] = v.astype(bf16)
        ync_ref[...] = ync.astype(bf16)
        yns_ref[...] = yns.astype(bf16)

    row = lambda wd: pl.BlockSpec((tm, wd), lambda i: (i, 0))
    wide = HEADS * HEAD_PAD
    return pl.pallas_call(
        body, name=name, grid=(T // tm,),
        in_specs=[row(PROJ_PAD), pl.BlockSpec((HALO, 2 * CONV_W), lambda i: (jnp.maximum(i * hb - 1, 0), 1)),
                  row(HEAD_PAD), row(HEAD_PAD), row(HEAD_PAD)] + [_const_spec(s) for _, s in BR_W],
        out_specs=[row(wide), row(wide), row(wide), row(CONV_W), row(SG_W)],
        out_shape=[jax.ShapeDtypeStruct((T, wide), bf16)] * 3 + [jax.ShapeDtypeStruct((T, CONV_W), bf16)] * 2,
    )(proj, proj, *tabs, *[bw[n] for n, _ in BR_W])


def _attn_fwd(q, k, v, tb, name, ride=None):
    T = q.shape[0]
    nb = T // tb
    pw = 2 * HEAD_PAD

    rc = min(64, tb)

    def body(q_ref, k_ref, v_ref, o_ref, lse_ref, p_scr):
        qi = pl.program_id(1)
        krow = lax.broadcasted_iota(jnp.int32, (rc, tb), 0)
        qcol = lax.broadcasted_iota(jnp.int32, (rc, tb), 1)
        heads = [slice(e * HEAD_PAD, (e + 1) * HEAD_PAD) for e in range(2)]
        qh = [q_ref[:, hs] for hs in heads]

        def step(off, carry, masked):
            s_all = [lax.dot_general(k_ref[pl.ds(off, tb), hs], qh[e], (((1,), (1,)), ((), ())),
                                     preferred_element_type=f32) for e, hs in enumerate(heads)]
            out = []
            for e, hs in enumerate(heads):
                m, l, acc = carry[e]
                blks = []
                m8 = jnp.full((8, tb), -1e30, f32)
                for r in range(0, tb, rc):
                    blk = s_all[e][r:r + rc, :]
                    if masked:
                        blk = jnp.where(krow + r <= qcol, blk, -1e30)
                    blks.append(blk)
                    m8 = jnp.maximum(m8, jnp.max(blk.reshape(rc // 8, 8, tb), axis=0))
                m_new = jnp.maximum(m, jnp.max(m8, axis=0, keepdims=True))
                alpha = jnp.exp2(m - m_new)
                p8 = jnp.zeros((8, tb), f32)
                for r in range(0, tb, rc):
                    p = jnp.exp2(blks[r // rc] - m_new)
                    p8 = p8 + jnp.sum(p.reshape(rc // 8, 8, tb), axis=0)
                    p_scr[e, r:r + rc, :] = p.astype(bf16)
                acc = alpha * acc + lax.dot_general(v_ref[pl.ds(off, tb), hs], p_scr[e], (((0,), (0,)), ((), ())),
                                                    preferred_element_type=f32)
                out.append((m_new, alpha * l + jnp.sum(p8, axis=0, keepdims=True), acc))
            return tuple(out)

        one = (jnp.full((1, tb), -1e30, f32), jnp.zeros((1, tb), f32), jnp.zeros((HEAD_PAD, tb), f32))
        carry = lax.fori_loop(0, qi, lambda ki, cr: step(pl.multiple_of(ki * tb, tb), cr, False), (one, one))
        carry = step(pl.multiple_of(qi * tb, tb), carry, True)
        o_t = jnp.zeros((HEAD_PAD, tb), f32)
        for e in range(2):
            m, l, acc = carry[e]
            o_t = o_t + acc / l
            lse_ref[e] = m + jnp.log2(l)
        o_ref[...] = o_t.T

    return _call_with_rider(
        body, 3, 2, ride, (HEADS // 2, nb),
        [pl.BlockSpec((tb, pw), lambda h, i: (i, h)), pl.BlockSpec((T, pw), lambda h, i: (0, h)),
         pl.BlockSpec((T, pw), lambda h, i: (0, h))],
        [pl.BlockSpec((tb, HEAD_PAD), lambda h, i: (i, h)), pl.BlockSpec((2, 1, tb), lambda h, i: (h, 0, i))],
        [jax.ShapeDtypeStruct((T, MLA_W), f32), jax.ShapeDtypeStruct((HEADS, 1, T), f32)],
        (q, k, v), name, scratch=[pltpu.VMEM((2, tb, tb), bf16)])


def _out_layer(x_ref, o_ref, zm_ref, ync_ref, yns_ref, g_ref, w_ref):
    ynm = _mla_out(o_ref[...], zm_ref[...].astype(f32), g_ref[...]).astype(bf16)
    y = x_ref[...]
    y = y + jnp.dot(ync_ref[...], w_ref[0:CONV_W, :], preferred_element_type=f32)
    y = y + jnp.dot(ynm, w_ref[CONV_W:CONV_W + MLA_W, :], preferred_element_type=f32)
    return y + jnp.dot(yns_ref[...], w_ref[CONV_W + MLA_W:, :], preferred_element_type=f32)


def _out_fwd(x, o, proj, ync, yns, g_m, w_out, tm, name):
    T = x.shape[0]

    def body(*refs):
        refs[-1][...] = _out_layer(*refs[:-1])

    row = lambda wd: pl.BlockSpec((tm, wd), lambda i: (i, 0))
    return pl.pallas_call(
        body, name=name, grid=(T // tm,),
        in_specs=[row(D_MODEL), row(MLA_W), row(MLA_W), row(CONV_W), row(SG_W), _const_spec((1, MLA_W)),
                  _const_spec((D_MODEL, D_MODEL))],
        out_specs=row(D_MODEL),
        out_shape=jax.ShapeDtypeStruct((T, D_MODEL), f32),
    )(x, o, proj, ync, yns, g_m, w_out)


def _out_fwd_loss(x, o, proj, ync, yns, g_m, w_out, tgt, tm, name):
    T = x.shape[0]
    nt = T // tm

    def body(x_ref, o_ref, zm_ref, ync_ref, yns_ref, g_ref, w_ref, t_ref, dy_ref, loss_ref, acc_ref):
        i = pl.program_id(0)

        @pl.when(i == 0)
        def _():
            acc_ref[...] = jnp.zeros_like(acc_ref)

        err = _out_layer(x_ref, o_ref, zm_ref, ync_ref, yns_ref, g_ref, w_ref) - t_ref[...]
        dy_ref[...] = err * (1.0 / D_MODEL)
        acc_ref[...] += jnp.sum(err * err, axis=0, keepdims=True)

        @pl.when(i == nt - 1)
        def _():
            loss_ref[...] = jnp.full((1, HEAD_PAD), 0.5 / D_MODEL, f32) * jnp.sum(acc_ref[...])

    row = lambda wd: pl.BlockSpec((tm, wd), lambda i: (i, 0))
    return pl.pallas_call(
        body, name=name, grid=(nt,),
        in_specs=[row(D_MODEL), row(MLA_W), row(MLA_W), row(CONV_W), row(SG_W), _const_spec((1, MLA_W)),
                  _const_spec((D_MODEL, D_MODEL)), row(D_MODEL)],
        out_specs=[row(D_MODEL), _const_spec((1, HEAD_PAD))],
        out_shape=[jax.ShapeDtypeStruct((T, D_MODEL), f32), jax.ShapeDtypeStruct((1, HEAD_PAD), f32)],
        scratch_shapes=[pltpu.VMEM((1, D_MODEL), f32)],
    )(x, o, proj, ync, yns, g_m, w_out, tgt)


def _out_bwd(dout, o, proj, ync, yns, g_m, w_out, tm, name):
    T = dout.shape[0]

    def body(dy_ref, o_ref, zm_ref, ync_ref, yns_ref, g_ref, w_ref,
             do_ref, dl_ref, dzm_ref, dync_ref, dyns_ref, dw_ref, dg_ref):
        i = pl.program_id(0)

        @pl.when(i == 0)
        def _():
            dw_ref[...] = jnp.zeros_like(dw_ref)
            dg_ref[...] = jnp.zeros_like(dg_ref)

        dyb = dy_ref[...].astype(bf16)
        nt = (((1,), (1,)), ((), ()))
        tn = (((0,), (0,)), ((), ()))
        d_c = lax.dot_general(dyb, w_ref[0:CONV_W, :], nt, preferred_element_type=f32)
        d_m = lax.dot_general(dyb, w_ref[CONV_W:CONV_W + MLA_W, :], nt, preferred_element_type=f32)
        d_s = lax.dot_general(dyb, w_ref[CONV_W + MLA_W:, :], nt, preferred_element_type=f32)
        o = o_ref[...]
        ynm, vjp = jax.vjp(_mla_out, o, zm_ref[...].astype(f32), g_ref[...])
        do, dzm, dg = vjp(d_m)
        do_ref[...] = do.astype(bf16)
        dzm_ref[...] = dzm.astype(bf16)
        dync_ref[...] = d_c.astype(bf16)
        dyns_ref[...] = d_s.astype(bf16)
        dg_ref[...] += dg
        dw_ref[0:CONV_W, :] += lax.dot_general(ync_ref[...], dyb, tn, preferred_element_type=f32)
        dw_ref[CONV_W:CONV_W + MLA_W, :] += lax.dot_general(ynm.astype(bf16), dyb, tn, preferred_element_type=f32)
        dw_ref[CONV_W + MLA_W:, :] += lax.dot_general(yns_ref[...], dyb, tn, preferred_element_type=f32)
        prod = do * o
        hi = prod.astype(bf16)
        lo = (prod - hi.astype(f32)).astype(bf16)
        sel = (lax.broadcasted_iota(jnp.int32, (HEADS, MLA_W), 1) // VDIM
               == lax.broadcasted_iota(jnp.int32, (HEADS, MLA_W), 0)).astype(bf16)
        dl_ref[...] = (lax.dot_general(sel, hi, nt, preferred_element_type=f32)
                       + lax.dot_general(sel, lo, nt, preferred_element_type=f32))

    row = lambda wd: pl.BlockSpec((tm, wd), lambda i: (i, 0))
    return pl.pallas_call(
        body, name=name, grid=(T // tm,),
        in_specs=[row(D_MODEL), row(MLA_W), row(MLA_W), row(CONV_W), row(SG_W), _const_spec((1, MLA_W)),
                  _const_spec((D_MODEL, D_MODEL))],
        out_specs=[row(MLA_W), pl.BlockSpec((HEADS, tm), lambda i: (0, i)), row(MLA_W), row(CONV_W), row(SG_W),
                   _const_spec((D_MODEL, D_MODEL)), _const_spec((1, MLA_W))],
        out_shape=[jax.ShapeDtypeStruct((T, MLA_W), bf16), jax.ShapeDtypeStruct((HEADS, T), f32),
                   jax.ShapeDtypeStruct((T, MLA_W), bf16), jax.ShapeDtypeStruct((T, CONV_W), bf16),
                   jax.ShapeDtypeStruct((T, SG_W), bf16), jax.ShapeDtypeStruct((D_MODEL, D_MODEL), f32),
                   jax.ShapeDtypeStruct((1, MLA_W), f32)],
    )(dout, o, proj, ync, yns, g_m, w_out)


def _attn_bwd(q, k, v, do, lse, delta, tb, name, ride=None):
    T = q.shape[0]
    nb = T // tb
    pw = 2 * HEAD_PAD
    nt = (((1,), (1,)), ((), ()))
    tn = (((0,), (0,)), ((), ()))

    def body(q_ref, do_ref, lse_ref, dl_ref, k_ref, v_ref, dq_ref, dk_ref, dv_ref):
        kj = pl.program_id(1)

        @pl.when(kj == 0)
        def _():
            dq_ref[...] = jnp.zeros_like(dq_ref)

        krow = lax.broadcasted_iota(jnp.int32, (tb, tb), 0)
        qcol = lax.broadcasted_iota(jnp.int32, (tb, tb), 1)
        heads = [slice(e * HEAD_PAD, (e + 1) * HEAD_PAD) for e in range(2)]
        kb = [k_ref[:, hs] for hs in heads]
        vb = [v_ref[:, hs] for hs in heads]

        def step(off, carry, masked):
            dob = do_ref[pl.ds(off, tb), :]
            out = []
            for e, hs in enumerate(heads):
                dk, dv = carry[e]
                qb = q_ref[pl.ds(off, tb), hs]
                s_t = lax.dot_general(kb[e], qb, nt, preferred_element_type=f32)
                p_t = jnp.exp2(s_t - lse_ref[e, :, pl.ds(off, tb)])
                if masked:
                    p_t = jnp.where(krow <= qcol, p_t, 0.0)
                dv = dv + jnp.dot(p_t.astype(bf16), dob, preferred_element_type=f32)
                dp_t = lax.dot_general(vb[e], dob, nt, preferred_element_type=f32)
                ds_t = (p_t * (dp_t - dl_ref[e, :, pl.ds(off, tb)])).astype(bf16)
                dk = dk + jnp.dot(ds_t, qb, preferred_element_type=f32)
                dq_ref[pl.ds(off, tb), hs] += lax.dot_general(ds_t, kb[e], tn, preferred_element_type=f32)
                out.append((dk, dv))
            return tuple(out)

        zero = jnp.zeros((tb, HEAD_PAD), f32)
        carry = step(pl.multiple_of(kj * tb, tb), ((zero, zero), (zero, zero)), True)
        carry = lax.fori_loop(kj + 1, nb, lambda qi, cr: step(pl.multiple_of(qi * tb, tb), cr, False), carry)
        for e, hs in enumerate(heads):
            dk_ref[:, hs] = carry[e][0]
            dv_ref[:, hs] = carry[e][1]

    wide = HEADS * HEAD_PAD
    return _call_with_rider(
        body, 6, 3, ride, (HEADS // 2, nb),
        [pl.BlockSpec((T, pw), lambda h, j: (0, h)), pl.BlockSpec((T, HEAD_PAD), lambda h, j: (0, h)),
         pl.BlockSpec((2, 1, T), lambda h, j: (h, 0, 0)), pl.BlockSpec((2, 1, T), lambda h, j: (h, 0, 0)),
         pl.BlockSpec((tb, pw), lambda h, j: (j, h)), pl.BlockSpec((tb, pw), lambda h, j: (j, h))],
        [pl.BlockSpec((T, pw), lambda h, j: (0, h)), pl.BlockSpec((tb, pw), lambda h, j: (j, h)),
         pl.BlockSpec((tb, pw), lambda h, j: (j, h))],
        [jax.ShapeDtypeStruct((T, wide), f32)] * 3,
        (q, do, lse, delta, k, v), name)


def _branch_bwd(proj, tabs, bw, dq, dk, dv, dync, dyns, tm, name):
    T = proj.shape[0]
    nt = T // tm
    hb = tm // HALO
    nw = len(BR_W)

    def body(proj_ref, halo_ref, rc_ref, rs1_ref, rs2_ref, *rest):
        wrefs = rest[:nw]
        dq_ref, dk_ref, dv_ref, dync_ref, dyns_ref = rest[nw:nw + 5]
        dp_ref = rest[nw + 5]
        gwrefs = rest[nw + 6:2 * nw + 6]
        carry_ref = rest[2 * nw + 6]
        i = pl.program_id(0)
        r = nt - 1 - i

        @pl.when(i == 0)
        def _():
            carry_ref[...] = jnp.zeros_like(carry_ref)
            for g in gwrefs:
                g[...] = jnp.zeros_like(g)

        c = {"hmask": (r > 0).astype(f32), "rc": rc_ref[...], "rs1": rs1_ref[...], "rs2": rs2_ref[...]}
        _, vjp = jax.vjp(lambda p_, w_: _branch(p_, w_, c), _load_branch_inputs(proj_ref, halo_ref),
                         _load_branch_weights(wrefs))
        cts = (dq_ref[...] * LN2, dk_ref[...] * LN2, dv_ref[...], dync_ref[...].astype(f32),
               dyns_ref[...].astype(f32))
        dp, dw = vjp(cts)
        for n in SEG:
            if n in ("zm", "a", "glu"):
                continue
            dp_ref[:, SEG[n][0] - 512:SEG[n][1] - 512] = dp[n].astype(bf16)
        for n, hn, lo in (("a", "ha", 0), ("glu", "hglu", CONV_W)):
            d = dp[n]
            tail = d[tm - HALO:, :] + carry_ref[:, lo:lo + CONV_W]
            s0 = SEG[n][0] - 512
            dp_ref[0:tm - HALO, s0:s0 + CONV_W] = d[:tm - HALO, :].astype(bf16)
            dp_ref[tm - HALO:tm, s0:s0 + CONV_W] = tail.astype(bf16)
        carry_ref[:, 0:CONV_W] = dp["ha"]
        carry_ref[:, CONV_W:] = dp["hglu"]
        for (n, _), g in zip(BR_W, gwrefs):
            if n in ("sg_w", "sg_b"):
                for gi in range(SG_HEADS):
                    g[gi] += dw[n][gi]
            else:
                g[...] += dw[n]

    row = lambda wd: pl.BlockSpec((tm, wd), lambda i: (nt - 1 - i, 0))
    wide = HEADS * HEAD_PAD
    return pl.pallas_call(
        body, name=name, grid=(nt,),
        in_specs=[row(PROJ_PAD),
                  pl.BlockSpec((HALO, 2 * CONV_W), lambda i: (jnp.maximum((nt - 1 - i) * hb - 1, 0), 1)),
                  row(HEAD_PAD), row(HEAD_PAD), row(HEAD_PAD)] + [_const_spec(s) for _, s in BR_W]
                 + [row(wide), row(wide), row(wide), row(CONV_W), row(SG_W)],
        out_specs=[row(REST)] + [_const_spec(s) for _, s in BR_W],
        out_shape=[jax.ShapeDtypeStruct((T, REST), bf16)] + [jax.ShapeDtypeStruct(s, f32) for _, s in BR_W],
        scratch_shapes=[pltpu.VMEM((HALO, 2 * CONV_W), f32)],
    )(proj, proj, *tabs, *[bw[n] for n, _ in BR_W], dq, dk, dv, dync, dyns)


def _inproj_bwd_dx(x, g, w, dzm, dprest, dout, tm, name, ride=None):
    T = x.shape[0]
    nt_dims = (((1,), (1,)), ((), ()))

    def body(x_ref, g_ref, w_ref, dzm_ref, dpr_ref, dout_ref, dx_ref, dg_ref, h_ref):
        i = pl.program_id(0)

        @pl.when(i == 0)
        def _():
            dg_ref[...] = jnp.zeros_like(dg_ref)

        dh = lax.dot_general(dzm_ref[...], w_ref[:, 0:512], nt_dims, preferred_element_type=f32)
        dh = dh + lax.dot_general(dpr_ref[...], w_ref[:, 512:], nt_dims, preferred_element_type=f32)
        h, vjp = jax.vjp(_rms, x_ref[...], g_ref[...])
        dx, dg = vjp(dh)
        dx_ref[...] = dout_ref[...] + dx
        dg_ref[...] += dg
        h_ref[...] = h.astype(bf16)

    row = lambda wd: pl.BlockSpec((tm, wd), lambda i: (i, 0))
    return _call_with_rider(
        body, 6, 3, ride, (T // tm,),
        [row(D_MODEL), _const_spec((1, D_MODEL)), _const_spec((D_MODEL, PROJ_PAD)), row(512), row(REST),
         row(D_MODEL)],
        [row(D_MODEL), _const_spec((1, D_MODEL)), row(D_MODEL)],
        [jax.ShapeDtypeStruct((T, D_MODEL), f32), jax.ShapeDtypeStruct((1, D_MODEL), f32),
         jax.ShapeDtypeStruct((T, D_MODEL), bf16)],
        (x, g, w, dzm, dprest, dout), name)


def _inproj_bwd_dw(h, dzm, dprest, tm, name, ride=None):
    T = h.shape[0]
    tm = min(4 * tm, T)
    nt = T // tm
    cw = REST // 3
    tn = (((0,), (0,)), ((), ()))

    def make_body():
        def body(h_ref, d_ref, o_ref):
            i = pl.program_id(1)

            @pl.when(i == 0)
            def _():
                o_ref[...] = jnp.zeros_like(o_ref)

            o_ref[...] += lax.dot_general(h_ref[...], d_ref[...], tn, preferred_element_type=f32)
        return body

    def run(d, width, cwid, nm, rd):
        return _call_with_rider(
            make_body(), 2, 1, rd, (width // cwid, nt),
            [pl.BlockSpec((tm, D_MODEL), lambda j, i: (i, 0)), pl.BlockSpec((tm, cwid), lambda j, i: (i, j))],
            [pl.BlockSpec((D_MODEL, cwid), lambda j, i: (0, j))], [jax.ShapeDtypeStruct((D_MODEL, width), f32)],
            (h, d), nm)

    (zm,) = run(dzm, 512, 512, name + "_zm", None)
    rest, *sent = run(dprest, REST, cw, name + "_rest", ride)
    return (zm, rest, *sent)


def _row_tile(rows):
    return ROW_ALIGN if rows > ROW_ALIGN and rows % ROW_ALIGN == 0 else rows


def _sum4(buf, name):
    _, R, C = buf.shape
    tr = _row_tile(R)

    def body(b_ref, o_ref):
        b = [b_ref[s].astype(f32) for s in range(4)]
        o_ref[...] = ((b[0] + b[1]) + b[2]) + b[3]

    return pl.pallas_call(
        body, name=name, grid=(R // tr,),
        in_specs=[pl.BlockSpec((4, tr, C), lambda i: (0, i, 0))],
        out_specs=pl.BlockSpec((tr, C), lambda i: (i, 0)),
        out_shape=jax.ShapeDtypeStruct((R, C), f32),
    )(buf)


def _sum4_whole(bufs, name):
    n = len(bufs)

    def body(*refs):
        for b_ref, o_ref in zip(refs[:n], refs[n:]):
            o_ref[...] = ((b_ref[0] + b_ref[1]) + b_ref[2]) + b_ref[3]

    return pl.pallas_call(
        body, name=name, out_shape=[jax.ShapeDtypeStruct(b.shape[1:], f32) for b in bufs])(*bufs)


def _adamw_math(w, m, v, g):
    m_new = ADAM_B1 * m + (1.0 - ADAM_B1) * g
    v_new = ADAM_B2 * v + (1.0 - ADAM_B2) * (g * g)
    m_hat = m_new / (1.0 - ADAM_B1 ** ADAM_STEP)
    v_hat = v_new / (1.0 - ADAM_B2 ** ADAM_STEP)
    return -ADAM_LR * (m_hat / (jnp.sqrt(v_hat) + ADAM_EPS) + ADAM_WD * w), m_new, v_new


def _adamw(w, m, v, p_a, p_b, name):
    R, C = w.shape
    tr = _row_tile(R)

    def body(w_ref, m_ref, v_ref, a_ref, b_ref, g_ref, d_ref, nm_ref, nv_ref):
        g = a_ref[...] + b_ref[...]
        g_ref[...] = g
        d_ref[...], nm_ref[...], nv_ref[...] = _adamw_math(w_ref[...], m_ref[...], v_ref[...], g)

    spec = pl.BlockSpec((tr, C), lambda i: (i, 0))
    return pl.pallas_call(
        body, name=name, grid=(R // tr,), in_specs=[spec] * 5, out_specs=[spec] * 4,
        out_shape=[jax.ShapeDtypeStruct((R, C), f32)] * 4,
    )(w, m, v, p_a, p_b)


def _adamw_whole(ws, ms, vs, p_a, p_b, name):
    n = len(ws)

    def body(*refs):
        ins, outs = refs[:5 * n], refs[5 * n:]
        for i in range(n):
            w, m, v, a, b = (ins[k * n + i][...] for k in range(5))
            g = a + b
            outs[i][...] = g
            outs[n + i][...], outs[2 * n + i][...], outs[3 * n + i][...] = _adamw_math(w, m, v, g)

    res = pl.pallas_call(
        body, name=name, out_shape=[jax.ShapeDtypeStruct(w.shape, f32) for w in ws] * 4,
    )(*ws, *ms, *vs, *p_a, *p_b)
    return [res[k * n:(k + 1) * n] for k in range(4)]


class _ChipExchange:
    def __init__(self, srcs, per_target):
        self.n = len(srcs)
        self.per_target = per_target
        blocks = [s.shape[1:] if p else s.shape for s, p in zip(srcs, per_target)]
        self.out_shape = [jax.ShapeDtypeStruct((4,) + tuple(b), s.dtype) for b, s in zip(blocks, srcs)]
        self.scratch_shapes = [pltpu.SemaphoreType.DMA((3 * self.n,)), pltpu.SemaphoreType.DMA((3 * self.n,)),
                               pltpu.SemaphoreType.DMA((self.n,))]
        self.any_specs = [pl.BlockSpec(memory_space=pl.ANY)] * self.n

    def _copies(self, src_refs, out_refs, sems):
        send_sems, recv_sems, local_sems = sems
        x, y, c = lax.axis_index("x"), lax.axis_index("y"), lax.axis_index("c")
        me = 2 * x + y
        chips = [(1 - x, y), (x, 1 - y), (1 - x, 1 - y)]

        def block_for(i, t):
            return src_refs[i].at[t] if self.per_target[i] else src_refs[i]

        def remote(i, k, block_t, slot):
            tx, ty = chips[k]
            return pltpu.make_async_remote_copy(
                src_ref=block_for(i, block_t), dst_ref=out_refs[i].at[slot], send_sem=send_sems.at[3 * i + k],
                recv_sem=recv_sems.at[3 * i + k], device_id=(tx, ty, c), device_id_type=pl.DeviceIdType.MESH)

        pairs = [(i, k) for i in range(self.n) for k in range(3)]
        sends = [remote(i, k, 2 * chips[k][0] + chips[k][1], me) for i, k in pairs]
        recvs = [remote(i, k, me, 2 * chips[k][0] + chips[k][1]) for i, k in pairs]
        mine = [pltpu.make_async_copy(block_for(i, me), out_refs[i].at[me], local_sems.at[i])
                for i in range(self.n)]
        return sends, recvs, mine

    def start(self, src_refs, out_refs, sems):
        sends, _, mine = self._copies(src_refs, out_refs, sems)
        for cp in sends + mine:
            cp.start()

    def wait(self, src_refs, out_refs, sems):
        sends, recvs, mine = self._copies(src_refs, out_refs, sems)
        for cp in recvs:
            cp.wait_recv()
        for cp in sends:
            cp.wait_send()
        for cp in mine:
            cp.wait()


def _call_with_rider(body, n_in, n_out, ride, grid, in_specs, out_specs, out_shape, args, name, scratch=()):
    scratch = list(scratch)
    if ride is None:
        return pl.pallas_call(body, name=name, grid=grid, in_specs=in_specs, out_specs=out_specs,
                              out_shape=out_shape, scratch_shapes=scratch)(*args)
    ex = _ChipExchange(*ride)
    ne = ex.n

    def wrapped(*refs):
        ins, src_refs = refs[:n_in], refs[n_in:n_in + ne]
        outs, ex_outs = refs[n_in + ne:n_in + ne + n_out], refs[n_in + ne + n_out:n_in + 2 * ne + n_out]
        own = refs[n_in + 2 * ne + n_out:n_in + 2 * ne + n_out + len(scratch)]
        sems = refs[n_in + 2 * ne + n_out + len(scratch):]
        ids = [pl.program_id(a) for a in range(len(grid))]
        first = functools.reduce(jnp.logical_and, [i == 0 for i in ids])
        last = functools.reduce(jnp.logical_and, [i == g - 1 for i, g in zip(ids, grid)])

        @pl.when(first)
        def _():
            ex.start(src_refs, ex_outs, sems)

        body(*ins, *outs, *own)

        @pl.when(last)
        def _():
            ex.wait(src_refs, ex_outs, sems)

    return pl.pallas_call(
        wrapped, name=name, grid=grid, in_specs=list(in_specs) + ex.any_specs,
        out_specs=list(out_specs) + ex.any_specs, out_shape=list(out_shape) + ex.out_shape,
        scratch_shapes=scratch + ex.scratch_shapes,
    )(*args, *ride[0])


def _chip_exchange(srcs, per_target, name):
    ex = _ChipExchange(srcs, per_target)
    n = ex.n

    def body(*refs):
        src_refs, out_refs, sems = refs[:n], refs[n:2 * n], refs[2 * n:]
        ex.start(src_refs, out_refs, sems)
        ex.wait(src_refs, out_refs, sems)

    return pl.pallas_call(
        body, name=name, in_specs=ex.any_specs, out_specs=ex.any_specs, out_shape=ex.out_shape,
        scratch_shapes=ex.scratch_shapes,
    )(*srcs)


def _sibling_exchange(srcs, name):
    n = len(srcs)

    def body(*refs):
        src_refs, out_refs = refs[:n], refs[n:2 * n]
        send_sems, recv_sems = refs[2 * n:]
        x, y, c = lax.axis_index("x"), lax.axis_index("y"), lax.axis_index("c")
        copies = [pltpu.make_async_remote_copy(
            src_ref=src_refs[i], dst_ref=out_refs[i], send_sem=send_sems.at[i], recv_sem=recv_sems.at[i],
            device_id=(x, y, 1 - c), device_id_type=pl.DeviceIdType.MESH) for i in range(n)]
        for cp in copies:
            cp.start()
        for cp in copies:
            cp.wait()

    return pl.pallas_call(
        body, name=name,
        in_specs=[pl.BlockSpec(memory_space=pl.ANY)] * n, out_specs=[pl.BlockSpec(memory_space=pl.ANY)] * n,
        out_shape=[jax.ShapeDtypeStruct(s.shape, s.dtype) for s in srcs],
        scratch_shapes=[pltpu.SemaphoreType.DMA((n,)), pltpu.SemaphoreType.DMA((n,))],
    )(*srcs)


W_IN_SHARD = IN_COLS // 4
W_IN_MAP = [(0, 1792, 512), (1792, 1824, 2368), (1824, 2336, 0), (2336, IN_COLS, 2432)]


def _pad_w_in(shards):
    def cols(a, b):
        out = []
        while a < b:
            j = a // W_IN_SHARD
            e = min(b, (j + 1) * W_IN_SHARD)
            out.append(shards[j][:, a - j * W_IN_SHARD:e - j * W_IN_SHARD])
            a = e
        return out

    z = lambda n: [jnp.zeros((D_MODEL, n), shards.dtype)]
    return jnp.concatenate(cols(1824, 2336) + cols(0, 1792) + z(64) + cols(1792, 1824) + z(32) + cols(2336, IN_COLS),
                           axis=1)


def _w_in_grad_shard(dw_zm, dw_rest, j):
    lo, hi = j * W_IN_SHARD, (j + 1) * W_IN_SHARD
    out = []
    for a, b, p in W_IN_MAP:
        a2, b2 = max(a, lo), min(b, hi)
        if a2 < b2:
            p0 = p + a2 - a
            src, off = (dw_zm, 0) if p0 < 512 else (dw_rest, 512)
            out.append(src[:, p0 - off:p0 - off + b2 - a2])
    return jnp.concatenate(out, axis=1)


def _v_cols():
    return [(h % 2) * VDIM for h in range(HEADS)]


def _layer_weights(gathered, small, l):
    full = {n: jnp.concatenate([g[j] for j in range(4)], axis=BIG[n] - 1) for n, g in gathered.items()}
    w_uq = full["w_uq"].astype(f32).reshape(Q_LORA, HEADS, QK)
    w_uq = jnp.pad(w_uq, ((0, 0), (0, 0), (0, HEAD_PAD - QK))).reshape(Q_LORA, HEADS * HEAD_PAD)
    ukv = full["w_ukv"].astype(f32).reshape(KV_LORA, HEADS, NOPE + VDIM)
    w_k = jnp.pad(ukv[:, :, :NOPE], ((0, 0), (0, 0), (0, HEAD_PAD - NOPE))).reshape(KV_LORA, HEADS * HEAD_PAD)
    zeros = jnp.zeros((KV_LORA, VDIM), f32)
    w_v = jnp.concatenate(
        [jnp.concatenate([ukv[:, h, NOPE:], zeros] if h % 2 == 0 else [zeros, ukv[:, h, NOPE:]], axis=1)
         for h in range(HEADS)], axis=1)
    row = lambda a: a.reshape(1, -1)
    bn = small["branch_norm_g"][l]
    bw = {
        "conv_w": jnp.pad(full["conv_w"], ((0, 1), (0, 0))), "conv_b": row(small["conv_b"][l]),
        "cln_g": row(small["conv_ln_g"][l]), "cln_b": row(small["conv_ln_b"][l]),
        "pw_w": full["conv_pw_w"].astype(f32), "pw_b": row(small["conv_pw_b"][l]),
        "qn_g": row(small["q_norm_g"][l]), "w_uq": w_uq, "kvn_g": row(small["kv_norm_g"][l]),
        "w_k": w_k, "w_v": w_v,
        "qkq_g": jnp.pad(row(small["qk_q_g"][l]), ((0, 0), (0, HEAD_PAD - QK))),
        "qkk_g": jnp.pad(row(small["qk_k_g"][l]), ((0, 0), (0, HEAD_PAD - QK))),
        "sln_g": row(small["sg_ln_g"][l]), "sln_b": row(small["sg_ln_b"][l]),
        "sg_w": small["sg_w"][l], "sg_b": small["sg_b"][l].reshape(SG_HEADS, SG_CHUNK, 1),
        "bn_c": row(bn[:CONV_W]), "bn_s": row(bn[CONV_W + MLA_W:]),
    }
    return {"bw": bw, "bn_m": row(bn[CONV_W:CONV_W + MLA_W]), "w_out": full["w_out"]}


def _layer_grads(gb, dw_out, dbn_m):
    duq = gb["w_uq"].reshape(Q_LORA, HEADS, HEAD_PAD)[:, :, :QK].reshape(Q_LORA, HEADS * QK)
    dk = gb["w_k"].reshape(KV_LORA, HEADS, HEAD_PAD)[:, :, :NOPE]
    dv = gb["w_v"].reshape(KV_LORA, HEADS, HEAD_PAD)
    dv = jnp.stack([dv[:, h, c0:c0 + VDIM] for h, c0 in enumerate(_v_cols())], axis=1)
    dukv = jnp.concatenate([dk, dv], axis=2).reshape(KV_LORA, HEADS * (NOPE + VDIM))
    return {
        "conv_w": gb["conv_w"][:CONV_K],
        "conv_b": gb["conv_b"][0], "conv_ln_g": gb["cln_g"][0], "conv_ln_b": gb["cln_b"][0],
        "conv_pw_w": gb["pw_w"], "conv_pw_b": gb["pw_b"][0], "q_norm_g": gb["qn_g"][0], "w_uq": duq,
        "kv_norm_g": gb["kvn_g"][0], "w_ukv": dukv, "qk_q_g": gb["qkq_g"][0, :QK], "qk_k_g": gb["qkk_g"][0, :QK],
        "sg_ln_g": gb["sln_g"][0], "sg_ln_b": gb["sln_b"][0], "sg_w": gb["sg_w"], "sg_b": gb["sg_b"][:, :, 0],
        "branch_norm_g": jnp.concatenate([gb["bn_c"][0], dbn_m[0], gb["bn_s"][0]]), "w_out": dw_out,
    }


def _rope_tables(T):
    half = ROPE // 2
    inv_freq = ROPE_THETA ** (-jnp.arange(half, dtype=f32) / half)
    ang = jnp.arange(T, dtype=f32)[:, None] * inv_freq[None, :]
    cos, sin = jnp.cos(ang), jnp.sin(ang)
    one = jnp.ones((T, NOPE), f32)
    z = lambda n: jnp.zeros((T, n), f32)
    rc = jnp.concatenate([one, cos, cos, jnp.ones((T, HEAD_PAD - QK), f32)], axis=1)
    rs1 = jnp.concatenate([z(NOPE), -sin, z(half), z(HEAD_PAD - QK)], axis=1)
    rs2 = jnp.concatenate([z(NOPE), z(half), sin, z(HEAD_PAD - QK)], axis=1)
    return rc, rs1, rs2


def kernel(x, norm_g, w_in, conv_w, conv_b, conv_ln_g, conv_ln_b, conv_pw_w, conv_pw_b, q_norm_g, w_uq, kv_norm_g, w_ukv, qk_q_g, qk_k_g, sg_ln_g, sg_ln_b, sg_w, sg_b, branch_norm_g, w_out, loss_target, m_norm_g, m_w_in, m_conv_w, m_conv_b, m_conv_ln_g, m_conv_ln_b, m_conv_pw_w, m_conv_pw_b, m_q_norm_g, m_w_uq, m_kv_norm_g, m_w_ukv, m_qk_q_g, m_qk_k_g, m_sg_ln_g, m_sg_ln_b, m_sg_w, m_sg_b, m_branch_norm_g, m_w_out, v_norm_g, v_w_in, v_conv_w, v_conv_b, v_conv_ln_g, v_conv_ln_b, v_conv_pw_w, v_conv_pw_b, v_q_norm_g, v_w_uq, v_kv_norm_g, v_w_ukv, v_qk_q_g, v_qk_k_g, v_sg_ln_g, v_sg_ln_b, v_sg_w, v_sg_b, v_branch_norm_g, v_w_out):
    wts = dict(zip(WEIGHTS, [norm_g, w_in, conv_w, conv_b, conv_ln_g, conv_ln_b, conv_pw_w, conv_pw_b, q_norm_g,
                             w_uq, kv_norm_g, w_ukv, qk_q_g, qk_k_g, sg_ln_g, sg_ln_b, sg_w, sg_b, branch_norm_g,
                             w_out]))
    mom_m = dict(zip(WEIGHTS, [m_norm_g, m_w_in, m_conv_w, m_conv_b, m_conv_ln_g, m_conv_ln_b, m_conv_pw_w,
                               m_conv_pw_b, m_q_norm_g, m_w_uq, m_kv_norm_g, m_w_ukv, m_qk_q_g, m_qk_k_g,
                               m_sg_ln_g, m_sg_ln_b, m_sg_w, m_sg_b, m_branch_norm_g, m_w_out]))
    mom_v = dict(zip(WEIGHTS, [v_norm_g, v_w_in, v_conv_w, v_conv_b, v_conv_ln_g, v_conv_ln_b, v_conv_pw_w,
                               v_conv_pw_b, v_q_norm_g, v_w_uq, v_kv_norm_g, v_w_ukv, v_qk_q_g, v_qk_k_g,
                               v_sg_ln_g, v_sg_ln_b, v_sg_w, v_sg_b, v_branch_norm_g, v_w_out]))
    xs = x[0]
    tgt = loss_target[0]
    T = xs.shape[0]
    tm = min(256, T)
    tmm = min(512, T)
    tb = min(512, T // 2)

    big = list(BIG)
    rest = [n for n in big if n != "w_in"]
    nb = len(big)
    shards = [{n: wts[n][l] if n == "conv_w" else wts[n][l].astype(bf16) for n in big} for l in range(DEPTH)]
    small = {n: wts[n] for n in SMALL}
    tabs = _rope_tables(T)
    gathered = dict(zip(["w_in"], _chip_exchange([shards[0]["w_in"]], [False], "gather_w_in_0")))

    acts, lw = [], []
    h_in = xs
    for l in range(DEPTH):
        W = {"norm_g": small["norm_g"][l].reshape(1, -1), "w_in": _pad_w_in(gathered["w_in"])}
        ride = ([shards[l][n] for n in rest], [False] * len(rest)) if l == 0 else None
        proj, *late = _inproj_fwd(h_in, W["norm_g"], W["w_in"], tmm, f"inproj_fwd_{l}", ride)
        if ride is not None:
            gathered.update(zip(rest, late))
        W.update(_layer_weights({n: gathered[n] for n in rest}, small, l))
        lw.append(W)
        q, k, v, ync, yns = _branch_fwd(proj, tabs, W["bw"], tm, f"branch_fwd_{l}")
        ride = ([shards[l + 1][n] for n in big], [False] * nb) if l + 1 < DEPTH else None
        o, lse, *ahead = _attn_fwd(q, k, v, tb, f"attn_fwd_{l}", ride)
        gathered = dict(zip(big, ahead))
        acts.append((h_in, proj, q, k, v, ync, yns, o, lse))
        if l + 1 < DEPTH:
            h_in = _out_fwd(h_in, o, proj, ync, yns, W["bn_m"], W["w_out"], tmm, f"out_fwd_{l}")
        else:
            dy, loss_row = _out_fwd_loss(h_in, o, proj, ync, yns, W["bn_m"], W["w_out"], tgt, tmm, f"out_fwd_{l}")
    loss = lax.psum(loss_row[0, 0], ("x", "y", "c"))

    def for_chips(n, g):
        if n == "w_in":
            blocks = [_w_in_grad_shard(*g, j) for j in range(4)]
        else:
            blocks = jnp.split(g, 4, axis=BIG[n] - 1)
        return jnp.stack(blocks).astype(bf16)

    mid = [n for n in rest if n != "w_out"]
    grads = [None] * DEPTH
    contrib = {}
    for l in reversed(range(DEPTH)):
        W = lw[l]
        xin, proj, q, k, v, ync, yns, o, lse = acts[l]
        do, delta, dzm, dync, dyns, dw_out, dbn_m = _out_bwd(
            dy, o, proj, ync, yns, W["bn_m"], W["w_out"], tmm, f"out_bwd_{l}")
        keys = [(l, "w_out")] + ([(l + 1, "w_in")] if l + 1 < DEPTH else [])
        srcs = [for_chips("w_out", dw_out)] + ([for_chips("w_in", grads[l + 1]["w_in"])] if l + 1 < DEPTH else [])
        dq, dk, dv, *sent = _attn_bwd(q, k, v, do, lse, delta.reshape(HEADS, 1, T), tb, f"attn_bwd_{l}",
                                      (srcs, [True] * len(srcs)))
        contrib.update(zip(keys, sent))
        outs = _branch_bwd(proj, tabs, W["bw"], dq, dk, dv, dync, dyns, tm, f"branch_bwd_{l}")
        dprest, gb = outs[0], dict(zip([n for n, _ in BR_W], outs[1:]))
        grads[l] = _layer_grads(gb, dw_out, dbn_m)
        dx, dnorm_g, hb = _inproj_bwd_dx(xin, W["norm_g"], W["w_in"], dzm, dprest, dy, tmm, f"inproj_bwd_dx_{l}")
        dw_zm, dw_rest, *sent = _inproj_bwd_dw(
            hb, dzm, dprest, tm, f"inproj_bwd_dw_{l}", ([for_chips(n, grads[l][n]) for n in mid], [True] * len(mid)))
        contrib.update(zip([(l, n) for n in mid], sent))
        grads[l]["norm_g"] = dnorm_g[0]
        grads[l]["w_in"] = (dw_zm, dw_rest)
        dy = dx
    grad_x = dy[None]
    g_small = [jnp.stack([grads[l][n] for l in range(DEPTH)]) for n in SMALL]
    last = _chip_exchange([for_chips("w_in", grads[0]["w_in"])] + g_small, [True] + [False] * len(SMALL),
                          "exchange_grads_last")
    contrib[(0, "w_in")] = last[0]
    view = lambda a: a.reshape(-1, a.shape[-1])
    parts = [view(jnp.stack([_sum4(contrib[(l, n)], f"sum_chips_{n}_{l}") for l in range(DEPTH)])) for n in big]
    parts += list(_sum4_whole(last[1:], "sum_chips_small"))
    others = _sibling_exchange(parts, "exchange_cores")

    res = {kind: {} for kind in ("grad", "delta", "new_m", "new_v")}
    for i, n in enumerate(big):
        outs = _adamw(*[view(d[n]) for d in (wts, mom_m, mom_v)], parts[i], others[i], f"adamw_{n}")
        for kind, o in zip(res, outs):
            res[kind][n] = o.reshape(wts[n].shape)
    outs = _adamw_whole(*[[d[n] for n in SMALL] for d in (wts, mom_m, mom_v)], parts[nb:], others[nb:],
                        "adamw_small")
    for kind, o in zip(res, outs):
        res[kind].update(zip(SMALL, o))
    out = [loss, grad_x]
    for kind in ("grad", "delta", "new_m", "new_v"):
        out += [res[kind][n] for n in WEIGHTS]
    return tuple(out)
```

```python
import functools

import jax
import jax.numpy as jnp
from jax import lax
from jax.experimental import pallas as pl
from jax.experimental.pallas import tpu as pltpu

f32 = jnp.float32
bf16 = jnp.bfloat16

DEPTH = 2
D_MODEL = 1024
CONV_W = 256
CONV_K = 31
HEADS = 8
NOPE = 64
ROPE = 32
QK = NOPE + ROPE
VDIM = 64
MLA_W = HEADS * VDIM
Q_LORA = 768
KV_LORA = 256
SG_W = 256
SG_HEADS = 4
SG_CHUNK = 128
IN_COLS = 3104
ROPE_THETA = 10000.0
EPS = 1e-6
HEAD_PAD = 128
LOG2E = 1.4426950408889634
LN2 = 0.6931471805599453
Q_SCALE = QK ** -0.5 * LOG2E
HALO = 32
ROW_ALIGN = 256

ADAM_LR = 0.001
ADAM_B1 = 0.9
ADAM_B2 = 0.999
ADAM_EPS = 1e-08
ADAM_WD = 0.01
ADAM_STEP = 10

PROJ_PAD = 3200
SEG = {
    "zm": (0, 512), "a": (512, 768), "glu": (768, 1024), "zc": (1024, 1280), "cq": (1280, 2048),
    "ckv": (2048, 2304), "kr": (2304, 2432), "us": (2432, 2688), "vs": (2688, 2944), "zs": (2944, 3200),
}
REST = PROJ_PAD - 512

WEIGHTS = ["norm_g", "w_in", "conv_w", "conv_b", "conv_ln_g", "conv_ln_b", "conv_pw_w", "conv_pw_b", "q_norm_g",
           "w_uq", "kv_norm_g", "w_ukv", "qk_q_g", "qk_k_g", "sg_ln_g", "sg_ln_b", "sg_w", "sg_b",
           "branch_norm_g", "w_out"]
BIG = {"w_in": 2, "conv_w": 2, "conv_pw_w": 1, "w_uq": 1, "w_ukv": 2, "w_out": 1}
SMALL = [n for n in WEIGHTS if n not in BIG]

BR_W = [("conv_w", (32, 256)), ("conv_b", (1, 256)), ("cln_g", (1, 256)), ("cln_b", (1, 256)),
        ("pw_w", (256, 256)), ("pw_b", (1, 256)), ("qn_g", (1, 768)), ("w_uq", (768, 1024)),
        ("kvn_g", (1, 256)), ("w_k", (256, 1024)), ("w_v", (256, 1024)), ("qkq_g", (1, 128)),
        ("qkk_g", (1, 128)), ("sln_g", (1, 256)), ("sln_b", (1, 256)), ("sg_w", (4, 128, 128)),
        ("sg_b", (4, 128, 1)), ("bn_c", (1, 256)), ("bn_s", (1, 256))]


@jax.custom_vjp
def _mm(a, w):
    return jnp.dot(a.astype(bf16), w.astype(bf16), preferred_element_type=f32)


def _mm_fwd(a, w):
    return _mm(a, w), (a, w)


def _mm_bwd(res, ct):
    a, w = res
    ctb = ct.astype(bf16)
    da = lax.dot_general(ctb, w.astype(bf16), (((1,), (1,)), ((), ())), preferred_element_type=f32)
    dw = lax.dot_general(a.astype(bf16), ctb, (((0,), (0,)), ((), ())), preferred_element_type=f32)
    return da.astype(a.dtype), dw.astype(w.dtype)


_mm.defvjp(_mm_fwd, _mm_bwd)


@jax.custom_vjp
def _rope(x, c, s1, s2):
    return x * c + pltpu.roll(x, HEAD_PAD - 16, 1) * s1 + pltpu.roll(x, 16, 1) * s2


def _rope_fwd(x, c, s1, s2):
    return _rope(x, c, s1, s2), (c, s1, s2)


def _rope_bwd(res, ct):
    c, s1, s2 = res
    dx = ct * c + pltpu.roll(ct * s1, 16, 1) + pltpu.roll(ct * s2, HEAD_PAD - 16, 1)
    return dx, jnp.zeros_like(c), jnp.zeros_like(s1), jnp.zeros_like(s2)


_rope.defvjp(_rope_fwd, _rope_bwd)


def _rms(x, g):
    return x * lax.rsqrt(jnp.mean(x * x, axis=-1, keepdims=True) + EPS) * g


def _head_rms(x, g):
    return x * lax.rsqrt(jnp.sum(x * x, axis=-1, keepdims=True) * (1.0 / QK) + EPS) * g


def _ln(x, g, b):
    mu = jnp.mean(x, axis=-1, keepdims=True)
    xc = x - mu
    var = jnp.mean(xc * xc, axis=-1, keepdims=True)
    return xc * lax.rsqrt(var + EPS) * g + b


def _branch(p, w, c):
    tm = p["a"].shape[0]
    yg = p["a"] * jax.nn.sigmoid(p["glu"])
    yh = p["ha"] * jax.nn.sigmoid(p["hglu"]) * c["hmask"]
    ycat = jnp.concatenate([yh, yg], axis=0)
    shifted = [ycat] + [ycat[b:b + tm + HALO - 8, :] for b in range(1, 8)]
    acc = jnp.zeros((tm, CONV_W), f32)
    for k in range(CONV_K):
        off = HALO - (CONV_K - 1) + k
        a8 = off - off % 8
        acc = acc + shifted[off % 8][a8:a8 + tm, :] * w["conv_w"][k:k + 1, :]
    yl = jax.nn.silu(_ln(acc + w["conv_b"], w["cln_g"], w["cln_b"]))
    ypw = _mm(yl, w["pw_w"]) + w["pw_b"]
    ync = _rms(ypw * jax.nn.silu(p["zc"]), w["bn_c"])
    cqn = _rms(p["cq"], w["qn_g"])
    qf = _mm(cqn, w["w_uq"])
    ckvn = _rms(p["ckv"], w["kvn_g"])
    kf = _mm(ckvn, w["w_k"])
    v = _mm(ckvn, w["w_v"])
    qs, ks = [], []
    for h in range(HEADS):
        sl = slice(h * HEAD_PAD, (h + 1) * HEAD_PAD)
        qh = _head_rms(qf[:, sl], w["qkq_g"])
        qs.append(_rope(qh, c["rc"], c["rs1"], c["rs2"]) * Q_SCALE)
        kh = _head_rms(kf[:, sl] + p["kr"], w["qkk_g"])
        ks.append(_rope(kh, c["rc"], c["rs1"], c["rs2"]))
    q = jnp.concatenate(qs, axis=-1)
    k = jnp.concatenate(ks, axis=-1)
    u = jax.nn.gelu(p["us"])
    v2 = _ln(jax.nn.gelu(p["vs"]), w["sln_g"], w["sln_b"])
    r = lax.broadcasted_iota(jnp.int32, (SG_CHUNK, SG_CHUNK), 0)
    cc = lax.broadcasted_iota(jnp.int32, (SG_CHUNK, SG_CHUNK), 1)
    lane = lax.broadcasted_iota(jnp.int32, (1, SG_W), 1) // (SG_W // SG_HEADS)
    wm = [jnp.where(r >= cc, w["sg_w"][g], 0.0) for g in range(SG_HEADS)]
    rows = []
    for ci in range(tm // SG_CHUNK):
        vc = v2[ci * SG_CHUNK:(ci + 1) * SG_CHUNK, :]
        mixed = jnp.zeros((SG_CHUNK, SG_W), f32)
        for g in range(SG_HEADS):
            mixed = mixed + jnp.where(lane == g, _mm(wm[g], vc) + w["sg_b"][g], 0.0)
        rows.append(mixed)
    mixed = jnp.concatenate(rows, axis=0) if len(rows) > 1 else rows[0]
    yns = _rms(u * mixed * jax.nn.silu(p["zs"]), w["bn_s"])
    return q, k, v, ync, yns


def _mla_out(o, zm, g):
    return _rms(o * jax.nn.silu(zm), g)


def _load_branch_inputs(proj_ref, halo_ref):
    p = {n: proj_ref[:, SEG[n][0]:SEG[n][1]].astype(f32) for n in SEG if n != "zm"}
    p["ha"] = halo_ref[:, 0:CONV_W].astype(f32)
    p["hglu"] = halo_ref[:, CONV_W:2 * CONV_W].astype(f32)
    return p


def _load_branch_weights(refs):
    w = {}
    for (n, _), r in zip(BR_W, refs):
        if n in ("sg_w", "sg_b"):
            w[n] = [r[g] for g in range(SG_HEADS)]
        else:
            w[n] = r[...]
    return w


def _const_spec(shape):
    nd = len(shape)
    return pl.BlockSpec(shape, lambda *_: (0,) * nd)


def _inproj_fwd(x, g, w, tm, name, ride=None):
    T = x.shape[0]
    nc = 5
    cw = PROJ_PAD // nc

    def body(x_ref, g_ref, w_ref, o_ref):
        h = _rms(x_ref[...], g_ref[...]).astype(bf16)
        for j in range(nc):
            o_ref[:, j * cw:(j + 1) * cw] = jnp.dot(
                h, w_ref[:, j * cw:(j + 1) * cw], preferred_element_type=f32).astype(bf16)

    return _call_with_rider(
        body, 3, 1, ride, (T // tm,),
        [pl.BlockSpec((tm, D_MODEL), lambda i: (i, 0)), _const_spec((1, D_MODEL)), _const_spec((D_MODEL, PROJ_PAD))],
        [pl.BlockSpec((tm, PROJ_PAD), lambda i: (i, 0))], [jax.ShapeDtypeStruct((T, PROJ_PAD), bf16)],
        (x, g, w), name)


def _branch_fwd(proj, tabs, bw, tm, name):
    T = proj.shape[0]
    hb = tm // HALO

    def body(proj_ref, halo_ref, rc_ref, rs1_ref, rs2_ref, *rest):
        wrefs, (q_ref, k_ref, v_ref, ync_ref, yns_ref) = rest[:len(BR_W)], rest[len(BR_W):]
        i = pl.program_id(0)
        c = {"hmask": (i > 0).astype(f32), "rc": rc_ref[...], "rs1": rs1_ref[...], "rs2": rs2_ref[...]}
        q, k, v, ync, yns = _branch(_load_branch_inputs(proj_ref, halo_ref), _load_branch_weights(wrefs), c)
        q_ref[...] = q.astype(bf16)
        k_ref[...] = k.astype(bf16)
        v_ref[...] = v.astype(bf16)
        ync_ref[...] = ync.astype(bf16)
        yns_ref[...] = yns.astype(bf16)

    row = lambda wd: pl.BlockSpec((tm, wd), lambda i: (i, 0))
    wide = HEADS * HEAD_PAD
    return pl.pallas_call(
        body, name=name, grid=(T // tm,),
        in_specs=[row(PROJ_PAD), pl.BlockSpec((HALO, 2 * CONV_W), lambda i: (jnp.maximum(i * hb - 1, 0), 1)),
                  row(HEAD_PAD), row(HEAD_PAD), row(HEAD_PAD)] + [_const_spec(s) for _, s in BR_W],
        out_specs=[row(wide), row(wide), row(wide), row(CONV_W), row(SG_W)],
        out_shape=[jax.ShapeDtypeStruct((T, wide), bf16)] * 3 + [jax.ShapeDtypeStruct((T, CONV_W), bf16)] * 2,
    )(proj, proj, *tabs, *[bw[n] for n, _ in BR_W])


def _attn_fwd(q, k, v, tb, name, ride=None):
    T = q.shape[0]
    nb = T // tb
    pw = 2 * HEAD_PAD

    rc = min(64, tb)

    def body(q_ref, k_ref, v_ref, o_ref, lse_ref, p_scr):
        qi = pl.program_id(1)
        krow = lax.broadcasted_iota(jnp.int32, (rc, tb), 0)
        qcol = lax.broadcasted_iota(jnp.int32, (rc, tb), 1)
        heads = [slice(e * HEAD_PAD, (e + 1) * HEAD_PAD) for e in range(2)]
        qh = [q_ref[:, hs] for hs in heads]

        def step(off, carry, masked):
            s_all = [lax.dot_general(k_ref[pl.ds(off, tb), hs], qh[e], (((1,), (1,)), ((), ())),
                                     preferred_element_type=f32) for e, hs in enumerate(heads)]
            out = []
            for e, hs in enumerate(heads):
                m, l, acc = carry[e]
                blks = []
                m8 = jnp.full((8, tb), -1e30, f32)
                for r in range(0, tb, rc):
                    blk = s_all[e][r:r + rc, :]
                    if masked:
                        blk = jnp.where(krow + r <= qcol, blk, -1e30)
                    blks.append(blk)
                    m8 = jnp.maximum(m8, jnp.max(blk.reshape(rc // 8, 8, tb), axis=0))
                m_new = jnp.maximum(m, jnp.max(m8, axis=0, keepdims=True))
                alpha = jnp.exp2(m - m_new)
                p8 = jnp.zeros((8, tb), f32)
                for r in range(0, tb, rc):
                    p = jnp.exp2(blks[r // rc] - m_new)
                    p8 = p8 + jnp.sum(p.reshape(rc // 8, 8, tb), axis=0)
                    p_scr[e, r:r + rc, :] = p.astype(bf16)
                acc = alpha * acc + lax.dot_general(v_ref[pl.ds(off, tb), hs], p_scr[e], (((0,), (0,)), ((), ())),
                                                    preferred_element_type=f32)
                out.append((m_new, alpha * l + jnp.sum(p8, axis=0, keepdims=True), acc))
            return tuple(out)

        one = (jnp.full((1, tb), -1e30, f32), jnp.zeros((1, tb), f32), jnp.zeros((HEAD_PAD, tb), f32))
        carry = lax.fori_loop(0, qi, lambda ki, cr: step(pl.multiple_of(ki * tb, tb), cr, False), (one, one))
        carry = step(pl.multiple_of(qi * tb, tb), carry, True)
        o_t = jnp.zeros((HEAD_PAD, tb), f32)
        for e in range(2):
            m, l, acc = carry[e]
            o_t = o_t + acc / l
            lse_ref[e] = m + jnp.log2(l)
        o_ref[...] = o_t.T

    return _call_with_rider(
        body, 3, 2, ride, (HEADS // 2, nb),
        [pl.BlockSpec((tb, pw), lambda h, i: (i, h)), pl.BlockSpec((T, pw), lambda h, i: (0, h)),
         pl.BlockSpec((T, pw), lambda h, i: (0, h))],
        [pl.BlockSpec((tb, HEAD_PAD), lambda h, i: (i, h)), pl.BlockSpec((2, 1, tb), lambda h, i: (h, 0, i))],
        [jax.ShapeDtypeStruct((T, MLA_W), f32), jax.ShapeDtypeStruct((HEADS, 1, T), f32)],
        (q, k, v), name, scratch=[pltpu.VMEM((2, tb, tb), bf16)])


def _out_layer(x_ref, o_ref, zm_ref, ync_ref, yns_ref, g_ref, w_ref):
    ynm = _mla_out(o_ref[...], zm_ref[...].astype(f32), g_ref[...]).astype(bf16)
    y = x_ref[...]
    y = y + jnp.dot(ync_ref[...], w_ref[0:CONV_W, :], preferred_element_type=f32)
    y = y + jnp.dot(ynm, w_ref[CONV_W:CONV_W + MLA_W, :], preferred_element_type=f32)
    return y + jnp.dot(yns_ref[...], w_ref[CONV_W + MLA_W:, :], preferred_element_type=f32)


def _out_fwd(x, o, proj, ync, yns, g_m, w_out, tm, name):
    T = x.shape[0]

    def body(*refs):
        refs[-1][...] = _out_layer(*refs[:-1])

    row = lambda wd: pl.BlockSpec((tm, wd), lambda i: (i, 0))
    return pl.pallas_call(
        body, name=name, grid=(T // tm,),
        in_specs=[row(D_MODEL), row(MLA_W), row(MLA_W), row(CONV_W), row(SG_W), _const_spec((1, MLA_W)),
                  _const_spec((D_MODEL, D_MODEL))],
        out_specs=row(D_MODEL),
        out_shape=jax.ShapeDtypeStruct((T, D_MODEL), f32),
    )(x, o, proj, ync, yns, g_m, w_out)


def _out_fwd_loss(x, o, proj, ync, yns, g_m, w_out, tgt, tm, name):
    T = x.shape[0]
    nt = T // tm

    def body(x_ref, o_ref, zm_ref, ync_ref, yns_ref, g_ref, w_ref, t_ref, dy_ref, loss_ref, acc_ref):
        i = pl.program_id(0)

        @pl.when(i == 0)
        def _():
            acc_ref[...] = jnp.zeros_like(acc_ref)

        err = _out_layer(x_ref, o_ref, zm_ref, ync_ref, yns_ref, g_ref, w_ref) - t_ref[...]
        dy_ref[...] = err * (1.0 / D_MODEL)
        acc_ref[...] += jnp.sum(err * err, axis=0, keepdims=True)

        @pl.when(i == nt - 1)
        def _():
            loss_ref[...] = jnp.full((1, HEAD_PAD), 0.5 / D_MODEL, f32) * jnp.sum(acc_ref[...])

    row = lambda wd: pl.BlockSpec((tm, wd), lambda i: (i, 0))
    return pl.pallas_call(
        body, name=name, grid=(nt,),
        in_specs=[row(D_MODEL), row(MLA_W), row(MLA_W), row(CONV_W), row(SG_W), _const_spec((1, MLA_W)),
                  _const_spec((D_MODEL, D_MODEL)), row(D_MODEL)],
        out_specs=[row(D_MODEL), _const_spec((1, HEAD_PAD))],
        out_shape=[jax.ShapeDtypeStruct((T, D_MODEL), f32), jax.ShapeDtypeStruct((1, HEAD_PAD), f32)],
        scratch_shapes=[pltpu.VMEM((1, D_MODEL), f32)],
    )(x, o, proj, ync, yns, g_m, w_out, tgt)


def _out_bwd(dout, o, proj, ync, yns, g_m, w_out, tm, name):
    T = dout.shape[0]

    def body(dy_ref, o_ref, zm_ref, ync_ref, yns_ref, g_ref, w_ref,
             do_ref, dl_ref, dzm_ref, dync_ref, dyns_ref, dw_ref, dg_ref):
        i = pl.program_id(0)

        @pl.when(i == 0)
        def _():
            dw_ref[...] = jnp.zeros_like(dw_ref)
            dg_ref[...] = jnp.zeros_like(dg_ref)

        dyb = dy_ref[...].astype(bf16)
        nt = (((1,), (1,)), ((), ()))
        tn = (((0,), (0,)), ((), ()))
        d_c = lax.dot_general(dyb, w_ref[0:CONV_W, :], nt, preferred_element_type=f32)
        d_m = lax.dot_general(dyb, w_ref[CONV_W:CONV_W + MLA_W, :], nt, preferred_element_type=f32)
        d_s = lax.dot_general(dyb, w_ref[CONV_W + MLA_W:, :], nt, preferred_element_type=f32)
        o = o_ref[...]
        ynm, vjp = jax.vjp(_mla_out, o, zm_ref[...].astype(f32), g_ref[...])
        do, dzm, dg = vjp(d_m)
        do_ref[...] = do.astype(bf16)
        dzm_ref[...] = dzm.astype(bf16)
        dync_ref[...] = d_c.astype(bf16)
        dyns_ref[...] = d_s.astype(bf16)
        dg_ref[...] += dg
        dw_ref[0:CONV_W, :] += lax.dot_general(ync_ref[...], dyb, tn, preferred_element_type=f32)
        dw_ref[CONV_W:CONV_W + MLA_W, :] += lax.dot_general(ynm.astype(bf16), dyb, tn, preferred_element_type=f32)
        dw_ref[CONV_W + MLA_W:, :] += lax.dot_general(yns_ref[...], dyb, tn, preferred_element_type=f32)
        prod = do * o
        hi = prod.astype(bf16)
        lo = (prod - hi.astype(f32)).astype(bf16)
        sel = (lax.broadcasted_iota(jnp.int32, (HEADS, MLA_W), 1) // VDIM
               == lax.broadcasted_iota(jnp.int32, (HEADS, MLA_W), 0)).astype(bf16)
        dl_ref[...] = (lax.dot_general(sel, hi, nt, preferred_element_type=f32)
                       + lax.dot_general(sel, lo, nt, preferred_element_type=f32))

    row = lambda wd: pl.BlockSpec((tm, wd), lambda i: (i, 0))
    return pl.pallas_call(
        body, name=name, grid=(T // tm,),
        in_specs=[row(D_MODEL), row(MLA_W), row(MLA_W), row(CONV_W), row(SG_W), _const_spec((1, MLA_W)),
                  _const_spec((D_MODEL, D_MODEL))],
        out_specs=[row(MLA_W), pl.BlockSpec((HEADS, tm), lambda i: (0, i)), row(MLA_W), row(CONV_W), row(SG_W),
                   _const_spec((D_MODEL, D_MODEL)), _const_spec((1, MLA_W))],
        out_shape=[jax.ShapeDtypeStruct((T, MLA_W), bf16), jax.ShapeDtypeStruct((HEADS, T), f32),
                   jax.ShapeDtypeStruct((T, MLA_W), bf16), jax.ShapeDtypeStruct((T, CONV_W), bf16),
                   jax.ShapeDtypeStruct((T, SG_W), bf16), jax.ShapeDtypeStruct((D_MODEL, D_MODEL), f32),
                   jax.ShapeDtypeStruct((1, MLA_W), f32)],
    )(dout, o, proj, ync, yns, g_m, w_out)


def _attn_bwd(q, k, v, do, lse, delta, tb, name, ride=None):
    T = q.shape[0]
    nb = T // tb
    pw = 2 * HEAD_PAD
    nt = (((1,), (1,)), ((), ()))
    tn = (((0,), (0,)), ((), ()))

    rc = min(64, tb)

    def body(q_ref, do_ref, lse_ref, dl_ref, k_ref, v_ref, dqt_ref, dk_ref, dv_ref, p_scr, ds_scr):
        kj = pl.program_id(1)

        @pl.when(kj == 0)
        def _():
            dqt_ref[...] = jnp.zeros_like(dqt_ref)

        krow = lax.broadcasted_iota(jnp.int32, (rc, tb), 0)
        qcol = lax.broadcasted_iota(jnp.int32, (rc, tb), 1)
        heads = [slice(e * HEAD_PAD, (e + 1) * HEAD_PAD) for e in range(2)]
        kb = [k_ref[:, hs] for hs in heads]
        vb = [v_ref[:, hs] for hs in heads]
        kbt = [k.T for k in kb]

        def step(off, carry, masked):
            dob = do_ref[pl.ds(off, tb), :]
            dobt = dob.T
            qb = [q_ref[pl.ds(off, tb), hs] for hs in heads]
            s_all = [lax.dot_general(kb[e], qb[e], nt, preferred_element_type=f32) for e in range(2)]
            dp_all = [lax.dot_general(vb[e], dob, nt, preferred_element_type=f32) for e in range(2)]
            out = []
            for e, hs in enumerate(heads):
                dkt, dvt = carry[e]
                lse = lse_ref[e, :, pl.ds(off, tb)]
                dl = dl_ref[e, :, pl.ds(off, tb)]
                for r in range(0, tb, rc):
                    p = jnp.exp2(s_all[e][r:r + rc, :] - lse)
                    if masked:
                        p = jnp.where(krow + r <= qcol, p, 0.0)
                    p_scr[e, r:r + rc, :] = p.astype(bf16)
                    ds_scr[e, r:r + rc, :] = (p * (dp_all[e][r:r + rc, :] - dl)).astype(bf16)
                dvt = dvt + lax.dot_general(dobt, p_scr[e], nt, preferred_element_type=f32)
                dkt = dkt + lax.dot_general(qb[e].T, ds_scr[e], nt, preferred_element_type=f32)
                dqt_ref[hs, pl.ds(off, tb)] += jnp.dot(kbt[e], ds_scr[e], preferred_element_type=f32)
                out.append((dkt, dvt))
            return tuple(out)

        zero = jnp.zeros((HEAD_PAD, tb), f32)
        carry = step(pl.multiple_of(kj * tb, tb), ((zero, zero), (zero, zero)), True)
        carry = lax.fori_loop(kj + 1, nb, lambda qi, cr: step(pl.multiple_of(qi * tb, tb), cr, False), carry)
        for e, hs in enumerate(heads):
            dk_ref[:, hs] = carry[e][0].T
            dv_ref[:, hs] = carry[e][1].T

    wide = HEADS * HEAD_PAD
    return _call_with_rider(
        body, 6, 3, ride, (HEADS // 2, nb),
        [pl.BlockSpec((T, pw), lambda h, j: (0, h)), pl.BlockSpec((T, HEAD_PAD), lambda h, j: (0, h)),
         pl.BlockSpec((2, 1, T), lambda h, j: (h, 0, 0)), pl.BlockSpec((2, 1, T), lambda h, j: (h, 0, 0)),
         pl.BlockSpec((tb, pw), lambda h, j: (j, h)), pl.BlockSpec((tb, pw), lambda h, j: (j, h))],
        [pl.BlockSpec((pw, T), lambda h, j: (h, 0)), pl.BlockSpec((tb, pw), lambda h, j: (j, h)),
         pl.BlockSpec((tb, pw), lambda h, j: (j, h))],
        [jax.ShapeDtypeStruct((wide, T), f32)] + [jax.ShapeDtypeStruct((T, wide), f32)] * 2,
        (q, do, lse, delta, k, v), name, scratch=[pltpu.VMEM((2, tb, tb), bf16), pltpu.VMEM((2, tb, tb), bf16)])


def _branch_bwd(proj, tabs, bw, dq, dk, dv, dync, dyns, tm, name):
    T = proj.shape[0]
    nt = T // tm
    hb = tm // HALO
    nw = len(BR_W)

    def body(proj_ref, halo_ref, rc_ref, rs1_ref, rs2_ref, *rest):
        wrefs = rest[:nw]
        dq_ref, dk_ref, dv_ref, dync_ref, dyns_ref = rest[nw:nw + 5]
        dp_ref = rest[nw + 5]
        gwrefs = rest[nw + 6:2 * nw + 6]
        carry_ref = rest[2 * nw + 6]
        i = pl.program_id(0)
        r = nt - 1 - i

        @pl.when(i == 0)
        def _():
            carry_ref[...] = jnp.zeros_like(carry_ref)
            for g in gwrefs:
                g[...] = jnp.zeros_like(g)

        c = {"hmask": (r > 0).astype(f32), "rc": rc_ref[...], "rs1": rs1_ref[...], "rs2": rs2_ref[...]}
        _, vjp = jax.vjp(lambda p_, w_: _branch(p_, w_, c), _load_branch_inputs(proj_ref, halo_ref),
                         _load_branch_weights(wrefs))
        cts = (dq_ref[...].T * LN2, dk_ref[...] * LN2, dv_ref[...], dync_ref[...].astype(f32),
               dyns_ref[...].astype(f32))
        dp, dw = vjp(cts)
        for n in SEG:
            if n in ("zm", "a", "glu"):
                continue
            dp_ref[:, SEG[n][0] - 512:SEG[n][1] - 512] = dp[n].astype(bf16)
        for n, hn, lo in (("a", "ha", 0), ("glu", "hglu", CONV_W)):
            d = dp[n]
            tail = d[tm - HALO:, :] + carry_ref[:, lo:lo + CONV_W]
            s0 = SEG[n][0] - 512
            dp_ref[0:tm - HALO, s0:s0 + CONV_W] = d[:tm - HALO, :].astype(bf16)
            dp_ref[tm - HALO:tm, s0:s0 + CONV_W] = tail.astype(bf16)
        carry_ref[:, 0:CONV_W] = dp["ha"]
        carry_ref[:, CONV_W:] = dp["hglu"]
        for (n, _), g in zip(BR_W, gwrefs):
            if n in ("sg_w", "sg_b"):
                for gi in range(SG_HEADS):
                    g[gi] += dw[n][gi]
            else:
                g[...] += dw[n]

    row = lambda wd: pl.BlockSpec((tm, wd), lambda i: (nt - 1 - i, 0))
    wide = HEADS * HEAD_PAD
    return pl.pallas_call(
        body, name=name, grid=(nt,),
        in_specs=[row(PROJ_PAD),
                  pl.BlockSpec((HALO, 2 * CONV_W), lambda i: (jnp.maximum((nt - 1 - i) * hb - 1, 0), 1)),
                  row(HEAD_PAD), row(HEAD_PAD), row(HEAD_PAD)] + [_const_spec(s) for _, s in BR_W]
                 + [pl.BlockSpec((wide, tm), lambda i: (0, nt - 1 - i)), row(wide), row(wide), row(CONV_W),
                    row(SG_W)],
        out_specs=[row(REST)] + [_const_spec(s) for _, s in BR_W],
        out_shape=[jax.ShapeDtypeStruct((T, REST), bf16)] + [jax.ShapeDtypeStruct(s, f32) for _, s in BR_W],
        scratch_shapes=[pltpu.VMEM((HALO, 2 * CONV_W), f32)],
    )(proj, proj, *tabs, *[bw[n] for n, _ in BR_W], dq, dk, dv, dync, dyns)


def _inproj_bwd_dx(x, g, w, dzm, dprest, dout, tm, name, ride=None):
    T = x.shape[0]
    nt_dims = (((1,), (1,)), ((), ()))

    def body(x_ref, g_ref, w_ref, dzm_ref, dpr_ref, dout_ref, dx_ref, dg_ref, h_ref):
        i = pl.program_id(0)

        @pl.when(i == 0)
        def _():
            dg_ref[...] = jnp.zeros_like(dg_ref)

        dh = lax.dot_general(dzm_ref[...], w_ref[:, 0:512], nt_dims, preferred_element_type=f32)
        dh = dh + lax.dot_general(dpr_ref[...], w_ref[:, 512:], nt_dims, preferred_element_type=f32)
        h, vjp = jax.vjp(_rms, x_ref[...], g_ref[...])
        dx, dg = vjp(dh)
        dx_ref[...] = dout_ref[...] + dx
        dg_ref[...] += dg
        h_ref[...] = h.astype(bf16)

    row = lambda wd: pl.BlockSpec((tm, wd), lambda i: (i, 0))
    return _call_with_rider(
        body, 6, 3, ride, (T // tm,),
        [row(D_MODEL), _const_spec((1, D_MODEL)), _const_spec((D_MODEL, PROJ_PAD)), row(512), row(REST),
         row(D_MODEL)],
        [row(D_MODEL), _const_spec((1, D_MODEL)), row(D_MODEL)],
        [jax.ShapeDtypeStruct((T, D_MODEL), f32), jax.ShapeDtypeStruct((1, D_MODEL), f32),
         jax.ShapeDtypeStruct((T, D_MODEL), bf16)],
        (x, g, w, dzm, dprest, dout), name)


def _inproj_bwd_dw(h, dzm, dprest, tm, name, ride=None):
    T = h.shape[0]
    tm = min(4 * tm, T)
    nt = T // tm
    cw = REST // 3
    tn = (((0,), (0,)), ((), ()))

    def make_body():
        def body(h_ref, d_ref, o_ref):
            i = pl.program_id(1)

            @pl.when(i == 0)
            def _():
                o_ref[...] = jnp.zeros_like(o_ref)

            o_ref[...] += lax.dot_general(h_ref[...], d_ref[...], tn, preferred_element_type=f32)
        return body

    def run(d, width, cwid, nm, rd):
        return _call_with_rider(
            make_body(), 2, 1, rd, (width // cwid, nt),
            [pl.BlockSpec((tm, D_MODEL), lambda j, i: (i, 0)), pl.BlockSpec((tm, cwid), lambda j, i: (i, j))],
            [pl.BlockSpec((D_MODEL, cwid), lambda j, i: (0, j))], [jax.ShapeDtypeStruct((D_MODEL, width), f32)],
            (h, d), nm)

    (zm,) = run(dzm, 512, 512, name + "_zm", None)
    rest, *sent = run(dprest, REST, cw, name + "_rest", ride)
    return (zm, rest, *sent)


def _row_tile(rows):
    return ROW_ALIGN if rows > ROW_ALIGN and rows % ROW_ALIGN == 0 else rows


def _sum4(buf, name):
    _, R, C = buf.shape
    tr = _row_tile(R)

    def body(b_ref, o_ref):
        b = [b_ref[s].astype(f32) for s in range(4)]
        o_ref[...] = ((b[0] + b[1]) + b[2]) + b[3]

    return pl.pallas_call(
        body, name=name, grid=(R // tr,),
        in_specs=[pl.BlockSpec((4, tr, C), lambda i: (0, i, 0))],
        out_specs=pl.BlockSpec((tr, C), lambda i: (i, 0)),
        out_shape=jax.ShapeDtypeStruct((R, C), f32),
    )(buf)


def _sum4_whole(bufs, name):
    n = len(bufs)

    def body(*refs):
        for b_ref, o_ref in zip(refs[:n], refs[n:]):
            o_ref[...] = ((b_ref[0] + b_ref[1]) + b_ref[2]) + b_ref[3]

    return pl.pallas_call(
        body, name=name, out_shape=[jax.ShapeDtypeStruct(b.shape[1:], f32) for b in bufs])(*bufs)


def _adamw_math(w, m, v, g):
    m_new = ADAM_B1 * m + (1.0 - ADAM_B1) * g
    v_new = ADAM_B2 * v + (1.0 - ADAM_B2) * (g * g)
    m_hat = m_new / (1.0 - ADAM_B1 ** ADAM_STEP)
    v_hat = v_new / (1.0 - ADAM_B2 ** ADAM_STEP)
    return -ADAM_LR * (m_hat / (jnp.sqrt(v_hat) + ADAM_EPS) + ADAM_WD * w), m_new, v_new


def _adamw(w, m, v, p_a, p_b, name):
    R, C = w.shape
    tr = _row_tile(R)

    def body(w_ref, m_ref, v_ref, a_ref, b_ref, g_ref, d_ref, nm_ref, nv_ref):
        g = a_ref[...] + b_ref[...]
        g_ref[...] = g
        d_ref[...], nm_ref[...], nv_ref[...] = _adamw_math(w_ref[...], m_ref[...], v_ref[...], g)

    spec = pl.BlockSpec((tr, C), lambda i: (i, 0))
    return pl.pallas_call(
        body, name=name, grid=(R // tr,), in_specs=[spec] * 5, out_specs=[spec] * 4,
        out_shape=[jax.ShapeDtypeStruct((R, C), f32)] * 4,
    )(w, m, v, p_a, p_b)


def _adamw_whole(ws, ms, vs, p_a, p_b, name):
    n = len(ws)

    def body(*refs):
        ins, outs = refs[:5 * n], refs[5 * n:]
        for i in range(n):
            w, m, v, a, b = (ins[k * n + i][...] for k in range(5))
            g = a + b
            outs[i][...] = g
            outs[n + i][...], outs[2 * n + i][...], outs[3 * n + i][...] = _adamw_math(w, m, v, g)

    res = pl.pallas_call(
        body, name=name, out_shape=[jax.ShapeDtypeStruct(w.shape, f32) for w in ws] * 4,
    )(*ws, *ms, *vs, *p_a, *p_b)
    return [res[k * n:(k + 1) * n] for k in range(4)]


class _ChipExchange:
    def __init__(self, srcs, per_target):
        self.n = len(srcs)
        self.per_target = per_target
        blocks = [s.shape[1:] if p else s.shape for s, p in zip(srcs, per_target)]
        self.out_shape = [jax.ShapeDtypeStruct((4,) + tuple(b), s.dtype) for b, s in zip(blocks, srcs)]
        self.scratch_shapes = [pltpu.SemaphoreType.DMA((3 * self.n,)), pltpu.SemaphoreType.DMA((3 * self.n,)),
                               pltpu.SemaphoreType.DMA((self.n,))]
        self.any_specs = [pl.BlockSpec(memory_space=pl.ANY)] * self.n

    def _copies(self, src_refs, out_refs, sems):
        send_sems, recv_sems, local_sems = sems
        x, y, c = lax.axis_index("x"), lax.axis_index("y"), lax.axis_index("c")
        me = 2 * x + y
        chips = [(1 - x, y), (x, 1 - y), (1 - x, 1 - y)]

        def block_for(i, t):
            return src_refs[i].at[t] if self.per_target[i] else src_refs[i]

        def remote(i, k, block_t, slot):
            tx, ty = chips[k]
            return pltpu.make_async_remote_copy(
                src_ref=block_for(i, block_t), dst_ref=out_refs[i].at[slot], send_sem=send_sems.at[3 * i + k],
                recv_sem=recv_sems.at[3 * i + k], device_id=(tx, ty, c), device_id_type=pl.DeviceIdType.MESH)

        pairs = [(i, k) for i in range(self.n) for k in range(3)]
        sends = [remote(i, k, 2 * chips[k][0] + chips[k][1], me) for i, k in pairs]
        recvs = [remote(i, k, me, 2 * chips[k][0] + chips[k][1]) for i, k in pairs]
        mine = [pltpu.make_async_copy(block_for(i, me), out_refs[i].at[me], local_sems.at[i])
                for i in range(self.n)]
        return sends, recvs, mine

    def start(self, src_refs, out_refs, sems):
        sends, _, mine = self._copies(src_refs, out_refs, sems)
        for cp in sends + mine:
            cp.start()

    def wait(self, src_refs, out_refs, sems):
        sends, recvs, mine = self._copies(src_refs, out_refs, sems)
        for cp in recvs:
            cp.wait_recv()
        for cp in sends:
            cp.wait_send()
        for cp in mine:
            cp.wait()


def _call_with_rider(body, n_in, n_out, ride, grid, in_specs, out_specs, out_shape, args, name, scratch=()):
    scratch = list(scratch)
    if ride is None:
        return pl.pallas_call(body, name=name, grid=grid, in_specs=in_specs, out_specs=out_specs,
                              out_shape=out_shape, scratch_shapes=scratch)(*args)
    ex = _ChipExchange(*ride)
    ne = ex.n

    def wrapped(*refs):
        ins, src_refs = refs[:n_in], refs[n_in:n_in + ne]
        outs, ex_outs = refs[n_in + ne:n_in + ne + n_out], refs[n_in + ne + n_out:n_in + 2 * ne + n_out]
        own = refs[n_in + 2 * ne + n_out:n_in + 2 * ne + n_out + len(scratch)]
        sems = refs[n_in + 2 * ne + n_out + len(scratch):]
        ids = [pl.program_id(a) for a in range(len(grid))]
        first = functools.reduce(jnp.logical_and, [i == 0 for i in ids])
        last = functools.reduce(jnp.logical_and, [i == g - 1 for i, g in zip(ids, grid)])

        @pl.when(first)
        def _():
            ex.start(src_refs, ex_outs, sems)

        body(*ins, *outs, *own)

        @pl.when(last)
        def _():
            ex.wait(src_refs, ex_outs, sems)

    return pl.pallas_call(
        wrapped, name=name, grid=grid, in_specs=list(in_specs) + ex.any_specs,
        out_specs=list(out_specs) + ex.any_specs, out_shape=list(out_shape) + ex.out_shape,
        scratch_shapes=scratch + ex.scratch_shapes,
    )(*args, *ride[0])


def _chip_exchange(srcs, per_target, name):
    ex = _ChipExchange(srcs, per_target)
    n = ex.n

    def body(*refs):
        src_refs, out_refs, sems = refs[:n], refs[n:2 * n], refs[2 * n:]
        ex.start(src_refs, out_refs, sems)
        ex.wait(src_refs, out_refs, sems)

    return pl.pallas_call(
        body, name=name, in_specs=ex.any_specs, out_specs=ex.any_specs, out_shape=ex.out_shape,
        scratch_shapes=ex.scratch_shapes,
    )(*srcs)


def _sibling_exchange(srcs, name):
    n = len(srcs)

    def body(*refs):
        src_refs, out_refs = refs[:n], refs[n:2 * n]
        send_sems, recv_sems = refs[2 * n:]
        x, y, c = lax.axis_index("x"), lax.axis_index("y"), lax.axis_index("c")
        copies = [pltpu.make_async_remote_copy(
            src_ref=src_refs[i], dst_ref=out_refs[i], send_sem=send_sems.at[i], recv_sem=recv_sems.at[i],
            device_id=(x, y, 1 - c), device_id_type=pl.DeviceIdType.MESH) for i in range(n)]
        for cp in copies:
            cp.start()
        for cp in copies:
            cp.wait()

    return pl.pallas_call(
        body, name=name,
        in_specs=[pl.BlockSpec(memory_space=pl.ANY)] * n, out_specs=[pl.BlockSpec(memory_space=pl.ANY)] * n,
        out_shape=[jax.ShapeDtypeStruct(s.shape, s.dtype) for s in srcs],
        scratch_shapes=[pltpu.SemaphoreType.DMA((n,)), pltpu.SemaphoreType.DMA((n,))],
    )(*srcs)


W_IN_SHARD = IN_COLS // 4
W_IN_MAP = [(0, 1792, 512), (1792, 1824, 2368), (1824, 2336, 0), (2336, IN_COLS, 2432)]


def _pad_w_in(shards):
    def cols(a, b):
        out = []
        while a < b:
            j = a // W_IN_SHARD
            e = min(b, (j + 1) * W_IN_SHARD)
            out.append(shards[j][:, a - j * W_IN_SHARD:e - j * W_IN_SHARD])
            a = e
        return out

    z = lambda n: [jnp.zeros((D_MODEL, n), shards.dtype)]
    return jnp.concatenate(cols(1824, 2336) + cols(0, 1792) + z(64) + cols(1792, 1824) + z(32) + cols(2336, IN_COLS),
                           axis=1)


def _w_in_grad_shard(dw_zm, dw_rest, j):
    lo, hi = j * W_IN_SHARD, (j + 1) * W_IN_SHARD
    out = []
    for a, b, p in W_IN_MAP:
        a2, b2 = max(a, lo), min(b, hi)
        if a2 < b2:
            p0 = p + a2 - a
            src, off = (dw_zm, 0) if p0 < 512 else (dw_rest, 512)
            out.append(src[:, p0 - off:p0 - off + b2 - a2])
    return jnp.concatenate(out, axis=1)


def _v_cols():
    return [(h % 2) * VDIM for h in range(HEADS)]


def _layer_weights(gathered, small, l):
    full = {n: jnp.concatenate([g[j] for j in range(4)], axis=BIG[n] - 1) for n, g in gathered.items()}
    w_uq = full["w_uq"].astype(f32).reshape(Q_LORA, HEADS, QK)
    w_uq = jnp.pad(w_uq, ((0, 0), (0, 0), (0, HEAD_PAD - QK))).reshape(Q_LORA, HEADS * HEAD_PAD)
    ukv = full["w_ukv"].astype(f32).reshape(KV_LORA, HEADS, NOPE + VDIM)
    w_k = jnp.pad(ukv[:, :, :NOPE], ((0, 0), (0, 0), (0, HEAD_PAD - NOPE))).reshape(KV_LORA, HEADS * HEAD_PAD)
    zeros = jnp.zeros((KV_LORA, VDIM), f32)
    w_v = jnp.concatenate(
        [jnp.concatenate([ukv[:, h, NOPE:], zeros] if h % 2 == 0 else [zeros, ukv[:, h, NOPE:]], axis=1)
         for h in range(HEADS)], axis=1)
    row = lambda a: a.reshape(1, -1)
    bn = small["branch_norm_g"][l]
    bw = {
        "conv_w": jnp.pad(full["conv_w"], ((0, 1), (0, 0))), "conv_b": row(small["conv_b"][l]),
        "cln_g": row(small["conv_ln_g"][l]), "cln_b": row(small["conv_ln_b"][l]),
        "pw_w": full["conv_pw_w"].astype(f32), "pw_b": row(small["conv_pw_b"][l]),
        "qn_g": row(small["q_norm_g"][l]), "w_uq": w_uq, "kvn_g": row(small["kv_norm_g"][l]),
        "w_k": w_k, "w_v": w_v,
        "qkq_g": jnp.pad(row(small["qk_q_g"][l]), ((0, 0), (0, HEAD_PAD - QK))),
        "qkk_g": jnp.pad(row(small["qk_k_g"][l]), ((0, 0), (0, HEAD_PAD - QK))),
        "sln_g": row(small["sg_ln_g"][l]), "sln_b": row(small["sg_ln_b"][l]),
        "sg_w": small["sg_w"][l], "sg_b": small["sg_b"][l].reshape(SG_HEADS, SG_CHUNK, 1),
        "bn_c": row(bn[:CONV_W]), "bn_s": row(bn[CONV_W + MLA_W:]),
    }
    return {"bw": bw, "bn_m": row(bn[CONV_W:CONV_W + MLA_W]), "w_out": full["w_out"]}


def _layer_grads(gb, dw_out, dbn_m):
    duq = gb["w_uq"].reshape(Q_LORA, HEADS, HEAD_PAD)[:, :, :QK].reshape(Q_LORA, HEADS * QK)
    dk = gb["w_k"].reshape(KV_LORA, HEADS, HEAD_PAD)[:, :, :NOPE]
    dv = gb["w_v"].reshape(KV_LORA, HEADS, HEAD_PAD)
    dv = jnp.stack([dv[:, h, c0:c0 + VDIM] for h, c0 in enumerate(_v_cols())], axis=1)
    dukv = jnp.concatenate([dk, dv], axis=2).reshape(KV_LORA, HEADS * (NOPE + VDIM))
    return {
        "conv_w": gb["conv_w"][:CONV_K],
        "conv_b": gb["conv_b"][0], "conv_ln_g": gb["cln_g"][0], "conv_ln_b": gb["cln_b"][0],
        "conv_pw_w": gb["pw_w"], "conv_pw_b": gb["pw_b"][0], "q_norm_g": gb["qn_g"][0], "w_uq": duq,
        "kv_norm_g": gb["kvn_g"][0], "w_ukv": dukv, "qk_q_g": gb["qkq_g"][0, :QK], "qk_k_g": gb["qkk_g"][0, :QK],
        "sg_ln_g": gb["sln_g"][0], "sg_ln_b": gb["sln_b"][0], "sg_w": gb["sg_w"], "sg_b": gb["sg_b"][:, :, 0],
        "branch_norm_g": jnp.concatenate([gb["bn_c"][0], dbn_m[0], gb["bn_s"][0]]), "w_out": dw_out,
    }


def _rope_tables(T):
    half = ROPE // 2
    inv_freq = ROPE_THETA ** (-jnp.arange(half, dtype=f32) / half)
    ang = jnp.arange(T, dtype=f32)[:, None] * inv_freq[None, :]
    cos, sin = jnp.cos(ang), jnp.sin(ang)
    one = jnp.ones((T, NOPE), f32)
    z = lambda n: jnp.zeros((T, n), f32)
    rc = jnp.concatenate([one, cos, cos, jnp.ones((T, HEAD_PAD - QK), f32)], axis=1)
    rs1 = jnp.concatenate([z(NOPE), -sin, z(half), z(HEAD_PAD - QK)], axis=1)
    rs2 = jnp.concatenate([z(NOPE), z(half), sin, z(HEAD_PAD - QK)], axis=1)
    return rc, rs1, rs2


def kernel(x, norm_g, w_in, conv_w, conv_b, conv_ln_g, conv_ln_b, conv_pw_w, conv_pw_b, q_norm_g, w_uq, kv_norm_g, w_ukv, qk_q_g, qk_k_g, sg_ln_g, sg_ln_b, sg_w, sg_b, branch_norm_g, w_out, loss_target, m_norm_g, m_w_in, m_conv_w, m_conv_b, m_conv_ln_g, m_conv_ln_b, m_conv_pw_w, m_conv_pw_b, m_q_norm_g, m_w_uq, m_kv_norm_g, m_w_ukv, m_qk_q_g, m_qk_k_g, m_sg_ln_g, m_sg_ln_b, m_sg_w, m_sg_b, m_branch_norm_g, m_w_out, v_norm_g, v_w_in, v_conv_w, v_conv_b, v_conv_ln_g, v_conv_ln_b, v_conv_pw_w, v_conv_pw_b, v_q_norm_g, v_w_uq, v_kv_norm_g, v_w_ukv, v_qk_q_g, v_qk_k_g, v_sg_ln_g, v_sg_ln_b, v_sg_w, v_sg_b, v_branch_norm_g, v_w_out):
    wts = dict(zip(WEIGHTS, [norm_g, w_in, conv_w, conv_b, conv_ln_g, conv_ln_b, conv_pw_w, conv_pw_b, q_norm_g,
                             w_uq, kv_norm_g, w_ukv, qk_q_g, qk_k_g, sg_ln_g, sg_ln_b, sg_w, sg_b, branch_norm_g,
                             w_out]))
    mom_m = dict(zip(WEIGHTS, [m_norm_g, m_w_in, m_conv_w, m_conv_b, m_conv_ln_g, m_conv_ln_b, m_conv_pw_w,
                               m_conv_pw_b, m_q_norm_g, m_w_uq, m_kv_norm_g, m_w_ukv, m_qk_q_g, m_qk_k_g,
                               m_sg_ln_g, m_sg_ln_b, m_sg_w, m_sg_b, m_branch_norm_g, m_w_out]))
    mom_v = dict(zip(WEIGHTS, [v_norm_g, v_w_in, v_conv_w, v_conv_b, v_conv_ln_g, v_conv_ln_b, v_conv_pw_w,
                               v_conv_pw_b, v_q_norm_g, v_w_uq, v_kv_norm_g, v_w_ukv, v_qk_q_g, v_qk_k_g,
                               v_sg_ln_g, v_sg_ln_b, v_sg_w, v_sg_b, v_branch_norm_g, v_w_out]))
    xs = x[0]
    tgt = loss_target[0]
    T = xs.shape[0]
    tm = min(256, T)
    tmm = min(512, T)
    tb = min(512, T // 2)

    big = list(BIG)
    rest = [n for n in big if n != "w_in"]
    nb = len(big)
    shards = [{n: wts[n][l] if n == "conv_w" else wts[n][l].astype(bf16) for n in big} for l in range(DEPTH)]
    small = {n: wts[n] for n in SMALL}
    tabs = _rope_tables(T)
    gathered = dict(zip(["w_in"], _chip_exchange([shards[0]["w_in"]], [False], "gather_w_in_0")))

    acts, lw = [], []
    h_in = xs
    for l in range(DEPTH):
        W = {"norm_g": small["norm_g"][l].reshape(1, -1), "w_in": _pad_w_in(gathered["w_in"])}
        ride = ([shards[l][n] for n in rest], [False] * len(rest)) if l == 0 else None
        proj, *late = _inproj_fwd(h_in, W["norm_g"], W["w_in"], tmm, f"inproj_fwd_{l}", ride)
        if ride is not None:
            gathered.update(zip(rest, late))
        W.update(_layer_weights({n: gathered[n] for n in rest}, small, l))
        lw.append(W)
        q, k, v, ync, yns = _branch_fwd(proj, tabs, W["bw"], tm, f"branch_fwd_{l}")
        ride = ([shards[l + 1][n] for n in big], [False] * nb) if l + 1 < DEPTH else None
        o, lse, *ahead = _attn_fwd(q, k, v, tb, f"attn_fwd_{l}", ride)
        gathered = dict(zip(big, ahead))
        acts.append((h_in, proj, q, k, v, ync, yns, o, lse))
        if l + 1 < DEPTH:
            h_in = _out_fwd(h_in, o, proj, ync, yns, W["bn_m"], W["w_out"], tmm, f"out_fwd_{l}")
        else:
            dy, loss_row = _out_fwd_loss(h_in, o, proj, ync, yns, W["bn_m"], W["w_out"], tgt, tmm, f"out_fwd_{l}")
    loss = lax.psum(loss_row[0, 0], ("x", "y", "c"))

    def for_chips(n, g):
        if n == "w_in":
            blocks = [_w_in_grad_shard(*g, j) for j in range(4)]
        else:
            blocks = jnp.split(g, 4, axis=BIG[n] - 1)
        return jnp.stack(blocks).astype(bf16)

    mid = [n for n in rest if n != "w_out"]
    grads = [None] * DEPTH
    contrib = {}
    for l in reversed(range(DEPTH)):
        W = lw[l]
        xin, proj, q, k, v, ync, yns, o, lse = acts[l]
        do, delta, dzm, dync, dyns, dw_out, dbn_m = _out_bwd(
            dy, o, proj, ync, yns, W["bn_m"], W["w_out"], tmm, f"out_bwd_{l}")
        keys = [(l, "w_out")] + ([(l + 1, "w_in")] if l + 1 < DEPTH else [])
        srcs = [for_chips("w_out", dw_out)] + ([for_chips("w_in", grads[l + 1]["w_in"])] if l + 1 < DEPTH else [])
        dq, dk, dv, *sent = _attn_bwd(q, k, v, do, lse, delta.reshape(HEADS, 1, T), tb, f"attn_bwd_{l}",
                                      (srcs, [True] * len(srcs)))
        contrib.update(zip(keys, sent))
        outs = _branch_bwd(proj, tabs, W["bw"], dq, dk, dv, dync, dyns, tm, f"branch_bwd_{l}")
        dprest, gb = outs[0], dict(zip([n for n, _ in BR_W], outs[1:]))
        grads[l] = _layer_grads(gb, dw_out, dbn_m)
        dx, dnorm_g, hb = _inproj_bwd_dx(xin, W["norm_g"], W["w_in"], dzm, dprest, dy, tmm, f"inproj_bwd_dx_{l}")
        dw_zm, dw_rest, *sent = _inproj_bwd_dw(
            hb, dzm, dprest, tm, f"inproj_bwd_dw_{l}", ([for_chips(n, grads[l][n]) for n in mid], [True] * len(mid)))
        contrib.update(zip([(l, n) for n in mid], sent))
        grads[l]["norm_g"] = dnorm_g[0]
        grads[l]["w_in"] = (dw_zm, dw_rest)
        dy = dx
    grad_x = dy[None]
    g_small = [jnp.stack([grads[l][n] for l in range(DEPTH)]) for n in SMALL]
    last = _chip_exchange([for_chips("w_in", grads[0]["w_in"])] + g_small, [True] + [False] * len(SMALL),
                          "exchange_grads_last")
    contrib[(0, "w_in")] = last[0]
    view = lambda a: a.reshape(-1, a.shape[-1])
    parts = [view(jnp.stack([_sum4(contrib[(l, n)], f"sum_chips_{n}_{l}") for l in range(DEPTH)])) for n in big]
    parts += list(_sum4_whole(last[1:], "sum_chips_small"))
    others = _sibling_exchange(parts, "exchange_cores")

    res = {kind: {} for kind in ("grad", "delta", "new_m", "new_v")}
    for i, n in enumerate(big):
        outs = _adamw(*[view(d[n]) for d in (wts, mom_m, mom_v)], parts[i], others[i], f"adamw_{n}")
        for kind, o in zip(res, outs):
            res[kind][n] = o.reshape(wts[n].shape)
    outs = _adamw_whole(*[[d[n] for n in SMALL] for d in (wts, mom_m, mom_v)], parts[nb:], others[nb:],
                        "adamw_small")
    for kind, o in zip(res, outs):
        res[kind].update(zip(SMALL, o))
    out = [loss, grad_x]
    for kind in ("grad", "delta", "new_m", "new_v"):
        out += [res[kind][n] for n in WEIGHTS]
    return tuple(out)
```

```python
import functools

import jax
import jax.numpy as jnp
from jax import lax
from jax.experimental import pallas as pl
from jax.experimental.pallas import tpu as pltpu

f32 = jnp.float32
bf16 = jnp.bfloat16

DEPTH = 2
D_MODEL = 1024
CONV_W = 256
CONV_K = 31
HEADS = 8
NOPE = 64
ROPE = 32
QK = NOPE + ROPE
VDIM = 64
MLA_W = HEADS * VDIM
Q_LORA = 768
KV_LORA = 256
SG_W = 256
SG_HEADS = 4
SG_CHUNK = 128
IN_COLS = 3104
ROPE_THETA = 10000.0
EPS = 1e-6
HEAD_PAD = 128
LOG2E = 1.4426950408889634
LN2 = 0.6931471805599453
Q_SCALE = QK ** -0.5 * LOG2E
HALO = 32
ROW_ALIGN = 256

ADAM_LR = 0.001
ADAM_B1 = 0.9
ADAM_B2 = 0.999
ADAM_EPS = 1e-08
ADAM_WD = 0.01
ADAM_STEP = 10

PROJ_PAD = 3200
SEG = {
    "zm": (0, 512), "a": (512, 768), "glu": (768, 1024), "zc": (1024, 1280), "cq": (1280, 2048),
    "ckv": (2048, 2304), "kr": (2304, 2432), "us": (2432, 2688), "vs": (2688, 2944), "zs": (2944, 3200),
}
REST = PROJ_PAD - 512

WEIGHTS = ["norm_g", "w_in", "conv_w", "conv_b", "conv_ln_g", "conv_ln_b", "conv_pw_w", "conv_pw_b", "q_norm_g",
           "w_uq", "kv_norm_g", "w_ukv", "qk_q_g", "qk_k_g", "sg_ln_g", "sg_ln_b", "sg_w", "sg_b",
           "branch_norm_g", "w_out"]
BIG = {"w_in": 2, "conv_w": 2, "conv_pw_w": 1, "w_uq": 1, "w_ukv": 2, "w_out": 1}
SMALL = [n for n in WEIGHTS if n not in BIG]

BR_W = [("conv_w", (32, 256)), ("conv_b", (1, 256)), ("cln_g", (1, 256)), ("cln_b", (1, 256)),
        ("pw_w", (256, 256)), ("pw_b", (1, 256)), ("qn_g", (1, 768)), ("w_uq", (768, 1024)),
        ("kvn_g", (1, 256)), ("w_k", (256, 1024)), ("w_v", (256, 1024)), ("qkq_g", (1, 128)),
        ("qkk_g", (1, 128)), ("sln_g", (1, 256)), ("sln_b", (1, 256)), ("sg_w", (4, 128, 128)),
        ("sg_b", (4, 128, 1)), ("bn_c", (1, 256)), ("bn_s", (1, 256))]


@jax.custom_vjp
def _mm(a, w):
    return jnp.dot(a.astype(bf16), w.astype(bf16), preferred_element_type=f32)


def _mm_fwd(a, w):
    return _mm(a, w), (a, w)


def _mm_bwd(res, ct):
    a, w = res
    ctb = ct.astype(bf16)
    da = lax.dot_general(ctb, w.astype(bf16), (((1,), (1,)), ((), ())), preferred_element_type=f32)
    dw = lax.dot_general(a.astype(bf16), ctb, (((0,), (0,)), ((), ())), preferred_element_type=f32)
    return da.astype(a.dtype), dw.astype(w.dtype)


_mm.defvjp(_mm_fwd, _mm_bwd)


@jax.custom_vjp
def _rope(x, c, s1, s2):
    return x * c + pltpu.roll(x, HEAD_PAD - 16, 1) * s1 + pltpu.roll(x, 16, 1) * s2


def _rope_fwd(x, c, s1, s2):
    return _rope(x, c, s1, s2), (c, s1, s2)


def _rope_bwd(res, ct):
    c, s1, s2 = res
    dx = ct * c + pltpu.roll(ct * s1, 16, 1) + pltpu.roll(ct * s2, HEAD_PAD - 16, 1)
    return dx, jnp.zeros_like(c), jnp.zeros_like(s1), jnp.zeros_like(s2)


_rope.defvjp(_rope_fwd, _rope_bwd)


def _rms(x, g):
    return x * lax.rsqrt(jnp.mean(x * x, axis=-1, keepdims=True) + EPS) * g


def _head_rms(x, g):
    return x * lax.rsqrt(jnp.sum(x * x, axis=-1, keepdims=True) * (1.0 / QK) + EPS) * g


def _ln(x, g, b):
    mu = jnp.mean(x, axis=-1, keepdims=True)
    xc = x - mu
    var = jnp.mean(xc * xc, axis=-1, keepdims=True)
    return xc * lax.rsqrt(var + EPS) * g + b


def _branch(p, w, c):
    tm = p["a"].shape[0]
    yg = p["a"] * jax.nn.sigmoid(p["glu"])
    yh = p["ha"] * jax.nn.sigmoid(p["hglu"]) * c["hmask"]
    ycat = jnp.concatenate([yh, yg], axis=0)
    shifted = [ycat] + [ycat[b:b + tm + HALO - 8, :] for b in range(1, 8)]
    acc = jnp.zeros((tm, CONV_W), f32)
    for k in range(CONV_K):
        off = HALO - (CONV_K - 1) + k
        a8 = off - off % 8
        acc = acc + shifted[off % 8][a8:a8 + tm, :] * w["conv_w"][k:k + 1, :]
    yl = jax.nn.silu(_ln(acc + w["conv_b"], w["cln_g"], w["cln_b"]))
    ypw = _mm(yl, w["pw_w"]) + w["pw_b"]
    ync = _rms(ypw * jax.nn.silu(p["zc"]), w["bn_c"])
    cqn = _rms(p["cq"], w["qn_g"])
    qf = _mm(cqn, w["w_uq"])
    ckvn = _rms(p["ckv"], w["kvn_g"])
    kf = _mm(ckvn, w["w_k"])
    v = _mm(ckvn, w["w_v"])
    qs, ks = [], []
    for h in range(HEADS):
        sl = slice(h * HEAD_PAD, (h + 1) * HEAD_PAD)
        qh = _head_rms(qf[:, sl], w["qkq_g"])
        qs.append(_rope(qh, c["rc"], c["rs1"], c["rs2"]) * Q_SCALE)
        kh = _head_rms(kf[:, sl] + p["kr"], w["qkk_g"])
        ks.append(_rope(kh, c["rc"], c["rs1"], c["rs2"]))
    q = jnp.concatenate(qs, axis=-1)
    k = jnp.concatenate(ks, axis=-1)
    u = jax.nn.gelu(p["us"])
    v2 = _ln(jax.nn.gelu(p["vs"]), w["sln_g"], w["sln_b"])
    r = lax.broadcasted_iota(jnp.int32, (SG_CHUNK, SG_CHUNK), 0)
    cc = lax.broadcasted_iota(jnp.int32, (SG_CHUNK, SG_CHUNK), 1)
    lane = lax.broadcasted_iota(jnp.int32, (1, SG_W), 1) // (SG_W // SG_HEADS)
    wm = [jnp.where(r >= cc, w["sg_w"][g], 0.0) for g in range(SG_HEADS)]
    rows = []
    for ci in range(tm // SG_CHUNK):
        vc = v2[ci * SG_CHUNK:(ci + 1) * SG_CHUNK, :]
        mixed = jnp.zeros((SG_CHUNK, SG_W), f32)
        for g in range(SG_HEADS):
            mixed = mixed + jnp.where(lane == g, _mm(wm[g], vc) + w["sg_b"][g], 0.0)
        rows.append(mixed)
    mixed = jnp.concatenate(rows, axis=0) if len(rows) > 1 else rows[0]
    yns = _rms(u * mixed * jax.nn.silu(p["zs"]), w["bn_s"])
    return q, k, v, ync, yns


def _mla_out(o, zm, g):
    return _rms(o * jax.nn.silu(zm), g)


def _load_branch_inputs(proj_ref, halo_ref):
    p = {n: proj_ref[:, SEG[n][0]:SEG[n][1]].astype(f32) for n in SEG if n != "zm"}
    p["ha"] = halo_ref[:, 0:CONV_W].astype(f32)
    p["hglu"] = halo_ref[:, CONV_W:2 * CONV_W].astype(f32)
    return p


def _load_branch_weights(refs):
    w = {}
    for (n, _), r in zip(BR_W, refs):
        if n in ("sg_w", "sg_b"):
            w[n] = [r[g] for g in range(SG_HEADS)]
        else:
            w[n] = r[...]
    return w


def _const_spec(shape):
    nd = len(shape)
    return pl.BlockSpec(shape, lambda *_: (0,) * nd)


def _inproj_fwd(x, g, w, tm, name, ride=None):
    T = x.shape[0]
    nc = 5
    cw = PROJ_PAD // nc

    def body(x_ref, g_ref, w_ref, o_ref):
        h = _rms(x_ref[...], g_ref[...]).astype(bf16)
        for j in range(nc):
            o_ref[:, j * cw:(j + 1) * cw] = jnp.dot(
                h, w_ref[:, j * cw:(j + 1) * cw], preferred_element_type=f32).astype(bf16)

    return _call_with_rider(
        body, 3, 1, ride, (T // tm,),
        [pl.BlockSpec((tm, D_MODEL), lambda i: (i, 0)), _const_spec((1, D_MODEL)), _const_spec((D_MODEL, PROJ_PAD))],
        [pl.BlockSpec((tm, PROJ_PAD), lambda i: (i, 0))], [jax.ShapeDtypeStruct((T, PROJ_PAD), bf16)],
        (x, g, w), name)


def _branch_fwd(proj, tabs, bw, tm, name):
    T = proj.shape[0]
    hb = tm // HALO

    def body(proj_ref, halo_ref, rc_ref, rs1_ref, rs2_ref, *rest):
        wrefs, (q_ref, k_ref, v_ref, ync_ref, yns_ref) = rest[:len(BR_W)], rest[len(BR_W):]
        i = pl.program_id(0)
        c = {"hmask": (i > 0).astype(f32), "rc": rc_ref[...], "rs1": rs1_ref[...], "rs2": rs2_ref[...]}
        q, k, v, ync, yns = _branch(_load_branch_inputs(proj_ref, halo_ref), _load_branch_weights(wrefs), c)
        q_ref[...] = q.astype(bf16)
        k_ref[...] = k.astype(bf16)
        v_ref[...] = v.astype(bf16)
        ync_ref[...] = ync.astype(bf16)
        yns_ref[...] = yns.astype(bf16)

    row = lambda wd: pl.BlockSpec((tm, wd), lambda i: (i, 0))
    wide = HEADS * HEAD_PAD
    return pl.pallas_call(
        body, name=name, grid=(T // tm,),
        in_specs=[row(PROJ_PAD), pl.BlockSpec((HALO, 2 * CONV_W), lambda i: (jnp.maximum(i * hb - 1, 0), 1)),
                  row(HEAD_PAD), row(HEAD_PAD), row(HEAD_PAD)] + [_const_spec(s) for _, s in BR_W],
        out_specs=[row(wide), row(wide), row(wide), row(CONV_W), row(SG_W)],
        out_shape=[jax.ShapeDtypeStruct((T, wide), bf16)] * 3 + [jax.ShapeDtypeStruct((T, CONV_W), bf16)] * 2,
    )(proj, proj, *tabs, *[bw[n] for n, _ in BR_W])


def _attn_fwd(q, k, v, tb, name, ride=None):
    T = q.shape[0]
    nb = T // tb
    pw = 2 * HEAD_PAD

    rc = min(64, tb)

    def body(q_ref, k_ref, v_ref, o_ref, lse_ref, p_scr):
        qi = pl.program_id(1)
        krow = lax.broadcasted_iota(jnp.int32, (rc, tb), 0)
        qcol = lax.broadcasted_iota(jnp.int32, (rc, tb), 1)
        heads = [slice(e * HEAD_PAD, (e + 1) * HEAD_PAD) for e in range(2)]
        qh = [q_ref[:, hs] for hs in heads]

        def step(off, carry, masked):
            s_all = [lax.dot_general(k_ref[pl.ds(off, tb), hs], qh[e], (((1,), (1,)), ((), ())),
                                     preferred_element_type=f32) for e, hs in enumerate(heads)]
            out = []
            for e, hs in enumerate(heads):
                m, l, acc = carry[e]
                blks = []
                m8 = jnp.full((8, tb), -1e30, f32)
                for r in range(0, tb, rc):
                    blk = s_all[e][r:r + rc, :]
                    if masked:
                        blk = jnp.where(krow + r <= qcol, blk, -1e30)
                    blks.append(blk)
                    m8 = jnp.maximum(m8, jnp.max(blk.reshape(rc // 8, 8, tb), axis=0))
                m_new = jnp.maximum(m, jnp.max(m8, axis=0, keepdims=True))
                alpha = jnp.exp2(m - m_new)
                p8 = jnp.zeros((8, tb), f32)
                for r in range(0, tb, rc):
                    p = jnp.exp2(blks[r // rc] - m_new)
                    p8 = p8 + jnp.sum(p.reshape(rc // 8, 8, tb), axis=0)
                    p_scr[e, r:r + rc, :] = p.astype(bf16)
                acc = alpha * acc + lax.dot_general(v_ref[pl.ds(off, tb), hs], p_scr[e], (((0,), (0,)), ((), ())),
                                                    preferred_element_type=f32)
                out.append((m_new, alpha * l + jnp.sum(p8, axis=0, keepdims=True), acc))
            return tuple(out)

        one = (jnp.full((1, tb), -1e30, f32), jnp.zeros((1, tb), f32), jnp.zeros((HEAD_PAD, tb), f32))
        carry = lax.fori_loop(0, qi, lambda ki, cr: step(pl.multiple_of(ki * tb, tb), cr, False), (one, one))
        carry = step(pl.multiple_of(qi * tb, tb), carry, True)
        o_t = jnp.zeros((HEAD_PAD, tb), f32)
        for e in range(2):
            m, l, acc = carry[e]
            o_t = o_t + acc / l
            lse_ref[e] = m + jnp.log2(l)
        o_ref[...] = o_t.T

    return _call_with_rider(
        body, 3, 2, ride, (HEADS // 2, nb),
        [pl.BlockSpec((tb, pw), lambda h, i: (i, h)), pl.BlockSpec((T, pw), lambda h, i: (0, h)),
         pl.BlockSpec((T, pw), lambda h, i: (0, h))],
        [pl.BlockSpec((tb, HEAD_PAD), lambda h, i: (i, h)), pl.BlockSpec((2, 1, tb), lambda h, i: (h, 0, i))],
        [jax.ShapeDtypeStruct((T, MLA_W), f32), jax.ShapeDtypeStruct((HEADS, 1, T), f32)],
        (q, k, v), name, scratch=[pltpu.VMEM((2, tb, tb), bf16)])


def _out_layer(x_ref, o_ref, zm_ref, ync_ref, yns_ref, g_ref, w_ref):
    ynm = _mla_out(o_ref[...], zm_ref[...].astype(f32), g_ref[...]).astype(bf16)
    y = x_ref[...]
    y = y + jnp.dot(ync_ref[...], w_ref[0:CONV_W, :], preferred_element_type=f32)
    y = y + jnp.dot(ynm, w_ref[CONV_W:CONV_W + MLA_W, :], preferred_element_type=f32)
    return y + jnp.dot(yns_ref[...], w_ref[CONV_W + MLA_W:, :], preferred_element_type=f32)


def _out_fwd(x, o, proj, ync, yns, g_m, w_out, tm, name):
    T = x.shape[0]

    def body(*refs):
        refs[-1][...] = _out_layer(*refs[:-1])

    row = lambda wd: pl.BlockSpec((tm, wd), lambda i: (i, 0))
    return pl.pallas_call(
        body, name=name, grid=(T // tm,),
        in_specs=[row(D_MODEL), row(MLA_W), row(MLA_W), row(CONV_W), row(SG_W), _const_spec((1, MLA_W)),
                  _const_spec((D_MODEL, D_MODEL))],
        out_specs=row(D_MODEL),
        out_shape=jax.ShapeDtypeStruct((T, D_MODEL), f32),
    )(x, o, proj, ync, yns, g_m, w_out)


def _out_fwd_loss(x, o, proj, ync, yns, g_m, w_out, tgt, tm, name):
    T = x.shape[0]
    nt = T // tm

    def body(x_ref, o_ref, zm_ref, ync_ref, yns_ref, g_ref, w_ref, t_ref, dy_ref, loss_ref, acc_ref):
        i = pl.program_id(0)

        @pl.when(i == 0)
        def _():
            acc_ref[...] = jnp.zeros_like(acc_ref)

        err = _out_layer(x_ref, o_ref, zm_ref, ync_ref, yns_ref, g_ref, w_ref) - t_ref[...]
        dy_ref[...] = err * (1.0 / D_MODEL)
        acc_ref[...] += jnp.sum(err * err, axis=0, keepdims=True)

        @pl.when(i == nt - 1)
        def _():
            loss_ref[...] = jnp.full((1, HEAD_PAD), 0.5 / D_MODEL, f32) * jnp.sum(acc_ref[...])

    row = lambda wd: pl.BlockSpec((tm, wd), lambda i: (i, 0))
    return pl.pallas_call(
        body, name=name, grid=(nt,),
        in_specs=[row(D_MODEL), row(MLA_W), row(MLA_W), row(CONV_W), row(SG_W), _const_spec((1, MLA_W)),
                  _const_spec((D_MODEL, D_MODEL)), row(D_MODEL)],
        out_specs=[row(D_MODEL), _const_spec((1, HEAD_PAD))],
        out_shape=[jax.ShapeDtypeStruct((T, D_MODEL), f32), jax.ShapeDtypeStruct((1, HEAD_PAD), f32)],
        scratch_shapes=[pltpu.VMEM((1, D_MODEL), f32)],
    )(x, o, proj, ync, yns, g_m, w_out, tgt)


def _out_bwd(dout, o, proj, ync, yns, g_m, w_out, tm, name):
    T = dout.shape[0]

    def body(dy_ref, o_ref, zm_ref, ync_ref, yns_ref, g_ref, w_ref,
             do_ref, dl_ref, dzm_ref, dync_ref, dyns_ref, dw_ref, dg_ref):
        i = pl.program_id(0)

        @pl.when(i == 0)
        def _():
            dw_ref[...] = jnp.zeros_like(dw_ref)
            dg_ref[...] = jnp.zeros_like(dg_ref)

        dyb = dy_ref[...].astype(bf16)
        nt = (((1,), (1,)), ((), ()))
        tn = (((0,), (0,)), ((), ()))
        d_c = lax.dot_general(dyb, w_ref[0:CONV_W, :], nt, preferred_element_type=f32)
        d_m = lax.dot_general(dyb, w_ref[CONV_W:CONV_W + MLA_W, :], nt, preferred_element_type=f32)
        d_s = lax.dot_general(dyb, w_ref[CONV_W + MLA_W:, :], nt, preferred_element_type=f32)
        o = o_ref[...]
        ynm, vjp = jax.vjp(_mla_out, o, zm_ref[...].astype(f32), g_ref[...])
        do, dzm, dg = vjp(d_m)
        do_ref[...] = do.astype(bf16)
        dzm_ref[...] = dzm.astype(bf16)
        dync_ref[...] = d_c.astype(bf16)
        dyns_ref[...] = d_s.astype(bf16)
        dg_ref[...] += dg
        dw_ref[0:CONV_W, :] += lax.dot_general(ync_ref[...], dyb, tn, preferred_element_type=f32)
        dw_ref[CONV_W:CONV_W + MLA_W, :] += lax.dot_general(ynm.astype(bf16), dyb, tn, preferred_element_type=f32)
        dw_ref[CONV_W + MLA_W:, :] += lax.dot_general(yns_ref[...], dyb, tn, preferred_element_type=f32)
        prod = do * o
        hi = prod.astype(bf16)
        lo = (prod - hi.astype(f32)).astype(bf16)
        sel = (lax.broadcasted_iota(jnp.int32, (HEADS, MLA_W), 1) // VDIM
               == lax.broadcasted_iota(jnp.int32, (HEADS, MLA_W), 0)).astype(bf16)
        dl_ref[...] = (lax.dot_general(sel, hi, nt, preferred_element_type=f32)
                       + lax.dot_general(sel, lo, nt, preferred_element_type=f32))

    row = lambda wd: pl.BlockSpec((tm, wd), lambda i: (i, 0))
    return pl.pallas_call(
        body, name=name, grid=(T // tm,),
        in_specs=[row(D_MODEL), row(MLA_W), row(MLA_W), row(CONV_W), row(SG_W), _const_spec((1, MLA_W)),
                  _const_spec((D_MODEL, D_MODEL))],
        out_specs=[row(MLA_W), pl.BlockSpec((HEADS, tm), lambda i: (0, i)), row(MLA_W), row(CONV_W), row(SG_W),
                   _const_spec((D_MODEL, D_MODEL)), _const_spec((1, MLA_W))],
        out_shape=[jax.ShapeDtypeStruct((T, MLA_W), bf16), jax.ShapeDtypeStruct((HEADS, T), f32),
                   jax.ShapeDtypeStruct((T, MLA_W), bf16), jax.ShapeDtypeStruct((T, CONV_W), bf16),
                   jax.ShapeDtypeStruct((T, SG_W), bf16), jax.ShapeDtypeStruct((D_MODEL, D_MODEL), f32),
                   jax.ShapeDtypeStruct((1, MLA_W), f32)],
    )(dout, o, proj, ync, yns, g_m, w_out)


def _attn_bwd(q, k, v, do, lse, delta, tb, name, ride=None):
    T = q.shape[0]
    nb = T // tb
    pw = 2 * HEAD_PAD
    nt = (((1,), (1,)), ((), ()))
    tn = (((0,), (0,)), ((), ()))

    def body(q_ref, do_ref, lse_ref, dl_ref, k_ref, v_ref, dq_ref, dk_ref, dv_ref):
        kj = pl.program_id(1)

        @pl.when(kj == 0)
        def _():
            dq_ref[...] = jnp.zeros_like(dq_ref)

        krow = lax.broadcasted_iota(jnp.int32, (tb, tb), 0)
        qcol = lax.broadcasted_iota(jnp.int32, (tb, tb), 1)
        heads = [slice(e * HEAD_PAD, (e + 1) * HEAD_PAD) for e in range(2)]
        kb = [k_ref[:, hs] for hs in heads]
        vb = [v_ref[:, hs] for hs in heads]

        def step(off, carry, masked):
            dob = do_ref[pl.ds(off, tb), :]
            out = []
            for e, hs in enumerate(heads):
                dk, dv = carry[e]
                qb = q_ref[pl.ds(off, tb), hs]
                s_t = lax.dot_general(kb[e], qb, nt, preferred_element_type=f32)
                p_t = jnp.exp2(s_t - lse_ref[e, :, pl.ds(off, tb)])
                if masked:
                    p_t = jnp.where(krow <= qcol, p_t, 0.0)
                dv = dv + jnp.dot(p_t.astype(bf16), dob, preferred_element_type=f32)
                dp_t = lax.dot_general(vb[e], dob, nt, preferred_element_type=f32)
                ds_t = (p_t * (dp_t - dl_ref[e, :, pl.ds(off, tb)])).astype(bf16)
                dk = dk + jnp.dot(ds_t, qb, preferred_element_type=f32)
                dq_ref[pl.ds(off, tb), hs] += lax.dot_general(ds_t, kb[e], tn, preferred_element_type=f32)
                out.append((dk, dv))
            return tuple(out)

        zero = jnp.zeros((tb, HEAD_PAD), f32)
        carry = step(pl.multiple_of(kj * tb, tb), ((zero, zero), (zero, zero)), True)
        carry = lax.fori_loop(kj + 1, nb, lambda qi, cr: step(pl.multiple_of(qi * tb, tb), cr, False), carry)
        for e, hs in enumerate(heads):
            dk_ref[:, hs] = carry[e][0]
            dv_ref[:, hs] = carry[e][1]

    wide = HEADS * HEAD_PAD
    return _call_with_rider(
        body, 6, 3, ride, (HEADS // 2, nb),
        [pl.BlockSpec((T, pw), lambda h, j: (0, h)), pl.BlockSpec((T, HEAD_PAD), lambda h, j: (0, h)),
         pl.BlockSpec((2, 1, T), lambda h, j: (h, 0, 0)), pl.BlockSpec((2, 1, T), lambda h, j: (h, 0, 0)),
         pl.BlockSpec((tb, pw), lambda h, j: (j, h)), pl.BlockSpec((tb, pw), lambda h, j: (j, h))],
        [pl.BlockSpec((T, pw), lambda h, j: (0, h)), pl.BlockSpec((tb, pw), lambda h, j: (j, h)),
         pl.BlockSpec((tb, pw), lambda h, j: (j, h))],
        [jax.ShapeDtypeStruct((T, wide), f32)] * 3,
        (q, do, lse, delta, k, v), name)


def _branch_bwd(proj, tabs, bw, dq, dk, dv, dync, dyns, tm, name):
    T = proj.shape[0]
    nt = T // tm
    hb = tm // HALO
    nw = len(BR_W)

    def body(proj_ref, halo_ref, rc_ref, rs1_ref, rs2_ref, *rest):
        wrefs = rest[:nw]
        dq_ref, dk_ref, dv_ref, dync_ref, dyns_ref = rest[nw:nw + 5]
        dp_ref = rest[nw + 5]
        gwrefs = rest[nw + 6:2 * nw + 6]
        carry_ref = rest[2 * nw + 6]
        i = pl.program_id(0)
        r = nt - 1 - i

        @pl.when(i == 0)
        def _():
            carry_ref[...] = jnp.zeros_like(carry_ref)
            for g in gwrefs:
                g[...] = jnp.zeros_like(g)

        c = {"hmask": (r > 0).astype(f32), "rc": rc_ref[...], "rs1": rs1_ref[...], "rs2": rs2_ref[...]}
        _, vjp = jax.vjp(lambda p_, w_: _branch(p_, w_, c), _load_branch_inputs(proj_ref, halo_ref),
                         _load_branch_weights(wrefs))
        cts = (dq_ref[...] * LN2, dk_ref[...] * LN2, dv_ref[...], dync_ref[...].astype(f32),
               dyns_ref[...].astype(f32))
        dp, dw = vjp(cts)
        for n in SEG:
            if n in ("zm", "a", "glu"):
                continue
            dp_ref[:, SEG[n][0] - 512:SEG[n][1] - 512] = dp[n].astype(bf16)
        for n, hn, lo in (("a", "ha", 0), ("glu", "hglu", CONV_W)):
            d = dp[n]
            tail = d[tm - HALO:, :] + carry_ref[:, lo:lo + CONV_W]
            s0 = SEG[n][0] - 512
            dp_ref[0:tm - HALO, s0:s0 + CONV_W] = d[:tm - HALO, :].astype(bf16)
            dp_ref[tm - HALO:tm, s0:s0 + CONV_W] = tail.astype(bf16)
        carry_ref[:, 0:CONV_W] = dp["ha"]
        carry_ref[:, CONV_W:] = dp["hglu"]
        for (n, _), g in zip(BR_W, gwrefs):
            if n in ("sg_w", "sg_b"):
                for gi in range(SG_HEADS):
                    g[gi] += dw[n][gi]
            else:
                g[...] += dw[n]

    row = lambda wd: pl.BlockSpec((tm, wd), lambda i: (nt - 1 - i, 0))
    wide = HEADS * HEAD_PAD
    return pl.pallas_call(
        body, name=name, grid=(nt,),
        in_specs=[row(PROJ_PAD),
                  pl.BlockSpec((HALO, 2 * CONV_W), lambda i: (jnp.maximum((nt - 1 - i) * hb - 1, 0), 1)),
                  row(HEAD_PAD), row(HEAD_PAD), row(HEAD_PAD)] + [_const_spec(s) for _, s in BR_W]
                 + [row(wide), row(wide), row(wide), row(CONV_W), row(SG_W)],
        out_specs=[row(REST)] + [_const_spec(s) for _, s in BR_W],
        out_shape=[jax.ShapeDtypeStruct((T, REST), bf16)] + [jax.ShapeDtypeStruct(s, f32) for _, s in BR_W],
        scratch_shapes=[pltpu.VMEM((HALO, 2 * CONV_W), f32)],
    )(proj, proj, *tabs, *[bw[n] for n, _ in BR_W], dq, dk, dv, dync, dyns)


def _inproj_bwd_dx(x, g, w, dzm, dprest, dout, tm, name, ride=None):
    T = x.shape[0]
    nt_dims = (((1,), (1,)), ((), ()))

    def body(x_ref, g_ref, w_ref, dzm_ref, dpr_ref, dout_ref, dx_ref, dg_ref, h_ref):
        i = pl.program_id(0)

        @pl.when(i == 0)
        def _():
            dg_ref[...] = jnp.zeros_like(dg_ref)

        dh = lax.dot_general(dzm_ref[...], w_ref[:, 0:512], nt_dims, preferred_element_type=f32)
        dh = dh + lax.dot_general(dpr_ref[...], w_ref[:, 512:], nt_dims, preferred_element_type=f32)
        h, vjp = jax.vjp(_rms, x_ref[...], g_ref[...])
        dx, dg = vjp(dh)
        dx_ref[...] = dout_ref[...] + dx
        dg_ref[...] += dg
        h_ref[...] = h.astype(bf16)

    row = lambda wd: pl.BlockSpec((tm, wd), lambda i: (i, 0))
    return _call_with_rider(
        body, 6, 3, ride, (T // tm,),
        [row(D_MODEL), _const_spec((1, D_MODEL)), _const_spec((D_MODEL, PROJ_PAD)), row(512), row(REST),
         row(D_MODEL)],
        [row(D_MODEL), _const_spec((1, D_MODEL)), row(D_MODEL)],
        [jax.ShapeDtypeStruct((T, D_MODEL), f32), jax.ShapeDtypeStruct((1, D_MODEL), f32),
         jax.ShapeDtypeStruct((T, D_MODEL), bf16)],
        (x, g, w, dzm, dprest, dout), name)


def _inproj_bwd_dw(h, dzm, dprest, tm, name, ride=None):
    T = h.shape[0]
    tm = min(4 * tm, T)
    nt = T // tm
    cw = REST // 3
    tn = (((0,), (0,)), ((), ()))

    def make_body():
        def body(h_ref, d_ref, o_ref):
            i = pl.program_id(1)

            @pl.when(i == 0)
            def _():
                o_ref[...] = jnp.zeros_like(o_ref)

            o_ref[...] += lax.dot_general(h_ref[...], d_ref[...], tn, preferred_element_type=f32)
        return body

    def run(d, width, cwid, nm, rd):
        return _call_with_rider(
            make_body(), 2, 1, rd, (width // cwid, nt),
            [pl.BlockSpec((tm, D_MODEL), lambda j, i: (i, 0)), pl.BlockSpec((tm, cwid), lambda j, i: (i, j))],
            [pl.BlockSpec((D_MODEL, cwid), lambda j, i: (0, j))], [jax.ShapeDtypeStruct((D_MODEL, width), f32)],
            (h, d), nm)

    (zm,) = run(dzm, 512, 512, name + "_zm", None)
    rest, *sent = run(dprest, REST, cw, name + "_rest", ride)
    return (zm, rest, *sent)


def _row_tile(rows):
    return ROW_ALIGN if rows > ROW_ALIGN and rows % ROW_ALIGN == 0 else rows


def _sum4(buf, name):
    _, R, C = buf.shape
    tr = _row_tile(R)

    def body(b_ref, o_ref):
        b = [b_ref[s].astype(f32) for s in range(4)]
        o_ref[...] = ((b[0] + b[1]) + b[2]) + b[3]

    return pl.pallas_call(
        body, name=name, grid=(R // tr,),
        in_specs=[pl.BlockSpec((4, tr, C), lambda i: (0, i, 0))],
        out_specs=pl.BlockSpec((tr, C), lambda i: (i, 0)),
        out_shape=jax.ShapeDtypeStruct((R, C), f32),
    )(buf)


def _sum4_whole(bufs, name):
    n = len(bufs)

    def body(*refs):
        for b_ref, o_ref in zip(refs[:n], refs[n:]):
            o_ref[...] = ((b_ref[0] + b_ref[1]) + b_ref[2]) + b_ref[3]

    return pl.pallas_call(
        body, name=name, out_shape=[jax.ShapeDtypeStruct(b.shape[1:], f32) for b in bufs])(*bufs)


def _adamw_math(w, m, v, g):
    m_new = ADAM_B1 * m + (1.0 - ADAM_B1) * g
    v_new = ADAM_B2 * v + (1.0 - ADAM_B2) * (g * g)
    m_hat = m_new / (1.0 - ADAM_B1 ** ADAM_STEP)
    v_hat = v_new / (1.0 - ADAM_B2 ** ADAM_STEP)
    return -ADAM_LR * (m_hat / (jnp.sqrt(v_hat) + ADAM_EPS) + ADAM_WD * w), m_new, v_new


def _adamw(w, m, v, p_a, p_b, name):
    R, C = w.shape
    tr = _row_tile(R)

    def body(w_ref, m_ref, v_ref, a_ref, b_ref, g_ref, d_ref, nm_ref, nv_ref):
        g = a_ref[...] + b_ref[...]
        g_ref[...] = g
        d_ref[...], nm_ref[...], nv_ref[...] = _adamw_math(w_ref[...], m_ref[...], v_ref[...], g)

    spec = pl.BlockSpec((tr, C), lambda i: (i, 0))
    return pl.pallas_call(
        body, name=name, grid=(R // tr,), in_specs=[spec] * 5, out_specs=[spec] * 4,
        out_shape=[jax.ShapeDtypeStruct((R, C), f32)] * 4,
    )(w, m, v, p_a, p_b)


def _adamw_whole(ws, ms, vs, p_a, p_b, name):
    n = len(ws)

    def body(*refs):
        ins, outs = refs[:5 * n], refs[5 * n:]
        for i in range(n):
            w, m, v, a, b = (ins[k * n + i][...] for k in range(5))
            g = a + b
            outs[i][...] = g
            outs[n + i][...], outs[2 * n + i][...], outs[3 * n + i][...] = _adamw_math(w, m, v, g)

    res = pl.pallas_call(
        body, name=name, out_shape=[jax.ShapeDtypeStruct(w.shape, f32) for w in ws] * 4,
    )(*ws, *ms, *vs, *p_a, *p_b)
    return [res[k * n:(k + 1) * n] for k in range(4)]


class _ChipExchange:
    def __init__(self, srcs, per_target):
        self.n = len(srcs)
        self.per_target = per_target
        blocks = [s.shape[1:] if p else s.shape for s, p in zip(srcs, per_target)]
        self.out_shape = [jax.ShapeDtypeStruct((4,) + tuple(b), s.dtype) for b, s in zip(blocks, srcs)]
        self.scratch_shapes = [pltpu.SemaphoreType.DMA((3 * self.n,)), pltpu.SemaphoreType.DMA((3 * self.n,)),
                               pltpu.SemaphoreType.DMA((self.n,))]
        self.any_specs = [pl.BlockSpec(memory_space=pl.ANY)] * self.n

    def _copies(self, src_refs, out_refs, sems):
        send_sems, recv_sems, local_sems = sems
        x, y, c = lax.axis_index("x"), lax.axis_index("y"), lax.axis_index("c")
        me = 2 * x + y
        chips = [(1 - x, y), (x, 1 - y), (1 - x, 1 - y)]

        def block_for(i, t):
            return src_refs[i].at[t] if self.per_target[i] else src_refs[i]

        def remote(i, k, block_t, slot):
            tx, ty = chips[k]
            return pltpu.make_async_remote_copy(
                src_ref=block_for(i, block_t), dst_ref=out_refs[i].at[slot], send_sem=send_sems.at[3 * i + k],
                recv_sem=recv_sems.at[3 * i + k], device_id=(tx, ty, c), device_id_type=pl.DeviceIdType.MESH)

        pairs = [(i, k) for i in range(self.n) for k in range(3)]
        sends = [remote(i, k, 2 * chips[k][0] + chips[k][1], me) for i, k in pairs]
        recvs = [remote(i, k, me, 2 * chips[k][0] + chips[k][1]) for i, k in pairs]
        mine = [pltpu.make_async_copy(block_for(i, me), out_refs[i].at[me], local_sems.at[i])
                for i in range(self.n)]
        return sends, recvs, mine

    def start(self, src_refs, out_refs, sems):
        sends, _, mine = self._copies(src_refs, out_refs, sems)
        for cp in sends + mine:
            cp.start()

    def wait(self, src_refs, out_refs, sems):
        sends, recvs, mine = self._copies(src_refs, out_refs, sems)
        for cp in recvs:
            cp.wait_recv()
        for cp in sends:
            cp.wait_send()
        for cp in mine:
            cp.wait()


def _call_with_rider(body, n_in, n_out, ride, grid, in_specs, out_specs, out_shape, args, name, scratch=()):
    scratch = list(scratch)
    if ride is None:
        return pl.pallas_call(body, name=name, grid=grid, in_specs=in_specs, out_specs=out_specs,
                              out_shape=out_shape, scratch_shapes=scratch)(*args)
    ex = _ChipExchange(*ride)
    ne = ex.n

    def wrapped(*refs):
        ins, src_refs = refs[:n_in], refs[n_in:n_in + ne]
        outs, ex_outs = refs[n_in + ne:n_in + ne + n_out], refs[n_in + ne + n_out:n_in + 2 * ne + n_out]
        own = refs[n_in + 2 * ne + n_out:n_in + 2 * ne + n_out + len(scratch)]
        sems = refs[n_in + 2 * ne + n_out + len(scratch):]
        ids = [pl.program_id(a) for a in range(len(grid))]
        first = functools.reduce(jnp.logical_and, [i == 0 for i in ids])
        last = functools.reduce(jnp.logical_and, [i == g - 1 for i, g in zip(ids, grid)])

        @pl.when(first)
        def _():
            ex.start(src_refs, ex_outs, sems)

        body(*ins, *outs, *own)

        @pl.when(last)
        def _():
            ex.wait(src_refs, ex_outs, sems)

    return pl.pallas_call(
        wrapped, name=name, grid=grid, in_specs=list(in_specs) + ex.any_specs,
        out_specs=list(out_specs) + ex.any_specs, out_shape=list(out_shape) + ex.out_shape,
        scratch_shapes=scratch + ex.scratch_shapes,
    )(*args, *ride[0])


def _chip_exchange(srcs, per_target, name):
    ex = _ChipExchange(srcs, per_target)
    n = ex.n

    def body(*refs):
        src_refs, out_refs, sems = refs[:n], refs[n:2 * n], refs[2 * n:]
        ex.start(src_refs, out_refs, sems)
        ex.wait(src_refs, out_refs, sems)

    return pl.pallas_call(
        body, name=name, in_specs=ex.any_specs, out_specs=ex.any_specs, out_shape=ex.out_shape,
        scratch_shapes=ex.scratch_shapes,
    )(*srcs)


def _sibling_exchange(srcs, name):
    n = len(srcs)

    def body(*refs):
        src_refs, out_refs = refs[:n], refs[n:2 * n]
        send_sems, recv_sems = refs[2 * n:]
        x, y, c = lax.axis_index("x"), lax.axis_index("y"), lax.axis_index("c")
        copies = [pltpu.make_async_remote_copy(
            src_ref=src_refs[i], dst_ref=out_refs[i], send_sem=send_sems.at[i], recv_sem=recv_sems.at[i],
            device_id=(x, y, 1 - c), device_id_type=pl.DeviceIdType.MESH) for i in range(n)]
        for cp in copies:
            cp.start()
        for cp in copies:
            cp.wait()

    return pl.pallas_call(
        body, name=name,
        in_specs=[pl.BlockSpec(memory_space=pl.ANY)] * n, out_specs=[pl.BlockSpec(memory_space=pl.ANY)] * n,
        out_shape=[jax.ShapeDtypeStruct(s.shape, s.dtype) for s in srcs],
        scratch_shapes=[pltpu.SemaphoreType.DMA((n,)), pltpu.SemaphoreType.DMA((n,))],
    )(*srcs)


W_IN_SHARD = IN_COLS // 4
W_IN_MAP = [(0, 1792, 512), (1792, 1824, 2368), (1824, 2336, 0), (2336, IN_COLS, 2432)]


def _pad_w_in(shards):
    def cols(a, b):
        out = []
        while a < b:
            j = a // W_IN_SHARD
            e = min(b, (j + 1) * W_IN_SHARD)
            out.append(shards[j][:, a - j * W_IN_SHARD:e - j * W_IN_SHARD])
            a = e
        return out

    z = lambda n: [jnp.zeros((D_MODEL, n), shards.dtype)]
    return jnp.concatenate(cols(1824, 2336) + cols(0, 1792) + z(64) + cols(1792, 1824) + z(32) + cols(2336, IN_COLS),
                           axis=1)


def _w_in_grad_shard(dw_zm, dw_rest, j):
    lo, hi = j * W_IN_SHARD, (j + 1) * W_IN_SHARD
    out = []
    for a, b, p in W_IN_MAP:
        a2, b2 = max(a, lo), min(b, hi)
        if a2 < b2:
            p0 = p + a2 - a
            src, off = (dw_zm, 0) if p0 < 512 else (dw_rest, 512)
            out.append(src[:, p0 - off:p0 - off + b2 - a2])
    return jnp.concatenate(out, axis=1)


def _v_cols():
    return [(h % 2) * VDIM for h in range(HEADS)]


def _layer_weights(gathered, small, l):
    full = {n: jnp.concatenate([g[j] for j in range(4)], axis=BIG[n] - 1) for n, g in gathered.items()}
    w_uq = full["w_uq"].astype(f32).reshape(Q_LORA, HEADS, QK)
    w_uq = jnp.pad(w_uq, ((0, 0), (0, 0), (0, HEAD_PAD - QK))).reshape(Q_LORA, HEADS * HEAD_PAD)
    ukv = full["w_ukv"].astype(f32).reshape(KV_LORA, HEADS, NOPE + VDIM)
    w_k = jnp.pad(ukv[:, :, :NOPE], ((0, 0), (0, 0), (0, HEAD_PAD - NOPE))).reshape(KV_LORA, HEADS * HEAD_PAD)
    zeros = jnp.zeros((KV_LORA, VDIM), f32)
    w_v = jnp.concatenate(
        [jnp.concatenate([ukv[:, h, NOPE:], zeros] if h % 2 == 0 else [zeros, ukv[:, h, NOPE:]], axis=1)
         for h in range(HEADS)], axis=1)
    row = lambda a: a.reshape(1, -1)
    bn = small["branch_norm_g"][l]
    bw = {
        "conv_w": jnp.pad(full["conv_w"], ((0, 1), (0, 0))), "conv_b": row(small["conv_b"][l]),
        "cln_g": row(small["conv_ln_g"][l]), "cln_b": row(small["conv_ln_b"][l]),
        "pw_w": full["conv_pw_w"].astype(f32), "pw_b": row(small["conv_pw_b"][l]),
        "qn_g": row(small["q_norm_g"][l]), "w_uq": w_uq, "kvn_g": row(small["kv_norm_g"][l]),
        "w_k": w_k, "w_v": w_v,
        "qkq_g": jnp.pad(row(small["qk_q_g"][l]), ((0, 0), (0, HEAD_PAD - QK))),
        "qkk_g": jnp.pad(row(small["qk_k_g"][l]), ((0, 0), (0, HEAD_PAD - QK))),
        "sln_g": row(small["sg_ln_g"][l]), "sln_b": row(small["sg_ln_b"][l]),
        "sg_w": small["sg_w"][l], "sg_b": small["sg_b"][l].reshape(SG_HEADS, SG_CHUNK, 1),
        "bn_c": row(bn[:CONV_W]), "bn_s": row(bn[CONV_W + MLA_W:]),
    }
    return {"bw": bw, "bn_m": row(bn[CONV_W:CONV_W + MLA_W]), "w_out": full["w_out"]}


def _layer_grads(gb, dw_out, dbn_m):
    duq = gb["w_uq"].reshape(Q_LORA, HEADS, HEAD_PAD)[:, :, :QK].reshape(Q_LORA, HEADS * QK)
    dk = gb["w_k"].reshape(KV_LORA, HEADS, HEAD_PAD)[:, :, :NOPE]
    dv = gb["w_v"].reshape(KV_LORA, HEADS, HEAD_PAD)
    dv = jnp.stack([dv[:, h, c0:c0 + VDIM] for h, c0 in enumerate(_v_cols())], axis=1)
    dukv = jnp.concatenate([dk, dv], axis=2).reshape(KV_LORA, HEADS * (NOPE + VDIM))
    return {
        "conv_w": gb["conv_w"][:CONV_K],
        "conv_b": gb["conv_b"][0], "conv_ln_g": gb["cln_g"][0], "conv_ln_b": gb["cln_b"][0],
        "conv_pw_w": gb["pw_w"], "conv_pw_b": gb["pw_b"][0], "q_norm_g": gb["qn_g"][0], "w_uq": duq,
        "kv_norm_g": gb["kvn_g"][0], "w_ukv": dukv, "qk_q_g": gb["qkq_g"][0, :QK], "qk_k_g": gb["qkk_g"][0, :QK],
        "sg_ln_g": gb["sln_g"][0], "sg_ln_b": gb["sln_b"][0], "sg_w": gb["sg_w"], "sg_b": gb["sg_b"][:, :, 0],
        "branch_norm_g": jnp.concatenate([gb["bn_c"][0], dbn_m[0], gb["bn_s"][0]]), "w_out": dw_out,
    }


def _rope_tables(T):
    half = ROPE // 2
    inv_freq = ROPE_THETA ** (-jnp.arange(half, dtype=f32) / half)
    ang = jnp.arange(T, dtype=f32)[:, None] * inv_freq[None, :]
    cos, sin = jnp.cos(ang), jnp.sin(ang)
    one = jnp.ones((T, NOPE), f32)
    z = lambda n: jnp.zeros((T, n), f32)
    rc = jnp.concatenate([one, cos, cos, jnp.ones((T, HEAD_PAD - QK), f32)], axis=1)
    rs1 = jnp.concatenate([z(NOPE), -sin, z(half), z(HEAD_PAD - QK)], axis=1)
    rs2 = jnp.concatenate([z(NOPE), z(half), sin, z(HEAD_PAD - QK)], axis=1)
    return rc, rs1, rs2


def kernel(x, norm_g, w_in, conv_w, conv_b, conv_ln_g, conv_ln_b, conv_pw_w, conv_pw_b, q_norm_g, w_uq, kv_norm_g, w_ukv, qk_q_g, qk_k_g, sg_ln_g, sg_ln_b, sg_w, sg_b, branch_norm_g, w_out, loss_target, m_norm_g, m_w_in, m_conv_w, m_conv_b, m_conv_ln_g, m_conv_ln_b, m_conv_pw_w, m_conv_pw_b, m_q_norm_g, m_w_uq, m_kv_norm_g, m_w_ukv, m_qk_q_g, m_qk_k_g, m_sg_ln_g, m_sg_ln_b, m_sg_w, m_sg_b, m_branch_norm_g, m_w_out, v_norm_g, v_w_in, v_conv_w, v_conv_b, v_conv_ln_g, v_conv_ln_b, v_conv_pw_w, v_conv_pw_b, v_q_norm_g, v_w_uq, v_kv_norm_g, v_w_ukv, v_qk_q_g, v_qk_k_g, v_sg_ln_g, v_sg_ln_b, v_sg_w, v_sg_b, v_branch_norm_g, v_w_out):
    wts = dict(zip(WEIGHTS, [norm_g, w_in, conv_w, conv_b, conv_ln_g, conv_ln_b, conv_pw_w, conv_pw_b, q_norm_g,
                             w_uq, kv_norm_g, w_ukv, qk_q_g, qk_k_g, sg_ln_g, sg_ln_b, sg_w, sg_b, branch_norm_g,
                             w_out]))
    mom_m = dict(zip(WEIGHTS, [m_norm_g, m_w_in, m_conv_w, m_conv_b, m_conv_ln_g, m_conv_ln_b, m_conv_pw_w,
                               m_conv_pw_b, m_q_norm_g, m_w_uq, m_kv_norm_g, m_w_ukv, m_qk_q_g, m_qk_k_g,
                               m_sg_ln_g, m_sg_ln_b, m_sg_w, m_sg_b, m_branch_norm_g, m_w_out]))
    mom_v = dict(zip(WEIGHTS, [v_norm_g, v_w_in, v_conv_w, v_conv_b, v_conv_ln_g, v_conv_ln_b, v_conv_pw_w,
                               v_conv_pw_b, v_q_norm_g, v_w_uq, v_kv_norm_g, v_w_ukv, v_qk_q_g, v_qk_k_g,
                               v_sg_ln_g, v_sg_ln_b, v_sg_w, v_sg_b, v_branch_norm_g, v_w_out]))
    xs = x[0]
    tgt = loss_target[0]
    T = xs.shape[0]
    tm = min(256, T)
    tmb = min(128, T)
    tmm = min(512, T)
    tb = min(512, T // 2)

    big = list(BIG)
    rest = [n for n in big if n != "w_in"]
    nb = len(big)
    shards = [{n: wts[n][l] if n == "conv_w" else wts[n][l].astype(bf16) for n in big} for l in range(DEPTH)]
    small = {n: wts[n] for n in SMALL}
    tabs = _rope_tables(T)
    gathered = dict(zip(["w_in"], _chip_exchange([shards[0]["w_in"]], [False], "gather_w_in_0")))

    acts, lw = [], []
    h_in = xs
    for l in range(DEPTH):
        W = {"norm_g": small["norm_g"][l].reshape(1, -1), "w_in": _pad_w_in(gathered["w_in"])}
        ride = ([shards[l][n] for n in rest], [False] * len(rest)) if l == 0 else None
        proj, *late = _inproj_fwd(h_in, W["norm_g"], W["w_in"], tmm, f"inproj_fwd_{l}", ride)
        if ride is not None:
            gathered.update(zip(rest, late))
        W.update(_layer_weights({n: gathered[n] for n in rest}, small, l))
        lw.append(W)
        q, k, v, ync, yns = _branch_fwd(proj, tabs, W["bw"], tm, f"branch_fwd_{l}")
        ride = ([shards[l + 1][n] for n in big], [False] * nb) if l + 1 < DEPTH else None
        o, lse, *ahead = _attn_fwd(q, k, v, tb, f"attn_fwd_{l}", ride)
        gathered = dict(zip(big, ahead))
        acts.append((h_in, proj, q, k, v, ync, yns, o, lse))
        if l + 1 < DEPTH:
            h_in = _out_fwd(h_in, o, proj, ync, yns, W["bn_m"], W["w_out"], tmm, f"out_fwd_{l}")
        else:
            dy, loss_row = _out_fwd_loss(h_in, o, proj, ync, yns, W["bn_m"], W["w_out"], tgt, tmm, f"out_fwd_{l}")
    loss = lax.psum(loss_row[0, 0], ("x", "y", "c"))

    def for_chips(n, g):
        if n == "w_in":
            blocks = [_w_in_grad_shard(*g, j) for j in range(4)]
        else:
            blocks = jnp.split(g, 4, axis=BIG[n] - 1)
        return jnp.stack(blocks).astype(bf16)

    mid = [n for n in rest if n != "w_out"]
    grads = [None] * DEPTH
    contrib = {}
    for l in reversed(range(DEPTH)):
        W = lw[l]
        xin, proj, q, k, v, ync, yns, o, lse = acts[l]
        do, delta, dzm, dync, dyns, dw_out, dbn_m = _out_bwd(
            dy, o, proj, ync, yns, W["bn_m"], W["w_out"], tmm, f"out_bwd_{l}")
        keys = [(l, "w_out")] + ([(l + 1, "w_in")] if l + 1 < DEPTH else [])
        srcs = [for_chips("w_out", dw_out)] + ([for_chips("w_in", grads[l + 1]["w_in"])] if l + 1 < DEPTH else [])
        dq, dk, dv, *sent = _attn_bwd(q, k, v, do, lse, delta.reshape(HEADS, 1, T), tb, f"attn_bwd_{l}",
                                      (srcs, [True] * len(srcs)))
        contrib.update(zip(keys, sent))
        outs = _branch_bwd(proj, tabs, W["bw"], dq, dk, dv, dync, dyns, tmb, f"branch_bwd_{l}")
        dprest, gb = outs[0], dict(zip([n for n, _ in BR_W], outs[1:]))
        grads[l] = _layer_grads(gb, dw_out, dbn_m)
        dx, dnorm_g, hb = _inproj_bwd_dx(xin, W["norm_g"], W["w_in"], dzm, dprest, dy, tmm, f"inproj_bwd_dx_{l}")
        dw_zm, dw_rest, *sent = _inproj_bwd_dw(
            hb, dzm, dprest, tm, f"inproj_bwd_dw_{l}", ([for_chips(n, grads[l][n]) for n in mid], [True] * len(mid)))
        contrib.update(zip([(l, n) for n in mid], sent))
        grads[l]["norm_g"] = dnorm_g[0]
        grads[l]["w_in"] = (dw_zm, dw_rest)
        dy = dx
    grad_x = dy[None]
    g_small = [jnp.stack([grads[l][n] for l in range(DEPTH)]) for n in SMALL]
    last = _chip_exchange([for_chips("w_in", grads[0]["w_in"])] + g_small, [True] + [False] * len(SMALL),
                          "exchange_grads_last")
    contrib[(0, "w_in")] = last[0]
    view = lambda a: a.reshape(-1, a.shape[-1])
    parts = [view(jnp.stack([_sum4(contrib[(l, n)], f"sum_chips_{n}_{l}") for l in range(DEPTH)])) for n in big]
    parts += list(_sum4_whole(last[1:], "sum_chips_small"))
    others = _sibling_exchange(parts, "exchange_cores")

    res = {kind: {} for kind in ("grad", "delta", "new_m", "new_v")}
    for i, n in enumerate(big):
        outs = _adamw(*[view(d[n]) for d in (wts, mom_m, mom_v)], parts[i], others[i], f"adamw_{n}")
        for kind, o in zip(res, outs):
            res[kind][n] = o.reshape(wts[n].shape)
    outs = _adamw_whole(*[[d[n] for n in SMALL] for d in (wts, mom_m, mom_v)], parts[nb:], others[nb:],
                        "adamw_small")
    for kind, o in zip(res, outs):
        res[kind].update(zip(SMALL, o))
    out = [loss, grad_x]
    for kind in ("grad", "delta", "new_m", "new_v"):
        out += [res[kind][n] for n in WEIGHTS]
    return tuple(out)
```

```python
import functools

import jax
import jax.numpy as jnp
from jax import lax
from jax.experimental import pallas as pl
from jax.experimental.pallas import tpu as pltpu

f32 = jnp.float32
bf16 = jnp.bfloat16

DEPTH = 2
D_MODEL = 1024
CONV_W = 256
CONV_K = 31
HEADS = 8
NOPE = 64
ROPE = 32
QK = NOPE + ROPE
VDIM = 64
MLA_W = HEADS * VDIM
Q_LORA = 768
KV_LORA = 256
SG_W = 256
SG_HEADS = 4
SG_CHUNK = 128
IN_COLS = 3104
ROPE_THETA = 10000.0
EPS = 1e-6
HEAD_PAD = 128
LOG2E = 1.4426950408889634
LN2 = 0.6931471805599453
Q_SCALE = QK ** -0.5 * LOG2E
HALO = 32
ROW_ALIGN = 256

ADAM_LR = 0.001
ADAM_B1 = 0.9
ADAM_B2 = 0.999
ADAM_EPS = 1e-08
ADAM_WD = 0.01
ADAM_STEP = 10

PROJ_PAD = 3200
SEG = {
    "zm": (0, 512), "a": (512, 768), "glu": (768, 1024), "zc": (1024, 1280), "cq": (1280, 2048),
    "ckv": (2048, 2304), "kr": (2304, 2432), "us": (2432, 2688), "vs": (2688, 2944), "zs": (2944, 3200),
}
REST = PROJ_PAD - 512

WEIGHTS = ["norm_g", "w_in", "conv_w", "conv_b", "conv_ln_g", "conv_ln_b", "conv_pw_w", "conv_pw_b", "q_norm_g",
           "w_uq", "kv_norm_g", "w_ukv", "qk_q_g", "qk_k_g", "sg_ln_g", "sg_ln_b", "sg_w", "sg_b",
           "branch_norm_g", "w_out"]
BIG = {"w_in": 2, "conv_w": 2, "conv_pw_w": 1, "w_uq": 1, "w_ukv": 2, "w_out": 1}
SMALL = [n for n in WEIGHTS if n not in BIG]

BR_W = [("conv_w", (32, 256)), ("conv_b", (1, 256)), ("cln_g", (1, 256)), ("cln_b", (1, 256)),
        ("pw_w", (256, 256)), ("pw_b", (1, 256)), ("qn_g", (1, 768)), ("w_uq", (768, 1024)),
        ("kvn_g", (1, 256)), ("w_k", (256, 1024)), ("w_v", (256, 1024)), ("qkq_g", (1, 128)),
        ("qkk_g", (1, 128)), ("sln_g", (1, 256)), ("sln_b", (1, 256)), ("sg_w", (4, 128, 128)),
        ("sg_b", (4, 128, 1)), ("bn_c", (1, 256)), ("bn_s", (1, 256))]


@jax.custom_vjp
def _mm(a, w):
    return jnp.dot(a.astype(bf16), w.astype(bf16), preferred_element_type=f32)


def _mm_fwd(a, w):
    return _mm(a, w), (a, w)


def _mm_bwd(res, ct):
    a, w = res
    ctb = ct.astype(bf16)
    da = lax.dot_general(ctb, w.astype(bf16), (((1,), (1,)), ((), ())), preferred_element_type=f32)
    dw = lax.dot_general(a.astype(bf16), ctb, (((0,), (0,)), ((), ())), preferred_element_type=f32)
    return da.astype(a.dtype), dw.astype(w.dtype)


_mm.defvjp(_mm_fwd, _mm_bwd)


@jax.custom_vjp
def _rope(x, c, s1, s2):
    return x * c + pltpu.roll(x, HEAD_PAD - 16, 1) * s1 + pltpu.roll(x, 16, 1) * s2


def _rope_fwd(x, c, s1, s2):
    return _rope(x, c, s1, s2), (c, s1, s2)


def _rope_bwd(res, ct):
    c, s1, s2 = res
    dx = ct * c + pltpu.roll(ct * s1, 16, 1) + pltpu.roll(ct * s2, HEAD_PAD - 16, 1)
    return dx, jnp.zeros_like(c), jnp.zeros_like(s1), jnp.zeros_like(s2)


_rope.defvjp(_rope_fwd, _rope_bwd)


def _rms(x, g):
    return x * lax.rsqrt(jnp.mean(x * x, axis=-1, keepdims=True) + EPS) * g


def _head_rms(x, g):
    return x * lax.rsqrt(jnp.sum(x * x, axis=-1, keepdims=True) * (1.0 / QK) + EPS) * g


def _ln(x, g, b):
    mu = jnp.mean(x, axis=-1, keepdims=True)
    xc = x - mu
    var = jnp.mean(xc * xc, axis=-1, keepdims=True)
    return xc * lax.rsqrt(var + EPS) * g + b


def _branch(p, w, c):
    tm = p["a"].shape[0]
    yg = p["a"] * jax.nn.sigmoid(p["glu"])
    yh = p["ha"] * jax.nn.sigmoid(p["hglu"]) * c["hmask"]
    ycat = jnp.concatenate([yh, yg], axis=0)
    shifted = [ycat] + [ycat[b:b + tm + HALO - 8, :] for b in range(1, 8)]
    acc = jnp.zeros((tm, CONV_W), f32)
    for k in range(CONV_K):
        off = HALO - (CONV_K - 1) + k
        a8 = off - off % 8
        acc = acc + shifted[off % 8][a8:a8 + tm, :] * w["conv_w"][k:k + 1, :]
    yl = jax.nn.silu(_ln(acc + w["conv_b"], w["cln_g"], w["cln_b"]))
    ypw = _mm(yl, w["pw_w"]) + w["pw_b"]
    ync = _rms(ypw * jax.nn.silu(p["zc"]), w["bn_c"])
    cqn = _rms(p["cq"], w["qn_g"])
    qf = _mm(cqn, w["w_uq"])
    ckvn = _rms(p["ckv"], w["kvn_g"])
    kf = _mm(ckvn, w["w_k"])
    v = _mm(ckvn, w["w_v"])
    qs, ks = [], []
    for h in range(HEADS):
        sl = slice(h * HEAD_PAD, (h + 1) * HEAD_PAD)
        qh = _head_rms(qf[:, sl], w["qkq_g"])
        qs.append(_rope(qh, c["rc"], c["rs1"], c["rs2"]) * Q_SCALE)
        kh = _head_rms(kf[:, sl] + p["kr"], w["qkk_g"])
        ks.append(_rope(kh, c["rc"], c["rs1"], c["rs2"]))
    q = jnp.concatenate(qs, axis=-1)
    k = jnp.concatenate(ks, axis=-1)
    u = jax.nn.gelu(p["us"])
    v2 = _ln(jax.nn.gelu(p["vs"]), w["sln_g"], w["sln_b"])
    r = lax.broadcasted_iota(jnp.int32, (SG_CHUNK, SG_CHUNK), 0)
    cc = lax.broadcasted_iota(jnp.int32, (SG_CHUNK, SG_CHUNK), 1)
    lane = lax.broadcasted_iota(jnp.int32, (1, SG_W), 1) // (SG_W // SG_HEADS)
    wm = [jnp.where(r >= cc, w["sg_w"][g], 0.0) for g in range(SG_HEADS)]
    rows = []
    for ci in range(tm // SG_CHUNK):
        vc = v2[ci * SG_CHUNK:(ci + 1) * SG_CHUNK, :]
        mixed = jnp.zeros((SG_CHUNK, SG_W), f32)
        for g in range(SG_HEADS):
            mixed = mixed + jnp.where(lane == g, _mm(wm[g], vc) + w["sg_b"][g], 0.0)
        rows.append(mixed)
    mixed = jnp.concatenate(rows, axis=0) if len(rows) > 1 else rows[0]
    yns = _rms(u * mixed * jax.nn.silu(p["zs"]), w["bn_s"])
    return q, k, v, ync, yns


def _mla_out(o, zm, g):
    return _rms(o * jax.nn.silu(zm), g)


def _load_branch_inputs(proj_ref, halo_ref):
    p = {n: proj_ref[:, SEG[n][0]:SEG[n][1]].astype(f32) for n in SEG if n != "zm"}
    p["ha"] = halo_ref[:, 0:CONV_W].astype(f32)
    p["hglu"] = halo_ref[:, CONV_W:2 * CONV_W].astype(f32)
    return p


def _load_branch_weights(refs):
    w = {}
    for (n, _), r in zip(BR_W, refs):
        if n in ("sg_w", "sg_b"):
            w[n] = [r[g] for g in range(SG_HEADS)]
        else:
            w[n] = r[...]
    return w


def _const_spec(shape):
    nd = len(shape)
    return pl.BlockSpec(shape, lambda *_: (0,) * nd)


def _inproj_fwd(x, g, w, tm, name, ride=None):
    T = x.shape[0]
    nc = 5
    cw = PROJ_PAD // nc

    def body(x_ref, g_ref, w_ref, o_ref, h_ref):
        h = _rms(x_ref[...], g_ref[...]).astype(bf16)
        h_ref[...] = h
        for j in range(nc):
            o_ref[:, j * cw:(j + 1) * cw] = jnp.dot(
                h, w_ref[:, j * cw:(j + 1) * cw], preferred_element_type=f32).astype(bf16)

    return _call_with_rider(
        body, 3, 2, ride, (T // tm,),
        [pl.BlockSpec((tm, D_MODEL), lambda i: (i, 0)), _const_spec((1, D_MODEL)), _const_spec((D_MODEL, PROJ_PAD))],
        [pl.BlockSpec((tm, PROJ_PAD), lambda i: (i, 0)), pl.BlockSpec((tm, D_MODEL), lambda i: (i, 0))],
        [jax.ShapeDtypeStruct((T, PROJ_PAD), bf16), jax.ShapeDtypeStruct((T, D_MODEL), bf16)],
        (x, g, w), name)


def _branch_fwd(proj, tabs, bw, tm, name):
    T = proj.shape[0]
    hb = tm // HALO

    def body(proj_ref, halo_ref, rc_ref, rs1_ref, rs2_ref, *rest):
        wrefs, (q_ref, k_ref, v_ref, ync_ref, yns_ref) = rest[:len(BR_W)], rest[len(BR_W):]
        i = pl.program_id(0)
        c = {"hmask": (i > 0).astype(f32), "rc": rc_ref[...], "rs1": rs1_ref[...], "rs2": rs2_ref[...]}
        q, k, v, ync, yns = _branch(_load_branch_inputs(proj_ref, halo_ref), _load_branch_weights(wrefs), c)
        q_ref[...] = q.astype(bf16)
        k_ref[...] = k.astype(bf16)
        v_ref[...] = v.astype(bf16)
        ync_ref[...] = ync.astype(bf16)
        yns_ref[...] = yns.astype(bf16)

    row = lambda wd: pl.BlockSpec((tm, wd), lambda i: (i, 0))
    wide = HEADS * HEAD_PAD
    return pl.pallas_call(
        body, name=name, grid=(T // tm,),
        in_specs=[row(PROJ_PAD), pl.BlockSpec((HALO, 2 * CONV_W), lambda i: (jnp.maximum(i * hb - 1, 0), 1)),
                  row(HEAD_PAD), row(HEAD_PAD), row(HEAD_PAD)] + [_const_spec(s) for _, s in BR_W],
        out_specs=[row(wide), row(wide), row(wide), row(CONV_W), row(SG_W)],
        out_shape=[jax.ShapeDtypeStruct((T, wide), bf16)] * 3 + [jax.ShapeDtypeStruct((T, CONV_W), bf16)] * 2,
    )(proj, proj, *tabs, *[bw[n] for n, _ in BR_W])


def _attn_fwd(q, k, v, tb, name, ride=None):
    T = q.shape[0]
    nb = T // tb
    pw = 2 * HEAD_PAD

    rc = min(64, tb)

    def body(q_ref, k_ref, v_ref, o_ref, lse_ref, p_scr):
        qi = pl.program_id(1)
        krow = lax.broadcasted_iota(jnp.int32, (rc, tb), 0)
        qcol = lax.broadcasted_iota(jnp.int32, (rc, tb), 1)
        heads = [slice(e * HEAD_PAD, (e + 1) * HEAD_PAD) for e in range(2)]
        qh = [q_ref[:, hs] for hs in heads]

        def step(off, carry, masked):
            s_all = [lax.dot_general(k_ref[pl.ds(off, tb), hs], qh[e], (((1,), (1,)), ((), ())),
                                     preferred_element_type=f32) for e, hs in enumerate(heads)]
            out = []
            for e, hs in enumerate(heads):
                m, l, acc = carry[e]
                blks = []
                m8 = jnp.full((8, tb), -1e30, f32)
                for r in range(0, tb, rc):
                    blk = s_all[e][r:r + rc, :]
                    if masked:
                        blk = jnp.where(krow + r <= qcol, blk, -1e30)
                    blks.append(blk)
                    m8 = jnp.maximum(m8, jnp.max(blk.reshape(rc // 8, 8, tb), axis=0))
                m_new = jnp.maximum(m, jnp.max(m8, axis=0, keepdims=True))
                alpha = jnp.exp2(m - m_new)
                p8 = jnp.zeros((8, tb), f32)
                for r in range(0, tb, rc):
                    p = jnp.exp2(blks[r // rc] - m_new)
                    p8 = p8 + jnp.sum(p.reshape(rc // 8, 8, tb), axis=0)
                    p_scr[e, r:r + rc, :] = p.astype(bf16)
                acc = alpha * acc + lax.dot_general(v_ref[pl.ds(off, tb), hs], p_scr[e], (((0,), (0,)), ((), ())),
                                                    preferred_element_type=f32)
                out.append((m_new, alpha * l + jnp.sum(p8, axis=0, keepdims=True), acc))
            return tuple(out)

        one = (jnp.full((1, tb), -1e30, f32), jnp.zeros((1, tb), f32), jnp.zeros((HEAD_PAD, tb), f32))
        carry = lax.fori_loop(0, qi, lambda ki, cr: step(pl.multiple_of(ki * tb, tb), cr, False), (one, one))
        carry = step(pl.multiple_of(qi * tb, tb), carry, True)
        o_t = jnp.zeros((HEAD_PAD, tb), f32)
        for e in range(2):
            m, l, acc = carry[e]
            o_t = o_t + acc / l
            lse_ref[e] = m + jnp.log2(l)
        o_ref[...] = o_t.T

    return _call_with_rider(
        body, 3, 2, ride, (HEADS // 2, nb),
        [pl.BlockSpec((tb, pw), lambda h, i: (i, h)), pl.BlockSpec((T, pw), lambda h, i: (0, h)),
         pl.BlockSpec((T, pw), lambda h, i: (0, h))],
        [pl.BlockSpec((tb, HEAD_PAD), lambda h, i: (i, h)), pl.BlockSpec((2, 1, tb), lambda h, i: (h, 0, i))],
        [jax.ShapeDtypeStruct((T, MLA_W), f32), jax.ShapeDtypeStruct((HEADS, 1, T), f32)],
        (q, k, v), name, scratch=[pltpu.VMEM((2, tb, tb), bf16)])


def _out_layer(x_ref, o_ref, zm_ref, ync_ref, yns_ref, g_ref, w_ref):
    ynm = _mla_out(o_ref[...], zm_ref[...].astype(f32), g_ref[...]).astype(bf16)
    y = x_ref[...]
    y = y + jnp.dot(ync_ref[...], w_ref[0:CONV_W, :], preferred_element_type=f32)
    y = y + jnp.dot(ynm, w_ref[CONV_W:CONV_W + MLA_W, :], preferred_element_type=f32)
    return y + jnp.dot(yns_ref[...], w_ref[CONV_W + MLA_W:, :], preferred_element_type=f32)


def _out_fwd(x, o, proj, ync, yns, g_m, w_out, tm, name):
    T = x.shape[0]

    def body(*refs):
        refs[-1][...] = _out_layer(*refs[:-1])

    row = lambda wd: pl.BlockSpec((tm, wd), lambda i: (i, 0))
    return pl.pallas_call(
        body, name=name, grid=(T // tm,),
        in_specs=[row(D_MODEL), row(MLA_W), row(MLA_W), row(CONV_W), row(SG_W), _const_spec((1, MLA_W)),
                  _const_spec((D_MODEL, D_MODEL))],
        out_specs=row(D_MODEL),
        out_shape=jax.ShapeDtypeStruct((T, D_MODEL), f32),
    )(x, o, proj, ync, yns, g_m, w_out)


def _out_fwd_loss(x, o, proj, ync, yns, g_m, w_out, tgt, tm, name):
    T = x.shape[0]
    nt = T // tm

    def body(x_ref, o_ref, zm_ref, ync_ref, yns_ref, g_ref, w_ref, t_ref, dy_ref, loss_ref, acc_ref):
        i = pl.program_id(0)

        @pl.when(i == 0)
        def _():
            acc_ref[...] = jnp.zeros_like(acc_ref)

        err = _out_layer(x_ref, o_ref, zm_ref, ync_ref, yns_ref, g_ref, w_ref) - t_ref[...]
        dy_ref[...] = err * (1.0 / D_MODEL)
        acc_ref[...] += jnp.sum(err * err, axis=0, keepdims=True)

        @pl.when(i == nt - 1)
        def _():
            loss_ref[...] = jnp.full((1, HEAD_PAD), 0.5 / D_MODEL, f32) * jnp.sum(acc_ref[...])

    row = lambda wd: pl.BlockSpec((tm, wd), lambda i: (i, 0))
    return pl.pallas_call(
        body, name=name, grid=(nt,),
        in_specs=[row(D_MODEL), row(MLA_W), row(MLA_W), row(CONV_W), row(SG_W), _const_spec((1, MLA_W)),
                  _const_spec((D_MODEL, D_MODEL)), row(D_MODEL)],
        out_specs=[row(D_MODEL), _const_spec((1, HEAD_PAD))],
        out_shape=[jax.ShapeDtypeStruct((T, D_MODEL), f32), jax.ShapeDtypeStruct((1, HEAD_PAD), f32)],
        scratch_shapes=[pltpu.VMEM((1, D_MODEL), f32)],
    )(x, o, proj, ync, yns, g_m, w_out, tgt)


def _out_bwd(dout, o, proj, ync, yns, g_m, w_out, tm, name):
    T = dout.shape[0]

    def body(dy_ref, o_ref, zm_ref, ync_ref, yns_ref, g_ref, w_ref,
             do_ref, dl_ref, dzm_ref, dync_ref, dyns_ref, dw_ref, dg_ref):
        i = pl.program_id(0)

        @pl.when(i == 0)
        def _():
            dw_ref[...] = jnp.zeros_like(dw_ref)
            dg_ref[...] = jnp.zeros_like(dg_ref)

        dyb = dy_ref[...].astype(bf16)
        nt = (((1,), (1,)), ((), ()))
        tn = (((0,), (0,)), ((), ()))
        d_c = lax.dot_general(dyb, w_ref[0:CONV_W, :], nt, preferred_element_type=f32)
        d_m = lax.dot_general(dyb, w_ref[CONV_W:CONV_W + MLA_W, :], nt, preferred_element_type=f32)
        d_s = lax.dot_general(dyb, w_ref[CONV_W + MLA_W:, :], nt, preferred_element_type=f32)
        o = o_ref[...]
        ynm, vjp = jax.vjp(_mla_out, o, zm_ref[...].astype(f32), g_ref[...])
        do, dzm, dg = vjp(d_m)
        do_ref[...] = do.astype(bf16)
        dzm_ref[...] = dzm.astype(bf16)
        dync_ref[...] = d_c.astype(bf16)
        dyns_ref[...] = d_s.astype(bf16)
        dg_ref[...] += dg
        dw_ref[0:CONV_W, :] += lax.dot_general(ync_ref[...], dyb, tn, preferred_element_type=f32)
        dw_ref[CONV_W:CONV_W + MLA_W, :] += lax.dot_general(ynm.astype(bf16), dyb, tn, preferred_element_type=f32)
        dw_ref[CONV_W + MLA_W:, :] += lax.dot_general(yns_ref[...], dyb, tn, preferred_element_type=f32)
        prod = do * o
        hi = prod.astype(bf16)
        lo = (prod - hi.astype(f32)).astype(bf16)
        sel = (lax.broadcasted_iota(jnp.int32, (HEADS, MLA_W), 1) // VDIM
               == lax.broadcasted_iota(jnp.int32, (HEADS, MLA_W), 0)).astype(bf16)
        dl_ref[...] = (lax.dot_general(sel, hi, nt, preferred_element_type=f32)
                       + lax.dot_general(sel, lo, nt, preferred_element_type=f32))

    row = lambda wd: pl.BlockSpec((tm, wd), lambda i: (i, 0))
    return pl.pallas_call(
        body, name=name, grid=(T // tm,),
        in_specs=[row(D_MODEL), row(MLA_W), row(MLA_W), row(CONV_W), row(SG_W), _const_spec((1, MLA_W)),
                  _const_spec((D_MODEL, D_MODEL))],
        out_specs=[row(MLA_W), pl.BlockSpec((HEADS, tm), lambda i: (0, i)), row(MLA_W), row(CONV_W), row(SG_W),
                   _const_spec((D_MODEL, D_MODEL)), _const_spec((1, MLA_W))],
        out_shape=[jax.ShapeDtypeStruct((T, MLA_W), bf16), jax.ShapeDtypeStruct((HEADS, T), f32),
                   jax.ShapeDtypeStruct((T, MLA_W), bf16), jax.ShapeDtypeStruct((T, CONV_W), bf16),
                   jax.ShapeDtypeStruct((T, SG_W), bf16), jax.ShapeDtypeStruct((D_MODEL, D_MODEL), f32),
                   jax.ShapeDtypeStruct((1, MLA_W), f32)],
    )(dout, o, proj, ync, yns, g_m, w_out)


def _attn_bwd(q, k, v, do, lse, delta, tb, name, ride=None):
    T = q.shape[0]
    nb = T // tb
    pw = 2 * HEAD_PAD
    nt = (((1,), (1,)), ((), ()))
    tn = (((0,), (0,)), ((), ()))

    def body(q_ref, do_ref, lse_ref, dl_ref, k_ref, v_ref, dq_ref, dk_ref, dv_ref):
        kj = pl.program_id(1)

        @pl.when(kj == 0)
        def _():
            dq_ref[...] = jnp.zeros_like(dq_ref)

        krow = lax.broadcasted_iota(jnp.int32, (tb, tb), 0)
        qcol = lax.broadcasted_iota(jnp.int32, (tb, tb), 1)
        heads = [slice(e * HEAD_PAD, (e + 1) * HEAD_PAD) for e in range(2)]
        kb = [k_ref[:, hs] for hs in heads]
        vb = [v_ref[:, hs] for hs in heads]

        def step(off, carry, masked):
            dob = do_ref[pl.ds(off, tb), :]
            out = []
            for e, hs in enumerate(heads):
                dk, dv = carry[e]
                qb = q_ref[pl.ds(off, tb), hs]
                s_t = lax.dot_general(kb[e], qb, nt, preferred_element_type=f32)
                p_t = jnp.exp2(s_t - lse_ref[e, :, pl.ds(off, tb)])
                if masked:
                    p_t = jnp.where(krow <= qcol, p_t, 0.0)
                dv = dv + jnp.dot(p_t.astype(bf16), dob, preferred_element_type=f32)
                dp_t = lax.dot_general(vb[e], dob, nt, preferred_element_type=f32)
                ds_t = (p_t * (dp_t - dl_ref[e, :, pl.ds(off, tb)])).astype(bf16)
                dk = dk + jnp.dot(ds_t, qb, preferred_element_type=f32)
                dq_ref[pl.ds(off, tb), hs] += lax.dot_general(ds_t, kb[e], tn, preferred_element_type=f32)
                out.append((dk, dv))
            return tuple(out)

        zero = jnp.zeros((tb, HEAD_PAD), f32)
        carry = step(pl.multiple_of(kj * tb, tb), ((zero, zero), (zero, zero)), True)
        carry = lax.fori_loop(kj + 1, nb, lambda qi, cr: step(pl.multiple_of(qi * tb, tb), cr, False), carry)
        for e, hs in enumerate(heads):
            dk_ref[:, hs] = carry[e][0]
            dv_ref[:, hs] = carry[e][1]

    wide = HEADS * HEAD_PAD
    return _call_with_rider(
        body, 6, 3, ride, (HEADS // 2, nb),
        [pl.BlockSpec((T, pw), lambda h, j: (0, h)), pl.BlockSpec((T, HEAD_PAD), lambda h, j: (0, h)),
         pl.BlockSpec((2, 1, T), lambda h, j: (h, 0, 0)), pl.BlockSpec((2, 1, T), lambda h, j: (h, 0, 0)),
         pl.BlockSpec((tb, pw), lambda h, j: (j, h)), pl.BlockSpec((tb, pw), lambda h, j: (j, h))],
        [pl.BlockSpec((T, pw), lambda h, j: (0, h)), pl.BlockSpec((tb, pw), lambda h, j: (j, h)),
         pl.BlockSpec((tb, pw), lambda h, j: (j, h))],
        [jax.ShapeDtypeStruct((T, wide), f32)] * 3,
        (q, do, lse, delta, k, v), name)


def _branch_bwd(proj, tabs, bw, dq, dk, dv, dync, dyns, tm, name):
    T = proj.shape[0]
    nt = T // tm
    hb = tm // HALO
    nw = len(BR_W)

    def body(proj_ref, halo_ref, rc_ref, rs1_ref, rs2_ref, *rest):
        wrefs = rest[:nw]
        dq_ref, dk_ref, dv_ref, dync_ref, dyns_ref = rest[nw:nw + 5]
        dp_ref = rest[nw + 5]
        gwrefs = rest[nw + 6:2 * nw + 6]
        carry_ref = rest[2 * nw + 6]
        i = pl.program_id(0)
        r = nt - 1 - i

        @pl.when(i == 0)
        def _():
            carry_ref[...] = jnp.zeros_like(carry_ref)
            for g in gwrefs:
                g[...] = jnp.zeros_like(g)

        c = {"hmask": (r > 0).astype(f32), "rc": rc_ref[...], "rs1": rs1_ref[...], "rs2": rs2_ref[...]}
        _, vjp = jax.vjp(lambda p_, w_: _branch(p_, w_, c), _load_branch_inputs(proj_ref, halo_ref),
                         _load_branch_weights(wrefs))
        cts = (dq_ref[...] * LN2, dk_ref[...] * LN2, dv_ref[...], dync_ref[...].astype(f32),
               dyns_ref[...].astype(f32))
        dp, dw = vjp(cts)
        for n in SEG:
            if n in ("zm", "a", "glu"):
                continue
            dp_ref[:, SEG[n][0] - 512:SEG[n][1] - 512] = dp[n].astype(bf16)
        for n, hn, lo in (("a", "ha", 0), ("glu", "hglu", CONV_W)):
            d = dp[n]
            tail = d[tm - HALO:, :] + carry_ref[:, lo:lo + CONV_W]
            s0 = SEG[n][0] - 512
            dp_ref[0:tm - HALO, s0:s0 + CONV_W] = d[:tm - HALO, :].astype(bf16)
            dp_ref[tm - HALO:tm, s0:s0 + CONV_W] = tail.astype(bf16)
        carry_ref[:, 0:CONV_W] = dp["ha"]
        carry_ref[:, CONV_W:] = dp["hglu"]
        for (n, _), g in zip(BR_W, gwrefs):
            if n in ("sg_w", "sg_b"):
                for gi in range(SG_HEADS):
                    g[gi] += dw[n][gi]
            else:
                g[...] += dw[n]

    row = lambda wd: pl.BlockSpec((tm, wd), lambda i: (nt - 1 - i, 0))
    wide = HEADS * HEAD_PAD
    return pl.pallas_call(
        body, name=name, grid=(nt,),
        in_specs=[row(PROJ_PAD),
                  pl.BlockSpec((HALO, 2 * CONV_W), lambda i: (jnp.maximum((nt - 1 - i) * hb - 1, 0), 1)),
                  row(HEAD_PAD), row(HEAD_PAD), row(HEAD_PAD)] + [_const_spec(s) for _, s in BR_W]
                 + [row(wide), row(wide), row(wide), row(CONV_W), row(SG_W)],
        out_specs=[row(REST)] + [_const_spec(s) for _, s in BR_W],
        out_shape=[jax.ShapeDtypeStruct((T, REST), bf16)] + [jax.ShapeDtypeStruct(s, f32) for _, s in BR_W],
        scratch_shapes=[pltpu.VMEM((HALO, 2 * CONV_W), f32)],
    )(proj, proj, *tabs, *[bw[n] for n, _ in BR_W], dq, dk, dv, dync, dyns)


def _inproj_bwd_dx(x, g, w, dzm, dprest, dout, tm, name, ride=None):
    T = x.shape[0]
    nt_dims = (((1,), (1,)), ((), ()))

    def body(x_ref, g_ref, w_ref, dzm_ref, dpr_ref, dout_ref, dx_ref, dg_ref):
        i = pl.program_id(0)

        @pl.when(i == 0)
        def _():
            dg_ref[...] = jnp.zeros_like(dg_ref)

        dh = lax.dot_general(dzm_ref[...], w_ref[:, 0:512], nt_dims, preferred_element_type=f32)
        dh = dh + lax.dot_general(dpr_ref[...], w_ref[:, 512:], nt_dims, preferred_element_type=f32)
        _, vjp = jax.vjp(_rms, x_ref[...], g_ref[...])
        dx, dg = vjp(dh)
        dx_ref[...] = dout_ref[...] + dx
        dg_ref[...] += dg

    row = lambda wd: pl.BlockSpec((tm, wd), lambda i: (i, 0))
    return _call_with_rider(
        body, 6, 2, ride, (T // tm,),
        [row(D_MODEL), _const_spec((1, D_MODEL)), _const_spec((D_MODEL, PROJ_PAD)), row(512), row(REST),
         row(D_MODEL)],
        [row(D_MODEL), _const_spec((1, D_MODEL))],
        [jax.ShapeDtypeStruct((T, D_MODEL), f32), jax.ShapeDtypeStruct((1, D_MODEL), f32)],
        (x, g, w, dzm, dprest, dout), name)


def _inproj_bwd_dw(h, dzm, dprest, tm, name, ride=None):
    T = h.shape[0]
    tm = min(4 * tm, T)
    nt = T // tm
    cw = REST // 3
    tn = (((0,), (0,)), ((), ()))

    def make_body():
        def body(h_ref, d_ref, o_ref):
            i = pl.program_id(1)

            @pl.when(i == 0)
            def _():
                o_ref[...] = jnp.zeros_like(o_ref)

            o_ref[...] += lax.dot_general(h_ref[...], d_ref[...], tn, preferred_element_type=f32)
        return body

    def run(d, width, cwid, nm, rd):
        return _call_with_rider(
            make_body(), 2, 1, rd, (width // cwid, nt),
            [pl.BlockSpec((tm, D_MODEL), lambda j, i: (i, 0)), pl.BlockSpec((tm, cwid), lambda j, i: (i, j))],
            [pl.BlockSpec((D_MODEL, cwid), lambda j, i: (0, j))], [jax.ShapeDtypeStruct((D_MODEL, width), f32)],
            (h, d), nm)

    (zm,) = run(dzm, 512, 512, name + "_zm", None)
    rest, *sent = run(dprest, REST, cw, name + "_rest", ride)
    return (zm, rest, *sent)


def _row_tile(rows):
    return ROW_ALIGN if rows > ROW_ALIGN and rows % ROW_ALIGN == 0 else rows


def _sum4(buf, name):
    _, R, C = buf.shape
    tr = _row_tile(R)

    def body(b_ref, o_ref):
        b = [b_ref[s].astype(f32) for s in range(4)]
        o_ref[...] = ((b[0] + b[1]) + b[2]) + b[3]

    return pl.pallas_call(
        body, name=name, grid=(R // tr,),
        in_specs=[pl.BlockSpec((4, tr, C), lambda i: (0, i, 0))],
        out_specs=pl.BlockSpec((tr, C), lambda i: (i, 0)),
        out_shape=jax.ShapeDtypeStruct((R, C), f32),
    )(buf)


def _sum4_whole(bufs, name):
    n = len(bufs)

    def body(*refs):
        for b_ref, o_ref in zip(refs[:n], refs[n:]):
            o_ref[...] = ((b_ref[0] + b_ref[1]) + b_ref[2]) + b_ref[3]

    return pl.pallas_call(
        body, name=name, out_shape=[jax.ShapeDtypeStruct(b.shape[1:], f32) for b in bufs])(*bufs)


def _adamw_math(w, m, v, g):
    m_new = ADAM_B1 * m + (1.0 - ADAM_B1) * g
    v_new = ADAM_B2 * v + (1.0 - ADAM_B2) * (g * g)
    m_hat = m_new / (1.0 - ADAM_B1 ** ADAM_STEP)
    v_hat = v_new / (1.0 - ADAM_B2 ** ADAM_STEP)
    return -ADAM_LR * (m_hat / (jnp.sqrt(v_hat) + ADAM_EPS) + ADAM_WD * w), m_new, v_new


def _adamw(w, m, v, p_a, p_b, name):
    R, C = w.shape
    tr = _row_tile(R)

    def body(w_ref, m_ref, v_ref, a_ref, b_ref, g_ref, d_ref, nm_ref, nv_ref):
        g = a_ref[...] + b_ref[...]
        g_ref[...] = g
        d_ref[...], nm_ref[...], nv_ref[...] = _adamw_math(w_ref[...], m_ref[...], v_ref[...], g)

    spec = pl.BlockSpec((tr, C), lambda i: (i, 0))
    return pl.pallas_call(
        body, name=name, grid=(R // tr,), in_specs=[spec] * 5, out_specs=[spec] * 4,
        out_shape=[jax.ShapeDtypeStruct((R, C), f32)] * 4,
    )(w, m, v, p_a, p_b)


def _adamw_whole(ws, ms, vs, p_a, p_b, name):
    n = len(ws)

    def body(*refs):
        ins, outs = refs[:5 * n], refs[5 * n:]
        for i in range(n):
            w, m, v, a, b = (ins[k * n + i][...] for k in range(5))
            g = a + b
            outs[i][...] = g
            outs[n + i][...], outs[2 * n + i][...], outs[3 * n + i][...] = _adamw_math(w, m, v, g)

    res = pl.pallas_call(
        body, name=name, out_shape=[jax.ShapeDtypeStruct(w.shape, f32) for w in ws] * 4,
    )(*ws, *ms, *vs, *p_a, *p_b)
    return [res[k * n:(k + 1) * n] for k in range(4)]


class _ChipExchange:
    def __init__(self, srcs, per_target):
        self.n = len(srcs)
        self.per_target = per_target
        blocks = [s.shape[1:] if p else s.shape for s, p in zip(srcs, per_target)]
        self.out_shape = [jax.ShapeDtypeStruct((4,) + tuple(b), s.dtype) for b, s in zip(blocks, srcs)]
        self.scratch_shapes = [pltpu.SemaphoreType.DMA((3 * self.n,)), pltpu.SemaphoreType.DMA((3 * self.n,)),
                               pltpu.SemaphoreType.DMA((self.n,))]
        self.any_specs = [pl.BlockSpec(memory_space=pl.ANY)] * self.n

    def _copies(self, src_refs, out_refs, sems):
        send_sems, recv_sems, local_sems = sems
        x, y, c = lax.axis_index("x"), lax.axis_index("y"), lax.axis_index("c")
        me = 2 * x + y
        chips = [(1 - x, y), (x, 1 - y), (1 - x, 1 - y)]

        def block_for(i, t):
            return src_refs[i].at[t] if self.per_target[i] else src_refs[i]

        def remote(i, k, block_t, slot):
            tx, ty = chips[k]
            return pltpu.make_async_remote_copy(
                src_ref=block_for(i, block_t), dst_ref=out_refs[i].at[slot], send_sem=send_sems.at[3 * i + k],
                recv_sem=recv_sems.at[3 * i + k], device_id=(tx, ty, c), device_id_type=pl.DeviceIdType.MESH)

        pairs = [(i, k) for i in range(self.n) for k in range(3)]
        sends = [remote(i, k, 2 * chips[k][0] + chips[k][1], me) for i, k in pairs]
        recvs = [remote(i, k, me, 2 * chips[k][0] + chips[k][1]) for i, k in pairs]
        mine = [pltpu.make_async_copy(block_for(i, me), out_refs[i].at[me], local_sems.at[i])
                for i in range(self.n)]
        return sends, recvs, mine

    def start(self, src_refs, out_refs, sems):
        sends, _, mine = self._copies(src_refs, out_refs, sems)
        for cp in sends + mine:
            cp.start()

    def wait(self, src_refs, out_refs, sems):
        sends, recvs, mine = self._copies(src_refs, out_refs, sems)
        for cp in recvs:
            cp.wait_recv()
        for cp in sends:
            cp.wait_send()
        for cp in mine:
            cp.wait()


def _call_with_rider(body, n_in, n_out, ride, grid, in_specs, out_specs, out_shape, args, name, scratch=()):
    scratch = list(scratch)
    if ride is None:
        return pl.pallas_call(body, name=name, grid=grid, in_specs=in_specs, out_specs=out_specs,
                              out_shape=out_shape, scratch_shapes=scratch)(*args)
    ex = _ChipExchange(*ride)
    ne = ex.n

    def wrapped(*refs):
        ins, src_refs = refs[:n_in], refs[n_in:n_in + ne]
        outs, ex_outs = refs[n_in + ne:n_in + ne + n_out], refs[n_in + ne + n_out:n_in + 2 * ne + n_out]
        own = refs[n_in + 2 * ne + n_out:n_in + 2 * ne + n_out + len(scratch)]
        sems = refs[n_in + 2 * ne + n_out + len(scratch):]
        ids = [pl.program_id(a) for a in range(len(grid))]
        first = functools.reduce(jnp.logical_and, [i == 0 for i in ids])
        last = functools.reduce(jnp.logical_and, [i == g - 1 for i, g in zip(ids, grid)])

        @pl.when(first)
        def _():
            ex.start(src_refs, ex_outs, sems)

        body(*ins, *outs, *own)

        @pl.when(last)
        def _():
            ex.wait(src_refs, ex_outs, sems)

    return pl.pallas_call(
        wrapped, name=name, grid=grid, in_specs=list(in_specs) + ex.any_specs,
        out_specs=list(out_specs) + ex.any_specs, out_shape=list(out_shape) + ex.out_shape,
        scratch_shapes=scratch + ex.scratch_shapes,
    )(*args, *ride[0])


def _chip_exchange(srcs, per_target, name):
    ex = _ChipExchange(srcs, per_target)
    n = ex.n

    def body(*refs):
        src_refs, out_refs, sems = refs[:n], refs[n:2 * n], refs[2 * n:]
        ex.start(src_refs, out_refs, sems)
        ex.wait(src_refs, out_refs, sems)

    return pl.pallas_call(
        body, name=name, in_specs=ex.any_specs, out_specs=ex.any_specs, out_shape=ex.out_shape,
        scratch_shapes=ex.scratch_shapes,
    )(*srcs)


def _sibling_exchange(srcs, name):
    n = len(srcs)

    def body(*refs):
        src_refs, out_refs = refs[:n], refs[n:2 * n]
        send_sems, recv_sems = refs[2 * n:]
        x, y, c = lax.axis_index("x"), lax.axis_index("y"), lax.axis_index("c")
        copies = [pltpu.make_async_remote_copy(
            src_ref=src_refs[i], dst_ref=out_refs[i], send_sem=send_sems.at[i], recv_sem=recv_sems.at[i],
            device_id=(x, y, 1 - c), device_id_type=pl.DeviceIdType.MESH) for i in range(n)]
        for cp in copies:
            cp.start()
        for cp in copies:
            cp.wait()

    return pl.pallas_call(
        body, name=name,
        in_specs=[pl.BlockSpec(memory_space=pl.ANY)] * n, out_specs=[pl.BlockSpec(memory_space=pl.ANY)] * n,
        out_shape=[jax.ShapeDtypeStruct(s.shape, s.dtype) for s in srcs],
        scratch_shapes=[pltpu.SemaphoreType.DMA((n,)), pltpu.SemaphoreType.DMA((n,))],
    )(*srcs)


W_IN_SHARD = IN_COLS // 4
W_IN_MAP = [(0, 1792, 512), (1792, 1824, 2368), (1824, 2336, 0), (2336, IN_COLS, 2432)]


def _pad_w_in(shards):
    def cols(a, b):
        out = []
        while a < b:
            j = a // W_IN_SHARD
            e = min(b, (j + 1) * W_IN_SHARD)
            out.append(shards[j][:, a - j * W_IN_SHARD:e - j * W_IN_SHARD])
            a = e
        return out

    z = lambda n: [jnp.zeros((D_MODEL, n), shards.dtype)]
    return jnp.concatenate(cols(1824, 2336) + cols(0, 1792) + z(64) + cols(1792, 1824) + z(32) + cols(2336, IN_COLS),
                           axis=1)


def _w_in_grad_shard(dw_zm, dw_rest, j):
    lo, hi = j * W_IN_SHARD, (j + 1) * W_IN_SHARD
    out = []
    for a, b, p in W_IN_MAP:
        a2, b2 = max(a, lo), min(b, hi)
        if a2 < b2:
            p0 = p + a2 - a
            src, off = (dw_zm, 0) if p0 < 512 else (dw_rest, 512)
            out.append(src[:, p0 - off:p0 - off + b2 - a2])
    return jnp.concatenate(out, axis=1)


def _v_cols():
    return [(h % 2) * VDIM for h in range(HEADS)]


def _layer_weights(gathered, small, l):
    full = {n: jnp.concatenate([g[j] for j in range(4)], axis=BIG[n] - 1) for n, g in gathered.items()}
    w_uq = full["w_uq"].astype(f32).reshape(Q_LORA, HEADS, QK)
    w_uq = jnp.pad(w_uq, ((0, 0), (0, 0), (0, HEAD_PAD - QK))).reshape(Q_LORA, HEADS * HEAD_PAD)
    ukv = full["w_ukv"].astype(f32).reshape(KV_LORA, HEADS, NOPE + VDIM)
    w_k = jnp.pad(ukv[:, :, :NOPE], ((0, 0), (0, 0), (0, HEAD_PAD - NOPE))).reshape(KV_LORA, HEADS * HEAD_PAD)
    zeros = jnp.zeros((KV_LORA, VDIM), f32)
    w_v = jnp.concatenate(
        [jnp.concatenate([ukv[:, h, NOPE:], zeros] if h % 2 == 0 else [zeros, ukv[:, h, NOPE:]], axis=1)
         for h in range(HEADS)], axis=1)
    row = lambda a: a.reshape(1, -1)
    bn = small["branch_norm_g"][l]
    bw = {
        "conv_w": jnp.pad(full["conv_w"], ((0, 1), (0, 0))), "conv_b": row(small["conv_b"][l]),
        "cln_g": row(small["conv_ln_g"][l]), "cln_b": row(small["conv_ln_b"][l]),
        "pw_w": full["conv_pw_w"].astype(f32), "pw_b": row(small["conv_pw_b"][l]),
        "qn_g": row(small["q_norm_g"][l]), "w_uq": w_uq, "kvn_g": row(small["kv_norm_g"][l]),
        "w_k": w_k, "w_v": w_v,
        "qkq_g": jnp.pad(row(small["qk_q_g"][l]), ((0, 0), (0, HEAD_PAD - QK))),
        "qkk_g": jnp.pad(row(small["qk_k_g"][l]), ((0, 0), (0, HEAD_PAD - QK))),
        "sln_g": row(small["sg_ln_g"][l]), "sln_b": row(small["sg_ln_b"][l]),
        "sg_w": small["sg_w"][l], "sg_b": small["sg_b"][l].reshape(SG_HEADS, SG_CHUNK, 1),
        "bn_c": row(bn[:CONV_W]), "bn_s": row(bn[CONV_W + MLA_W:]),
    }
    return {"bw": bw, "bn_m": row(bn[CONV_W:CONV_W + MLA_W]), "w_out": full["w_out"]}


def _layer_grads(gb, dw_out, dbn_m):
    duq = gb["w_uq"].reshape(Q_LORA, HEADS, HEAD_PAD)[:, :, :QK].reshape(Q_LORA, HEADS * QK)
    dk = gb["w_k"].reshape(KV_LORA, HEADS, HEAD_PAD)[:, :, :NOPE]
    dv = gb["w_v"].reshape(KV_LORA, HEADS, HEAD_PAD)
    dv = jnp.stack([dv[:, h, c0:c0 + VDIM] for h, c0 in enumerate(_v_cols())], axis=1)
    dukv = jnp.concatenate([dk, dv], axis=2).reshape(KV_LORA, HEADS * (NOPE + VDIM))
    return {
        "conv_w": gb["conv_w"][:CONV_K],
        "conv_b": gb["conv_b"][0], "conv_ln_g": gb["cln_g"][0], "conv_ln_b": gb["cln_b"][0],
        "conv_pw_w": gb["pw_w"], "conv_pw_b": gb["pw_b"][0], "q_norm_g": gb["qn_g"][0], "w_uq": duq,
        "kv_norm_g": gb["kvn_g"][0], "w_ukv": dukv, "qk_q_g": gb["qkq_g"][0, :QK], "qk_k_g": gb["qkk_g"][0, :QK],
        "sg_ln_g": gb["sln_g"][0], "sg_ln_b": gb["sln_b"][0], "sg_w": gb["sg_w"], "sg_b": gb["sg_b"][:, :, 0],
        "branch_norm_g": jnp.concatenate([gb["bn_c"][0], dbn_m[0], gb["bn_s"][0]]), "w_out": dw_out,
    }


def _rope_tables(T):
    half = ROPE // 2
    inv_freq = ROPE_THETA ** (-jnp.arange(half, dtype=f32) / half)
    ang = jnp.arange(T, dtype=f32)[:, None] * inv_freq[None, :]
    cos, sin = jnp.cos(ang), jnp.sin(ang)
    one = jnp.ones((T, NOPE), f32)
    z = lambda n: jnp.zeros((T, n), f32)
    rc = jnp.concatenate([one, cos, cos, jnp.ones((T, HEAD_PAD - QK), f32)], axis=1)
    rs1 = jnp.concatenate([z(NOPE), -sin, z(half), z(HEAD_PAD - QK)], axis=1)
    rs2 = jnp.concatenate([z(NOPE), z(half), sin, z(HEAD_PAD - QK)], axis=1)
    return rc, rs1, rs2


def kernel(x, norm_g, w_in, conv_w, conv_b, conv_ln_g, conv_ln_b, conv_pw_w, conv_pw_b, q_norm_g, w_uq, kv_norm_g, w_ukv, qk_q_g, qk_k_g, sg_ln_g, sg_ln_b, sg_w, sg_b, branch_norm_g, w_out, loss_target, m_norm_g, m_w_in, m_conv_w, m_conv_b, m_conv_ln_g, m_conv_ln_b, m_conv_pw_w, m_conv_pw_b, m_q_norm_g, m_w_uq, m_kv_norm_g, m_w_ukv, m_qk_q_g, m_qk_k_g, m_sg_ln_g, m_sg_ln_b, m_sg_w, m_sg_b, m_branch_norm_g, m_w_out, v_norm_g, v_w_in, v_conv_w, v_conv_b, v_conv_ln_g, v_conv_ln_b, v_conv_pw_w, v_conv_pw_b, v_q_norm_g, v_w_uq, v_kv_norm_g, v_w_ukv, v_qk_q_g, v_qk_k_g, v_sg_ln_g, v_sg_ln_b, v_sg_w, v_sg_b, v_branch_norm_g, v_w_out):
    wts = dict(zip(WEIGHTS, [norm_g, w_in, conv_w, conv_b, conv_ln_g, conv_ln_b, conv_pw_w, conv_pw_b, q_norm_g,
                             w_uq, kv_norm_g, w_ukv, qk_q_g, qk_k_g, sg_ln_g, sg_ln_b, sg_w, sg_b, branch_norm_g,
                             w_out]))
    mom_m = dict(zip(WEIGHTS, [m_norm_g, m_w_in, m_conv_w, m_conv_b, m_conv_ln_g, m_conv_ln_b, m_conv_pw_w,
                               m_conv_pw_b, m_q_norm_g, m_w_uq, m_kv_norm_g, m_w_ukv, m_qk_q_g, m_qk_k_g,
                               m_sg_ln_g, m_sg_ln_b, m_sg_w, m_sg_b, m_branch_norm_g, m_w_out]))
    mom_v = dict(zip(WEIGHTS, [v_norm_g, v_w_in, v_conv_w, v_conv_b, v_conv_ln_g, v_conv_ln_b, v_conv_pw_w,
                               v_conv_pw_b, v_q_norm_g, v_w_uq, v_kv_norm_g, v_w_ukv, v_qk_q_g, v_qk_k_g,
                               v_sg_ln_g, v_sg_ln_b, v_sg_w, v_sg_b, v_branch_norm_g, v_w_out]))
    xs = x[0]
    tgt = loss_target[0]
    T = xs.shape[0]
    tm = min(256, T)
    tmb = min(128, T)
    tmm = min(512, T)
    tb = min(512, T // 2)

    big = list(BIG)
    rest = [n for n in big if n != "w_in"]
    nb = len(big)
    shards = [{n: wts[n][l] if n == "conv_w" else wts[n][l].astype(bf16) for n in big} for l in range(DEPTH)]
    small = {n: wts[n] for n in SMALL}
    tabs = _rope_tables(T)
    gathered = dict(zip(["w_in"], _chip_exchange([shards[0]["w_in"]], [False], "gather_w_in_0")))

    acts, lw = [], []
    h_in = xs
    for l in range(DEPTH):
        W = {"norm_g": small["norm_g"][l].reshape(1, -1), "w_in": _pad_w_in(gathered["w_in"])}
        ride = ([shards[l][n] for n in rest], [False] * len(rest)) if l == 0 else None
        proj, hb, *late = _inproj_fwd(h_in, W["norm_g"], W["w_in"], tmm, f"inproj_fwd_{l}", ride)
        if ride is not None:
            gathered.update(zip(rest, late))
        W.update(_layer_weights({n: gathered[n] for n in rest}, small, l))
        lw.append(W)
        q, k, v, ync, yns = _branch_fwd(proj, tabs, W["bw"], tm, f"branch_fwd_{l}")
        ride = ([shards[l + 1][n] for n in big], [False] * nb) if l + 1 < DEPTH else None
        o, lse, *ahead = _attn_fwd(q, k, v, tb, f"attn_fwd_{l}", ride)
        gathered = dict(zip(big, ahead))
        acts.append((h_in, hb, proj, q, k, v, ync, yns, o, lse))
        if l + 1 < DEPTH:
            h_in = _out_fwd(h_in, o, proj, ync, yns, W["bn_m"], W["w_out"], tmm, f"out_fwd_{l}")
        else:
            dy, loss_row = _out_fwd_loss(h_in, o, proj, ync, yns, W["bn_m"], W["w_out"], tgt, tmm, f"out_fwd_{l}")
    loss = lax.psum(loss_row[0, 0], ("x", "y", "c"))

    def for_chips(n, g):
        if n == "w_in":
            blocks = [_w_in_grad_shard(*g, j) for j in range(4)]
        else:
            blocks = jnp.split(g, 4, axis=BIG[n] - 1)
        return jnp.stack(blocks).astype(bf16)

    mid = [n for n in rest if n != "w_out"]
    grads = [None] * DEPTH
    contrib = {}
    for l in reversed(range(DEPTH)):
        W = lw[l]
        xin, hb, proj, q, k, v, ync, yns, o, lse = acts[l]
        do, delta, dzm, dync, dyns, dw_out, dbn_m = _out_bwd(
            dy, o, proj, ync, yns, W["bn_m"], W["w_out"], tmm, f"out_bwd_{l}")
        keys = [(l, "w_out")] + ([(l + 1, "w_in")] if l + 1 < DEPTH else [])
        srcs = [for_chips("w_out", dw_out)] + ([for_chips("w_in", grads[l + 1]["w_in"])] if l + 1 < DEPTH else [])
        dq, dk, dv, *sent = _attn_bwd(q, k, v, do, lse, delta.reshape(HEADS, 1, T), tb, f"attn_bwd_{l}",
                                      (srcs, [True] * len(srcs)))
        contrib.update(zip(keys, sent))
        outs = _branch_bwd(proj, tabs, W["bw"], dq, dk, dv, dync, dyns, tmb, f"branch_bwd_{l}")
        dprest, gb = outs[0], dict(zip([n for n, _ in BR_W], outs[1:]))
        grads[l] = _layer_grads(gb, dw_out, dbn_m)
        dw_zm, dw_rest, *sent = _inproj_bwd_dw(
            hb, dzm, dprest, tm, f"inproj_bwd_dw_{l}", ([for_chips(n, grads[l][n]) for n in mid], [True] * len(mid)))
        contrib.update(zip([(l, n) for n in mid], sent))
        grads[l]["w_in"] = (dw_zm, dw_rest)
        ride = ([for_chips("w_in", grads[l]["w_in"])], [True]) if l == 0 else None
        dy, dnorm_g, *sent = _inproj_bwd_dx(xin, W["norm_g"], W["w_in"], dzm, dprest, dy, tmm,
                                            f"inproj_bwd_dx_{l}", ride)
        contrib.update(zip([(l, "w_in")], sent))
        grads[l]["norm_g"] = dnorm_g[0]
    grad_x = dy[None]
    g_small = [jnp.stack([grads[l][n] for l in range(DEPTH)]) for n in SMALL]
    last = _chip_exchange(g_small, [False] * len(SMALL), "exchange_grads_last")
    view = lambda a: a.reshape(-1, a.shape[-1])
    parts = [view(jnp.stack([_sum4(contrib[(l, n)], f"sum_chips_{n}_{l}") for l in range(DEPTH)])) for n in big]
    parts += list(_sum4_whole(last, "sum_chips_small"))
    others = _sibling_exchange(parts, "exchange_cores")

    res = {kind: {} for kind in ("grad", "delta", "new_m", "new_v")}
    for i, n in enumerate(big):
        outs = _adamw(*[view(d[n]) for d in (wts, mom_m, mom_v)], parts[i], others[i], f"adamw_{n}")
        for kind, o in zip(res, outs):
            res[kind][n] = o.reshape(wts[n].shape)
    outs = _adamw_whole(*[[d[n] for n in SMALL] for d in (wts, mom_m, mom_v)], parts[nb:], others[nb:],
                        "adamw_small")
    for kind, o in zip(res, outs):
        res[kind].update(zip(SMALL, o))
    out = [loss, grad_x]
    for kind in ("grad", "delta", "new_m", "new_v"):
        out += [res[kind][n] for n in WEIGHTS]
    return tuple(out)
```

```python
import functools

import jax
import jax.numpy as jnp
from jax import lax
from jax.experimental import pallas as pl
from jax.experimental.pallas import tpu as pltpu

f32 = jnp.float32
bf16 = jnp.bfloat16

DEPTH = 2
D_MODEL = 1024
CONV_W = 256
CONV_K = 31
HEADS = 8
NOPE = 64
ROPE = 32
QK = NOPE + ROPE
VDIM = 64
MLA_W = HEADS * VDIM
Q_LORA = 768
KV_LORA = 256
SG_W = 256
SG_HEADS = 4
SG_CHUNK = 128
IN_COLS = 3104
ROPE_THETA = 10000.0
EPS = 1e-6
HEAD_PAD = 128
LOG2E = 1.4426950408889634
LN2 = 0.6931471805599453
Q_SCALE = QK ** -0.5 * LOG2E
HALO = 32
ROW_ALIGN = 256

ADAM_LR = 0.001
ADAM_B1 = 0.9
ADAM_B2 = 0.999
ADAM_EPS = 1e-08
ADAM_WD = 0.01
ADAM_STEP = 10

PROJ_PAD = 3200
SEG = {
    "zm": (0, 512), "a": (512, 768), "glu": (768, 1024), "zc": (1024, 1280), "cq": (1280, 2048),
    "ckv": (2048, 2304), "kr": (2304, 2432), "us": (2432, 2688), "vs": (2688, 2944), "zs": (2944, 3200),
}
REST = PROJ_PAD - 512

WEIGHTS = ["norm_g", "w_in", "conv_w", "conv_b", "conv_ln_g", "conv_ln_b", "conv_pw_w", "conv_pw_b", "q_norm_g",
           "w_uq", "kv_norm_g", "w_ukv", "qk_q_g", "qk_k_g", "sg_ln_g", "sg_ln_b", "sg_w", "sg_b",
           "branch_norm_g", "w_out"]
BIG = {"w_in": 2, "conv_w": 2, "conv_pw_w": 1, "w_uq": 1, "w_ukv": 2, "w_out": 1}
SMALL = [n for n in WEIGHTS if n not in BIG]

BR_W = [("conv_w", (32, 256)), ("conv_b", (1, 256)), ("cln_g", (1, 256)), ("cln_b", (1, 256)),
        ("pw_w", (256, 256)), ("pw_b", (1, 256)), ("qn_g", (1, 768)), ("w_uq", (768, 1024)),
        ("kvn_g", (1, 256)), ("w_k", (256, 1024)), ("w_v", (256, 1024)), ("qkq_g", (1, 128)),
        ("qkk_g", (1, 128)), ("sln_g", (1, 256)), ("sln_b", (1, 256)), ("sg_w", (4, 128, 128)),
        ("sg_b", (4, 128, 1)), ("bn_c", (1, 256)), ("bn_s", (1, 256))]


@jax.custom_vjp
def _mm(a, w):
    return jnp.dot(a.astype(bf16), w.astype(bf16), preferred_element_type=f32)


def _mm_fwd(a, w):
    return _mm(a, w), (a, w)


def _mm_bwd(res, ct):
    a, w = res
    ctb = ct.astype(bf16)
    da = lax.dot_general(ctb, w.astype(bf16), (((1,), (1,)), ((), ())), preferred_element_type=f32)
    dw = lax.dot_general(a.astype(bf16), ctb, (((0,), (0,)), ((), ())), preferred_element_type=f32)
    return da.astype(a.dtype), dw.astype(w.dtype)


_mm.defvjp(_mm_fwd, _mm_bwd)


@jax.custom_vjp
def _rope(x, c, s1, s2):
    return x * c + pltpu.roll(x, HEAD_PAD - 16, 1) * s1 + pltpu.roll(x, 16, 1) * s2


def _rope_fwd(x, c, s1, s2):
    return _rope(x, c, s1, s2), (c, s1, s2)


def _rope_bwd(res, ct):
    c, s1, s2 = res
    dx = ct * c + pltpu.roll(ct * s1, 16, 1) + pltpu.roll(ct * s2, HEAD_PAD - 16, 1)
    return dx, jnp.zeros_like(c), jnp.zeros_like(s1), jnp.zeros_like(s2)


_rope.defvjp(_rope_fwd, _rope_bwd)


def _rms(x, g):
    return x * lax.rsqrt(jnp.mean(x * x, axis=-1, keepdims=True) + EPS) * g


def _head_rms(x, g):
    return x * lax.rsqrt(jnp.sum(x * x, axis=-1, keepdims=True) * (1.0 / QK) + EPS) * g


def _ln(x, g, b):
    mu = jnp.mean(x, axis=-1, keepdims=True)
    xc = x - mu
    var = jnp.mean(xc * xc, axis=-1, keepdims=True)
    return xc * lax.rsqrt(var + EPS) * g + b


def _branch(p, w, c):
    tm = p["a"].shape[0]
    yg = p["a"] * jax.nn.sigmoid(p["glu"])
    yh = p["ha"] * jax.nn.sigmoid(p["hglu"]) * c["hmask"]
    ycat = jnp.concatenate([yh, yg], axis=0)
    shifted = [ycat] + [ycat[b:b + tm + HALO - 8, :] for b in range(1, 8)]
    acc = jnp.zeros((tm, CONV_W), f32)
    for k in range(CONV_K):
        off = HALO - (CONV_K - 1) + k
        a8 = off - off % 8
        acc = acc + shifted[off % 8][a8:a8 + tm, :] * w["conv_w"][k:k + 1, :]
    yl = jax.nn.silu(_ln(acc + w["conv_b"], w["cln_g"], w["cln_b"]))
    ypw = _mm(yl, w["pw_w"]) + w["pw_b"]
    ync = _rms(ypw * jax.nn.silu(p["zc"]), w["bn_c"])
    cqn = _rms(p["cq"], w["qn_g"])
    qf = _mm(cqn, w["w_uq"])
    ckvn = _rms(p["ckv"], w["kvn_g"])
    kf = _mm(ckvn, w["w_k"])
    v = _mm(ckvn, w["w_v"])
    qs, ks = [], []
    for h in range(HEADS):
        sl = slice(h * HEAD_PAD, (h + 1) * HEAD_PAD)
        qh = _head_rms(qf[:, sl], w["qkq_g"])
        qs.append(_rope(qh, c["rc"], c["rs1"], c["rs2"]) * Q_SCALE)
        kh = _head_rms(kf[:, sl] + p["kr"], w["qkk_g"])
        ks.append(_rope(kh, c["rc"], c["rs1"], c["rs2"]))
    q = jnp.concatenate(qs, axis=-1)
    k = jnp.concatenate(ks, axis=-1)
    u = jax.nn.gelu(p["us"])
    v2 = _ln(jax.nn.gelu(p["vs"]), w["sln_g"], w["sln_b"])
    r = lax.broadcasted_iota(jnp.int32, (SG_CHUNK, SG_CHUNK), 0)
    cc = lax.broadcasted_iota(jnp.int32, (SG_CHUNK, SG_CHUNK), 1)
    lane = lax.broadcasted_iota(jnp.int32, (1, SG_W), 1) // (SG_W // SG_HEADS)
    wm = [jnp.where(r >= cc, w["sg_w"][g], 0.0) for g in range(SG_HEADS)]
    rows = []
    for ci in range(tm // SG_CHUNK):
        vc = v2[ci * SG_CHUNK:(ci + 1) * SG_CHUNK, :]
        mixed = jnp.zeros((SG_CHUNK, SG_W), f32)
        for g in range(SG_HEADS):
            mixed = mixed + jnp.where(lane == g, _mm(wm[g], vc) + w["sg_b"][g], 0.0)
        rows.append(mixed)
    mixed = jnp.concatenate(rows, axis=0) if len(rows) > 1 else rows[0]
    yns = _rms(u * mixed * jax.nn.silu(p["zs"]), w["bn_s"])
    return q, k, v, ync, yns


def _mla_out(o, zm, g):
    return _rms(o * jax.nn.silu(zm), g)


def _load_branch_inputs(proj_ref, halo_ref):
    p = {n: proj_ref[:, SEG[n][0]:SEG[n][1]].astype(f32) for n in SEG if n != "zm"}
    p["ha"] = halo_ref[:, 0:CONV_W].astype(f32)
    p["hglu"] = halo_ref[:, CONV_W:2 * CONV_W].astype(f32)
    return p


def _load_branch_weights(refs):
    w = {}
    for (n, _), r in zip(BR_W, refs):
        if n in ("sg_w", "sg_b"):
            w[n] = [r[g] for g in range(SG_HEADS)]
        else:
            w[n] = r[...]
    return w


def _const_spec(shape):
    nd = len(shape)
    return pl.BlockSpec(shape, lambda *_: (0,) * nd)


def _inproj_fwd(x, g, w, tm, name, ride=None):
    T = x.shape[0]
    nc = 5
    cw = PROJ_PAD // nc

    def body(x_ref, g_ref, w_ref, o_ref, h_ref):
        h = _rms(x_ref[...], g_ref[...]).astype(bf16)
        h_ref[...] = h
        for j in range(nc):
            o_ref[:, j * cw:(j + 1) * cw] = jnp.dot(
                h, w_ref[:, j * cw:(j + 1) * cw], preferred_element_type=f32).astype(bf16)

    return _call_with_rider(
        body, 3, 2, ride, (T // tm,),
        [pl.BlockSpec((tm, D_MODEL), lambda i: (i, 0)), _const_spec((1, D_MODEL)), _const_spec((D_MODEL, PROJ_PAD))],
        [pl.BlockSpec((tm, PROJ_PAD), lambda i: (i, 0)), pl.BlockSpec((tm, D_MODEL), lambda i: (i, 0))],
        [jax.ShapeDtypeStruct((T, PROJ_PAD), bf16), jax.ShapeDtypeStruct((T, D_MODEL), bf16)],
        (x, g, w), name)


def _branch_fwd(proj, tabs, bw, tm, name):
    T = proj.shape[0]
    hb = tm // HALO

    def body(proj_ref, halo_ref, rc_ref, rs1_ref, rs2_ref, *rest):
        wrefs, (q_ref, k_ref, v_ref, ync_ref, yns_ref) = rest[:len(BR_W)], rest[len(BR_W):]
        i = pl.program_id(0)
        c = {"hmask": (i > 0).astype(f32), "rc": rc_ref[...], "rs1": rs1_ref[...], "rs2": rs2_ref[...]}
        q, k, v, ync, yns = _branch(_load_branch_inputs(proj_ref, halo_ref), _load_branch_weights(wrefs), c)
        q_ref[...] = q.astype(bf16)
        k_ref[...] = k.astype(bf16)
        v_ref[...] = v.astype(bf16)
        ync_ref[...] = ync.astype(bf16)
        yns_ref[...] = yns.astype(bf16)

    row = lambda wd: pl.BlockSpec((tm, wd), lambda i: (i, 0))
    wide = HEADS * HEAD_PAD
    return pl.pallas_call(
        body, name=name, grid=(T // tm,),
        in_specs=[row(PROJ_PAD), pl.BlockSpec((HALO, 2 * CONV_W), lambda i: (jnp.maximum(i * hb - 1, 0), 1)),
                  row(HEAD_PAD), row(HEAD_PAD), row(HEAD_PAD)] + [_const_spec(s) for _, s in BR_W],
        out_specs=[row(wide), row(wide), row(wide), row(CONV_W), row(SG_W)],
        out_shape=[jax.ShapeDtypeStruct((T, wide), bf16)] * 3 + [jax.ShapeDtypeStruct((T, CONV_W), bf16)] * 2,
    )(proj, proj, *tabs, *[bw[n] for n, _ in BR_W])


def _attn_fwd(q, k, v, tb, name, ride=None):
    T = q.shape[0]
    nb = T // tb
    pw = 2 * HEAD_PAD

    rc = min(64, tb)

    def body(q_ref, k_ref, v_ref, o_ref, lse_ref, p_scr):
        qi = pl.program_id(1)
        krow = lax.broadcasted_iota(jnp.int32, (rc, tb), 0)
        qcol = lax.broadcasted_iota(jnp.int32, (rc, tb), 1)
        heads = [slice(e * HEAD_PAD, (e + 1) * HEAD_PAD) for e in range(2)]
        qh = [q_ref[:, hs] for hs in heads]

        def step(off, carry, masked):
            s_all = [lax.dot_general(k_ref[pl.ds(off, tb), hs], qh[e], (((1,), (1,)), ((), ())),
                                     preferred_element_type=f32) for e, hs in enumerate(heads)]
            out = []
            for e, hs in enumerate(heads):
                m, l, acc = carry[e]
                blks = []
                m8 = jnp.full((8, tb), -1e30, f32)
                for r in range(0, tb, rc):
                    blk = s_all[e][r:r + rc, :]
                    if masked:
                        blk = jnp.where(krow + r <= qcol, blk, -1e30)
                    blks.append(blk)
                    m8 = jnp.maximum(m8, jnp.max(blk.reshape(rc // 8, 8, tb), axis=0))
                m_new = jnp.maximum(m, jnp.max(m8, axis=0, keepdims=True))
                alpha = jnp.exp2(m - m_new)
                p8 = jnp.zeros((8, tb), f32)
                for r in range(0, tb, rc):
                    p = jnp.exp2(blks[r // rc] - m_new)
                    p8 = p8 + jnp.sum(p.reshape(rc // 8, 8, tb), axis=0)
                    p_scr[e, r:r + rc, :] = p.astype(bf16)
                acc = alpha * acc + lax.dot_general(v_ref[pl.ds(off, tb), hs], p_scr[e], (((0,), (0,)), ((), ())),
                                                    preferred_element_type=f32)
                out.append((m_new, alpha * l + jnp.sum(p8, axis=0, keepdims=True), acc))
            return tuple(out)

        one = (jnp.full((1, tb), -1e30, f32), jnp.zeros((1, tb), f32), jnp.zeros((HEAD_PAD, tb), f32))
        carry = lax.fori_loop(0, qi, lambda ki, cr: step(pl.multiple_of(ki * tb, tb), cr, False), (one, one))
        carry = step(pl.multiple_of(qi * tb, tb), carry, True)
        o_t = jnp.zeros((HEAD_PAD, tb), f32)
        for e in range(2):
            m, l, acc = carry[e]
            o_t = o_t + acc / l
            lse_ref[e] = m + jnp.log2(l)
        o_ref[...] = o_t.T

    return _call_with_rider(
        body, 3, 2, ride, (HEADS // 2, nb),
        [pl.BlockSpec((tb, pw), lambda h, i: (i, h)), pl.BlockSpec((T, pw), lambda h, i: (0, h)),
         pl.BlockSpec((T, pw), lambda h, i: (0, h))],
        [pl.BlockSpec((tb, HEAD_PAD), lambda h, i: (i, h)), pl.BlockSpec((2, 1, tb), lambda h, i: (h, 0, i))],
        [jax.ShapeDtypeStruct((T, MLA_W), f32), jax.ShapeDtypeStruct((HEADS, 1, T), f32)],
        (q, k, v), name, scratch=[pltpu.VMEM((2, tb, tb), bf16)])


def _out_layer(x_ref, o_ref, zm_ref, ync_ref, yns_ref, g_ref, w_ref):
    ynm = _mla_out(o_ref[...], zm_ref[...].astype(f32), g_ref[...]).astype(bf16)
    y = x_ref[...]
    y = y + jnp.dot(ync_ref[...], w_ref[0:CONV_W, :], preferred_element_type=f32)
    y = y + jnp.dot(ynm, w_ref[CONV_W:CONV_W + MLA_W, :], preferred_element_type=f32)
    return y + jnp.dot(yns_ref[...], w_ref[CONV_W + MLA_W:, :], preferred_element_type=f32)


def _out_fwd(x, o, proj, ync, yns, g_m, w_out, tm, name):
    T = x.shape[0]

    def body(*refs):
        refs[-1][...] = _out_layer(*refs[:-1])

    row = lambda wd: pl.BlockSpec((tm, wd), lambda i: (i, 0))
    return pl.pallas_call(
        body, name=name, grid=(T // tm,),
        in_specs=[row(D_MODEL), row(MLA_W), row(MLA_W), row(CONV_W), row(SG_W), _const_spec((1, MLA_W)),
                  _const_spec((D_MODEL, D_MODEL))],
        out_specs=row(D_MODEL),
        out_shape=jax.ShapeDtypeStruct((T, D_MODEL), f32),
    )(x, o, proj, ync, yns, g_m, w_out)


def _out_fwd_loss(x, o, proj, ync, yns, g_m, w_out, tgt, tm, name):
    T = x.shape[0]
    nt = T // tm

    def body(x_ref, o_ref, zm_ref, ync_ref, yns_ref, g_ref, w_ref, t_ref, dy_ref, loss_ref, acc_ref):
        i = pl.program_id(0)

        @pl.when(i == 0)
        def _():
            acc_ref[...] = jnp.zeros_like(acc_ref)

        err = _out_layer(x_ref, o_ref, zm_ref, ync_ref, yns_ref, g_ref, w_ref) - t_ref[...]
        dy_ref[...] = err * (1.0 / D_MODEL)
        acc_ref[...] += jnp.sum(err * err, axis=0, keepdims=True)

        @pl.when(i == nt - 1)
        def _():
            loss_ref[...] = jnp.full((1, HEAD_PAD), 0.5 / D_MODEL, f32) * jnp.sum(acc_ref[...])

    row = lambda wd: pl.BlockSpec((tm, wd), lambda i: (i, 0))
    return pl.pallas_call(
        body, name=name, grid=(nt,),
        in_specs=[row(D_MODEL), row(MLA_W), row(MLA_W), row(CONV_W), row(SG_W), _const_spec((1, MLA_W)),
                  _const_spec((D_MODEL, D_MODEL)), row(D_MODEL)],
        out_specs=[row(D_MODEL), _const_spec((1, HEAD_PAD))],
        out_shape=[jax.ShapeDtypeStruct((T, D_MODEL), f32), jax.ShapeDtypeStruct((1, HEAD_PAD), f32)],
        scratch_shapes=[pltpu.VMEM((1, D_MODEL), f32)],
    )(x, o, proj, ync, yns, g_m, w_out, tgt)


def _out_bwd(dout, o, proj, ync, yns, g_m, w_out, tm, name):
    T = dout.shape[0]

    def body(dy_ref, o_ref, zm_ref, ync_ref, yns_ref, g_ref, w_ref,
             do_ref, dl_ref, dzm_ref, dync_ref, dyns_ref, dw_ref, dg_ref):
        i = pl.program_id(0)

        @pl.when(i == 0)
        def _():
            dw_ref[...] = jnp.zeros_like(dw_ref)
            dg_ref[...] = jnp.zeros_like(dg_ref)

        dyb = dy_ref[...].astype(bf16)
        nt = (((1,), (1,)), ((), ()))
        tn = (((0,), (0,)), ((), ()))
        d_c = lax.dot_general(dyb, w_ref[0:CONV_W, :], nt, preferred_element_type=f32)
        d_m = lax.dot_general(dyb, w_ref[CONV_W:CONV_W + MLA_W, :], nt, preferred_element_type=f32)
        d_s = lax.dot_general(dyb, w_ref[CONV_W + MLA_W:, :], nt, preferred_element_type=f32)
        o = o_ref[...]
        ynm, vjp = jax.vjp(_mla_out, o, zm_ref[...].astype(f32), g_ref[...])
        do, dzm, dg = vjp(d_m)
        do_ref[...] = do.astype(bf16)
        dzm_ref[...] = dzm.astype(bf16)
        dync_ref[...] = d_c.astype(bf16)
        dyns_ref[...] = d_s.astype(bf16)
        dg_ref[...] += dg
        dw_ref[0:CONV_W, :] += lax.dot_general(ync_ref[...], dyb, tn, preferred_element_type=f32)
        dw_ref[CONV_W:CONV_W + MLA_W, :] += lax.dot_general(ynm.astype(bf16), dyb, tn, preferred_element_type=f32)
        dw_ref[CONV_W + MLA_W:, :] += lax.dot_general(yns_ref[...], dyb, tn, preferred_element_type=f32)
        prod = do * o
        hi = prod.astype(bf16)
        lo = (prod - hi.astype(f32)).astype(bf16)
        sel = (lax.broadcasted_iota(jnp.int32, (HEADS, MLA_W), 1) // VDIM
               == lax.broadcasted_iota(jnp.int32, (HEADS, MLA_W), 0)).astype(bf16)
        dl_ref[...] = (lax.dot_general(sel, hi, nt, preferred_element_type=f32)
                       + lax.dot_general(sel, lo, nt, preferred_element_type=f32))

    row = lambda wd: pl.BlockSpec((tm, wd), lambda i: (i, 0))
    return pl.pallas_call(
        body, name=name, grid=(T // tm,),
        in_specs=[row(D_MODEL), row(MLA_W), row(MLA_W), row(CONV_W), row(SG_W), _const_spec((1, MLA_W)),
                  _const_spec((D_MODEL, D_MODEL))],
        out_specs=[row(MLA_W), pl.BlockSpec((HEADS, tm), lambda i: (0, i)), row(MLA_W), row(CONV_W), row(SG_W),
                   _const_spec((D_MODEL, D_MODEL)), _const_spec((1, MLA_W))],
        out_shape=[jax.ShapeDtypeStruct((T, MLA_W), bf16), jax.ShapeDtypeStruct((HEADS, T), f32),
                   jax.ShapeDtypeStruct((T, MLA_W), bf16), jax.ShapeDtypeStruct((T, CONV_W), bf16),
                   jax.ShapeDtypeStruct((T, SG_W), bf16), jax.ShapeDtypeStruct((D_MODEL, D_MODEL), f32),
                   jax.ShapeDtypeStruct((1, MLA_W), f32)],
    )(dout, o, proj, ync, yns, g_m, w_out)


def _attn_bwd(q, k, v, do, lse, delta, tb, name, ride=None):
    T = q.shape[0]
    nb = T // tb
    pw = 2 * HEAD_PAD
    nt = (((1,), (1,)), ((), ()))
    tn = (((0,), (0,)), ((), ()))

    def body(q_ref, do_ref, lse_ref, dl_ref, k_ref, v_ref, dq_ref, dk_ref, dv_ref):
        kj = pl.program_id(1)

        @pl.when(kj == 0)
        def _():
            dq_ref[...] = jnp.zeros_like(dq_ref)

        hw = tb // 2
        krow = lax.broadcasted_iota(jnp.int32, (tb, hw), 0)
        qcol = lax.broadcasted_iota(jnp.int32, (tb, hw), 1)
        heads = [slice(e * HEAD_PAD, (e + 1) * HEAD_PAD) for e in range(2)]
        kb = [k_ref[:, hs] for hs in heads]
        vb = [v_ref[:, hs] for hs in heads]

        def step(off, carry, masked):
            carry = [list(c) for c in carry]
            for half in range(2):
                o2 = pl.multiple_of(off + half * hw, hw)
                dob = do_ref[pl.ds(o2, hw), :]
                for e, hs in enumerate(heads):
                    dk, dv = carry[e]
                    qb = q_ref[pl.ds(o2, hw), hs]
                    s_t = lax.dot_general(kb[e], qb, nt, preferred_element_type=f32)
                    p_t = jnp.exp2(s_t - lse_ref[e, :, pl.ds(o2, hw)])
                    if masked:
                        p_t = jnp.where(krow <= qcol + half * hw, p_t, 0.0)
                    dv = dv + jnp.dot(p_t.astype(bf16), dob, preferred_element_type=f32)
                    dp_t = lax.dot_general(vb[e], dob, nt, preferred_element_type=f32)
                    ds_t = (p_t * (dp_t - dl_ref[e, :, pl.ds(o2, hw)])).astype(bf16)
                    dk = dk + jnp.dot(ds_t, qb, preferred_element_type=f32)
                    dq_ref[pl.ds(o2, hw), hs] += lax.dot_general(ds_t, kb[e], tn, preferred_element_type=f32)
                    carry[e] = [dk, dv]
            return tuple(tuple(c) for c in carry)

        zero = jnp.zeros((tb, HEAD_PAD), f32)
        carry = step(pl.multiple_of(kj * tb, tb), ((zero, zero), (zero, zero)), True)
        carry = lax.fori_loop(kj + 1, nb, lambda qi, cr: step(pl.multiple_of(qi * tb, tb), cr, False), carry)
        for e, hs in enumerate(heads):
            dk_ref[:, hs] = carry[e][0]
            dv_ref[:, hs] = carry[e][1]

    wide = HEADS * HEAD_PAD
    return _call_with_rider(
        body, 6, 3, ride, (HEADS // 2, nb),
        [pl.BlockSpec((T, pw), lambda h, j: (0, h)), pl.BlockSpec((T, HEAD_PAD), lambda h, j: (0, h)),
         pl.BlockSpec((2, 1, T), lambda h, j: (h, 0, 0)), pl.BlockSpec((2, 1, T), lambda h, j: (h, 0, 0)),
         pl.BlockSpec((tb, pw), lambda h, j: (j, h)), pl.BlockSpec((tb, pw), lambda h, j: (j, h))],
        [pl.BlockSpec((T, pw), lambda h, j: (0, h)), pl.BlockSpec((tb, pw), lambda h, j: (j, h)),
         pl.BlockSpec((tb, pw), lambda h, j: (j, h))],
        [jax.ShapeDtypeStruct((T, wide), f32)] * 3,
        (q, do, lse, delta, k, v), name)


def _branch_bwd(proj, tabs, bw, dq, dk, dv, dync, dyns, tm, name):
    T = proj.shape[0]
    nt = T // tm
    hb = tm // HALO
    nw = len(BR_W)

    def body(proj_ref, halo_ref, rc_ref, rs1_ref, rs2_ref, *rest):
        wrefs = rest[:nw]
        dq_ref, dk_ref, dv_ref, dync_ref, dyns_ref = rest[nw:nw + 5]
        dp_ref = rest[nw + 5]
        gwrefs = rest[nw + 6:2 * nw + 6]
        carry_ref = rest[2 * nw + 6]
        i = pl.program_id(0)
        r = nt - 1 - i

        @pl.when(i == 0)
        def _():
            carry_ref[...] = jnp.zeros_like(carry_ref)
            for g in gwrefs:
                g[...] = jnp.zeros_like(g)

        c = {"hmask": (r > 0).astype(f32), "rc": rc_ref[...], "rs1": rs1_ref[...], "rs2": rs2_ref[...]}
        _, vjp = jax.vjp(lambda p_, w_: _branch(p_, w_, c), _load_branch_inputs(proj_ref, halo_ref),
                         _load_branch_weights(wrefs))
        cts = (dq_ref[...] * LN2, dk_ref[...] * LN2, dv_ref[...], dync_ref[...].astype(f32),
               dyns_ref[...].astype(f32))
        dp, dw = vjp(cts)
        for n in SEG:
            if n in ("zm", "a", "glu"):
                continue
            dp_ref[:, SEG[n][0] - 512:SEG[n][1] - 512] = dp[n].astype(bf16)
        for n, hn, lo in (("a", "ha", 0), ("glu", "hglu", CONV_W)):
            d = dp[n]
            tail = d[tm - HALO:, :] + carry_ref[:, lo:lo + CONV_W]
            s0 = SEG[n][0] - 512
            dp_ref[0:tm - HALO, s0:s0 + CONV_W] = d[:tm - HALO, :].astype(bf16)
            dp_ref[tm - HALO:tm, s0:s0 + CONV_W] = tail.astype(bf16)
        carry_ref[:, 0:CONV_W] = dp["ha"]
        carry_ref[:, CONV_W:] = dp["hglu"]
        for (n, _), g in zip(BR_W, gwrefs):
            if n in ("sg_w", "sg_b"):
                for gi in range(SG_HEADS):
                    g[gi] += dw[n][gi]
            else:
                g[...] += dw[n]

    row = lambda wd: pl.BlockSpec((tm, wd), lambda i: (nt - 1 - i, 0))
    wide = HEADS * HEAD_PAD
    return pl.pallas_call(
        body, name=name, grid=(nt,),
        in_specs=[row(PROJ_PAD),
                  pl.BlockSpec((HALO, 2 * CONV_W), lambda i: (jnp.maximum((nt - 1 - i) * hb - 1, 0), 1)),
                  row(HEAD_PAD), row(HEAD_PAD), row(HEAD_PAD)] + [_const_spec(s) for _, s in BR_W]
                 + [row(wide), row(wide), row(wide), row(CONV_W), row(SG_W)],
        out_specs=[row(REST)] + [_const_spec(s) for _, s in BR_W],
        out_shape=[jax.ShapeDtypeStruct((T, REST), bf16)] + [jax.ShapeDtypeStruct(s, f32) for _, s in BR_W],
        scratch_shapes=[pltpu.VMEM((HALO, 2 * CONV_W), f32)],
    )(proj, proj, *tabs, *[bw[n] for n, _ in BR_W], dq, dk, dv, dync, dyns)


def _inproj_bwd_dx(x, g, w, dzm, dprest, dout, tm, name, ride=None):
    T = x.shape[0]
    nt_dims = (((1,), (1,)), ((), ()))

    def body(x_ref, g_ref, w_ref, dzm_ref, dpr_ref, dout_ref, dx_ref, dg_ref):
        i = pl.program_id(0)

        @pl.when(i == 0)
        def _():
            dg_ref[...] = jnp.zeros_like(dg_ref)

        dh = lax.dot_general(dzm_ref[...], w_ref[:, 0:512], nt_dims, preferred_element_type=f32)
        dh = dh + lax.dot_general(dpr_ref[...], w_ref[:, 512:], nt_dims, preferred_element_type=f32)
        _, vjp = jax.vjp(_rms, x_ref[...], g_ref[...])
        dx, dg = vjp(dh)
        dx_ref[...] = dout_ref[...] + dx
        dg_ref[...] += dg

    row = lambda wd: pl.BlockSpec((tm, wd), lambda i: (i, 0))
    return _call_with_rider(
        body, 6, 2, ride, (T // tm,),
        [row(D_MODEL), _const_spec((1, D_MODEL)), _const_spec((D_MODEL, PROJ_PAD)), row(512), row(REST),
         row(D_MODEL)],
        [row(D_MODEL), _const_spec((1, D_MODEL))],
        [jax.ShapeDtypeStruct((T, D_MODEL), f32), jax.ShapeDtypeStruct((1, D_MODEL), f32)],
        (x, g, w, dzm, dprest, dout), name)


def _inproj_bwd_dw(h, dzm, dprest, tm, name, ride=None):
    T = h.shape[0]
    tm = min(4 * tm, T)
    nt = T // tm
    cw = REST // 3
    tn = (((0,), (0,)), ((), ()))

    def make_body():
        def body(h_ref, d_ref, o_ref):
            i = pl.program_id(1)

            @pl.when(i == 0)
            def _():
                o_ref[...] = jnp.zeros_like(o_ref)

            o_ref[...] += lax.dot_general(h_ref[...], d_ref[...], tn, preferred_element_type=f32)
        return body

    def run(d, width, cwid, nm, rd):
        return _call_with_rider(
            make_body(), 2, 1, rd, (width // cwid, nt),
            [pl.BlockSpec((tm, D_MODEL), lambda j, i: (i, 0)), pl.BlockSpec((tm, cwid), lambda j, i: (i, j))],
            [pl.BlockSpec((D_MODEL, cwid), lambda j, i: (0, j))], [jax.ShapeDtypeStruct((D_MODEL, width), f32)],
            (h, d), nm)

    (zm,) = run(dzm, 512, 512, name + "_zm", None)
    rest, *sent = run(dprest, REST, cw, name + "_rest", ride)
    return (zm, rest, *sent)


def _row_tile(rows):
    return ROW_ALIGN if rows > ROW_ALIGN and rows % ROW_ALIGN == 0 else rows


def _sum4(buf, name):
    _, R, C = buf.shape
    tr = _row_tile(R)

    def body(b_ref, o_ref):
        b = [b_ref[s].astype(f32) for s in range(4)]
        o_ref[...] = ((b[0] + b[1]) + b[2]) + b[3]

    return pl.pallas_call(
        body, name=name, grid=(R // tr,),
        in_specs=[pl.BlockSpec((4, tr, C), lambda i: (0, i, 0))],
        out_specs=pl.BlockSpec((tr, C), lambda i: (i, 0)),
        out_shape=jax.ShapeDtypeStruct((R, C), f32),
    )(buf)


def _sum4_whole(bufs, name):
    n = len(bufs)

    def body(*refs):
        for b_ref, o_ref in zip(refs[:n], refs[n:]):
            o_ref[...] = ((b_ref[0] + b_ref[1]) + b_ref[2]) + b_ref[3]

    return pl.pallas_call(
        body, name=name, out_shape=[jax.ShapeDtypeStruct(b.shape[1:], f32) for b in bufs])(*bufs)


def _adamw_math(w, m, v, g):
    m_new = ADAM_B1 * m + (1.0 - ADAM_B1) * g
    v_new = ADAM_B2 * v + (1.0 - ADAM_B2) * (g * g)
    m_hat = m_new / (1.0 - ADAM_B1 ** ADAM_STEP)
    v_hat = v_new / (1.0 - ADAM_B2 ** ADAM_STEP)
    return -ADAM_LR * (m_hat / (jnp.sqrt(v_hat) + ADAM_EPS) + ADAM_WD * w), m_new, v_new


def _adamw(w, m, v, p_a, p_b, name):
    R, C = w.shape
    tr = _row_tile(R)

    def body(w_ref, m_ref, v_ref, a_ref, b_ref, g_ref, d_ref, nm_ref, nv_ref):
        g = a_ref[...] + b_ref[...]
        g_ref[...] = g
        d_ref[...], nm_ref[...], nv_ref[...] = _adamw_math(w_ref[...], m_ref[...], v_ref[...], g)

    spec = pl.BlockSpec((tr, C), lambda i: (i, 0))
    return pl.pallas_call(
        body, name=name, grid=(R // tr,), in_specs=[spec] * 5, out_specs=[spec] * 4,
        out_shape=[jax.ShapeDtypeStruct((R, C), f32)] * 4,
    )(w, m, v, p_a, p_b)


def _adamw_whole(ws, ms, vs, p_a, p_b, name):
    n = len(ws)

    def body(*refs):
        ins, outs = refs[:5 * n], refs[5 * n:]
        for i in range(n):
            w, m, v, a, b = (ins[k * n + i][...] for k in range(5))
            g = a + b
            outs[i][...] = g
            outs[n + i][...], outs[2 * n + i][...], outs[3 * n + i][...] = _adamw_math(w, m, v, g)

    res = pl.pallas_call(
        body, name=name, out_shape=[jax.ShapeDtypeStruct(w.shape, f32) for w in ws] * 4,
    )(*ws, *ms, *vs, *p_a, *p_b)
    return [res[k * n:(k + 1) * n] for k in range(4)]


class _ChipExchange:
    def __init__(self, srcs, per_target):
        self.n = len(srcs)
        self.per_target = per_target
        blocks = [s.shape[1:] if p else s.shape for s, p in zip(srcs, per_target)]
        self.out_shape = [jax.ShapeDtypeStruct((4,) + tuple(b), s.dtype) for b, s in zip(blocks, srcs)]
        self.scratch_shapes = [pltpu.SemaphoreType.DMA((3 * self.n,)), pltpu.SemaphoreType.DMA((3 * self.n,)),
                               pltpu.SemaphoreType.DMA((self.n,))]
        self.any_specs = [pl.BlockSpec(memory_space=pl.ANY)] * self.n

    def _copies(self, src_refs, out_refs, sems):
        send_sems, recv_sems, local_sems = sems
        x, y, c = lax.axis_index("x"), lax.axis_index("y"), lax.axis_index("c")
        me = 2 * x + y
        chips = [(1 - x, y), (x, 1 - y), (1 - x, 1 - y)]

        def block_for(i, t):
            return src_refs[i].at[t] if self.per_target[i] else src_refs[i]

        def remote(i, k, block_t, slot):
            tx, ty = chips[k]
            return pltpu.make_async_remote_copy(
                src_ref=block_for(i, block_t), dst_ref=out_refs[i].at[slot], send_sem=send_sems.at[3 * i + k],
                recv_sem=recv_sems.at[3 * i + k], device_id=(tx, ty, c), device_id_type=pl.DeviceIdType.MESH)

        pairs = [(i, k) for i in range(self.n) for k in range(3)]
        sends = [remote(i, k, 2 * chips[k][0] + chips[k][1], me) for i, k in pairs]
        recvs = [remote(i, k, me, 2 * chips[k][0] + chips[k][1]) for i, k in pairs]
        mine = [pltpu.make_async_copy(block_for(i, me), out_refs[i].at[me], local_sems.at[i])
                for i in range(self.n)]
        return sends, recvs, mine

    def start(self, src_refs, out_refs, sems):
        sends, _, mine = self._copies(src_refs, out_refs, sems)
        for cp in sends + mine:
            cp.start()

    def wait(self, src_refs, out_refs, sems):
        sends, recvs, mine = self._copies(src_refs, out_refs, sems)
        for cp in recvs:
            cp.wait_recv()
        for cp in sends:
            cp.wait_send()
        for cp in mine:
            cp.wait()


def _call_with_rider(body, n_in, n_out, ride, grid, in_specs, out_specs, out_shape, args, name, scratch=()):
    scratch = list(scratch)
    if ride is None:
        return pl.pallas_call(body, name=name, grid=grid, in_specs=in_specs, out_specs=out_specs,
                              out_shape=out_shape, scratch_shapes=scratch)(*args)
    ex = _ChipExchange(*ride)
    ne = ex.n

    def wrapped(*refs):
        ins, src_refs = refs[:n_in], refs[n_in:n_in + ne]
        outs, ex_outs = refs[n_in + ne:n_in + ne + n_out], refs[n_in + ne + n_out:n_in + 2 * ne + n_out]
        own = refs[n_in + 2 * ne + n_out:n_in + 2 * ne + n_out + len(scratch)]
        sems = refs[n_in + 2 * ne + n_out + len(scratch):]
        ids = [pl.program_id(a) for a in range(len(grid))]
        first = functools.reduce(jnp.logical_and, [i == 0 for i in ids])
        last = functools.reduce(jnp.logical_and, [i == g - 1 for i, g in zip(ids, grid)])

        @pl.when(first)
        def _():
            ex.start(src_refs, ex_outs, sems)

        body(*ins, *outs, *own)

        @pl.when(last)
        def _():
            ex.wait(src_refs, ex_outs, sems)

    return pl.pallas_call(
        wrapped, name=name, grid=grid, in_specs=list(in_specs) + ex.any_specs,
        out_specs=list(out_specs) + ex.any_specs, out_shape=list(out_shape) + ex.out_shape,
        scratch_shapes=scratch + ex.scratch_shapes,
    )(*args, *ride[0])


def _chip_exchange(srcs, per_target, name):
    ex = _ChipExchange(srcs, per_target)
    n = ex.n

    def body(*refs):
        src_refs, out_refs, sems = refs[:n], refs[n:2 * n], refs[2 * n:]
        ex.start(src_refs, out_refs, sems)
        ex.wait(src_refs, out_refs, sems)

    return pl.pallas_call(
        body, name=name, in_specs=ex.any_specs, out_specs=ex.any_specs, out_shape=ex.out_shape,
        scratch_shapes=ex.scratch_shapes,
    )(*srcs)


def _sibling_exchange(srcs, name):
    n = len(srcs)

    def body(*refs):
        src_refs, out_refs = refs[:n], refs[n:2 * n]
        send_sems, recv_sems = refs[2 * n:]
        x, y, c = lax.axis_index("x"), lax.axis_index("y"), lax.axis_index("c")
        copies = [pltpu.make_async_remote_copy(
            src_ref=src_refs[i], dst_ref=out_refs[i], send_sem=send_sems.at[i], recv_sem=recv_sems.at[i],
            device_id=(x, y, 1 - c), device_id_type=pl.DeviceIdType.MESH) for i in range(n)]
        for cp in copies:
            cp.start()
        for cp in copies:
            cp.wait()

    return pl.pallas_call(
        body, name=name,
        in_specs=[pl.BlockSpec(memory_space=pl.ANY)] * n, out_specs=[pl.BlockSpec(memory_space=pl.ANY)] * n,
        out_shape=[jax.ShapeDtypeStruct(s.shape, s.dtype) for s in srcs],
        scratch_shapes=[pltpu.SemaphoreType.DMA((n,)), pltpu.SemaphoreType.DMA((n,))],
    )(*srcs)


W_IN_SHARD = IN_COLS // 4
W_IN_MAP = [(0, 1792, 512), (1792, 1824, 2368), (1824, 2336, 0), (2336, IN_COLS, 2432)]


def _pad_w_in(shards):
    def cols(a, b):
        out = []
        while a < b:
            j = a // W_IN_SHARD
            e = min(b, (j + 1) * W_IN_SHARD)
            out.append(shards[j][:, a - j * W_IN_SHARD:e - j * W_IN_SHARD])
            a = e
        return out

    z = lambda n: [jnp.zeros((D_MODEL, n), shards.dtype)]
    return jnp.concatenate(cols(1824, 2336) + cols(0, 1792) + z(64) + cols(1792, 1824) + z(32) + cols(2336, IN_COLS),
                           axis=1)


def _w_in_grad_shard(dw_zm, dw_rest, j):
    lo, hi = j * W_IN_SHARD, (j + 1) * W_IN_SHARD
    out = []
    for a, b, p in W_IN_MAP:
        a2, b2 = max(a, lo), min(b, hi)
        if a2 < b2:
            p0 = p + a2 - a
            src, off = (dw_zm, 0) if p0 < 512 else (dw_rest, 512)
            out.append(src[:, p0 - off:p0 - off + b2 - a2])
    return jnp.concatenate(out, axis=1)


def _v_cols():
    return [(h % 2) * VDIM for h in range(HEADS)]


def _layer_weights(gathered, small, l):
    full = {n: jnp.concatenate([g[j] for j in range(4)], axis=BIG[n] - 1) for n, g in gathered.items()}
    w_uq = full["w_uq"].astype(f32).reshape(Q_LORA, HEADS, QK)
    w_uq = jnp.pad(w_uq, ((0, 0), (0, 0), (0, HEAD_PAD - QK))).reshape(Q_LORA, HEADS * HEAD_PAD)
    ukv = full["w_ukv"].astype(f32).reshape(KV_LORA, HEADS, NOPE + VDIM)
    w_k = jnp.pad(ukv[:, :, :NOPE], ((0, 0), (0, 0), (0, HEAD_PAD - NOPE))).reshape(KV_LORA, HEADS * HEAD_PAD)
    zeros = jnp.zeros((KV_LORA, VDIM), f32)
    w_v = jnp.concatenate(
        [jnp.concatenate([ukv[:, h, NOPE:], zeros] if h % 2 == 0 else [zeros, ukv[:, h, NOPE:]], axis=1)
         for h in range(HEADS)], axis=1)
    row = lambda a: a.reshape(1, -1)
    bn = small["branch_norm_g"][l]
    bw = {
        "conv_w": jnp.pad(full["conv_w"], ((0, 1), (0, 0))), "conv_b": row(small["conv_b"][l]),
        "cln_g": row(small["conv_ln_g"][l]), "cln_b": row(small["conv_ln_b"][l]),
        "pw_w": full["conv_pw_w"].astype(f32), "pw_b": row(small["conv_pw_b"][l]),
        "qn_g": row(small["q_norm_g"][l]), "w_uq": w_uq, "kvn_g": row(small["kv_norm_g"][l]),
        "w_k": w_k, "w_v": w_v,
        "qkq_g": jnp.pad(row(small["qk_q_g"][l]), ((0, 0), (0, HEAD_PAD - QK))),
        "qkk_g": jnp.pad(row(small["qk_k_g"][l]), ((0, 0), (0, HEAD_PAD - QK))),
        "sln_g": row(small["sg_ln_g"][l]), "sln_b": row(small["sg_ln_b"][l]),
        "sg_w": small["sg_w"][l], "sg_b": small["sg_b"][l].reshape(SG_HEADS, SG_CHUNK, 1),
        "bn_c": row(bn[:CONV_W]), "bn_s": row(bn[CONV_W + MLA_W:]),
    }
    return {"bw": bw, "bn_m": row(bn[CONV_W:CONV_W + MLA_W]), "w_out": full["w_out"]}


def _layer_grads(gb, dw_out, dbn_m):
    duq = gb["w_uq"].reshape(Q_LORA, HEADS, HEAD_PAD)[:, :, :QK].reshape(Q_LORA, HEADS * QK)
    dk = gb["w_k"].reshape(KV_LORA, HEADS, HEAD_PAD)[:, :, :NOPE]
    dv = gb["w_v"].reshape(KV_LORA, HEADS, HEAD_PAD)
    dv = jnp.stack([dv[:, h, c0:c0 + VDIM] for h, c0 in enumerate(_v_cols())], axis=1)
    dukv = jnp.concatenate([dk, dv], axis=2).reshape(KV_LORA, HEADS * (NOPE + VDIM))
    return {
        "conv_w": gb["conv_w"][:CONV_K],
        "conv_b": gb["conv_b"][0], "conv_ln_g": gb["cln_g"][0], "conv_ln_b": gb["cln_b"][0],
        "conv_pw_w": gb["pw_w"], "conv_pw_b": gb["pw_b"][0], "q_norm_g": gb["qn_g"][0], "w_uq": duq,
        "kv_norm_g": gb["kvn_g"][0], "w_ukv": dukv, "qk_q_g": gb["qkq_g"][0, :QK], "qk_k_g": gb["qkk_g"][0, :QK],
        "sg_ln_g": gb["sln_g"][0], "sg_ln_b": gb["sln_b"][0], "sg_w": gb["sg_w"], "sg_b": gb["sg_b"][:, :, 0],
        "branch_norm_g": jnp.concatenate([gb["bn_c"][0], dbn_m[0], gb["bn_s"][0]]), "w_out": dw_out,
    }


def _rope_tables(T):
    half = ROPE // 2
    inv_freq = ROPE_THETA ** (-jnp.arange(half, dtype=f32) / half)
    ang = jnp.arange(T, dtype=f32)[:, None] * inv_freq[None, :]
    cos, sin = jnp.cos(ang), jnp.sin(ang)
    one = jnp.ones((T, NOPE), f32)
    z = lambda n: jnp.zeros((T, n), f32)
    rc = jnp.concatenate([one, cos, cos, jnp.ones((T, HEAD_PAD - QK), f32)], axis=1)
    rs1 = jnp.concatenate([z(NOPE), -sin, z(half), z(HEAD_PAD - QK)], axis=1)
    rs2 = jnp.concatenate([z(NOPE), z(half), sin, z(HEAD_PAD - QK)], axis=1)
    return rc, rs1, rs2


def kernel(x, norm_g, w_in, conv_w, conv_b, conv_ln_g, conv_ln_b, conv_pw_w, conv_pw_b, q_norm_g, w_uq, kv_norm_g, w_ukv, qk_q_g, qk_k_g, sg_ln_g, sg_ln_b, sg_w, sg_b, branch_norm_g, w_out, loss_target, m_norm_g, m_w_in, m_conv_w, m_conv_b, m_conv_ln_g, m_conv_ln_b, m_conv_pw_w, m_conv_pw_b, m_q_norm_g, m_w_uq, m_kv_norm_g, m_w_ukv, m_qk_q_g, m_qk_k_g, m_sg_ln_g, m_sg_ln_b, m_sg_w, m_sg_b, m_branch_norm_g, m_w_out, v_norm_g, v_w_in, v_conv_w, v_conv_b, v_conv_ln_g, v_conv_ln_b, v_conv_pw_w, v_conv_pw_b, v_q_norm_g, v_w_uq, v_kv_norm_g, v_w_ukv, v_qk_q_g, v_qk_k_g, v_sg_ln_g, v_sg_ln_b, v_sg_w, v_sg_b, v_branch_norm_g, v_w_out):
    wts = dict(zip(WEIGHTS, [norm_g, w_in, conv_w, conv_b, conv_ln_g, conv_ln_b, conv_pw_w, conv_pw_b, q_norm_g,
                             w_uq, kv_norm_g, w_ukv, qk_q_g, qk_k_g, sg_ln_g, sg_ln_b, sg_w, sg_b, branch_norm_g,
                             w_out]))
    mom_m = dict(zip(WEIGHTS, [m_norm_g, m_w_in, m_conv_w, m_conv_b, m_conv_ln_g, m_conv_ln_b, m_conv_pw_w,
                               m_conv_pw_b, m_q_norm_g, m_w_uq, m_kv_norm_g, m_w_ukv, m_qk_q_g, m_qk_k_g,
                               m_sg_ln_g, m_sg_ln_b, m_sg_w, m_sg_b, m_branch_norm_g, m_w_out]))
    mom_v = dict(zip(WEIGHTS, [v_norm_g, v_w_in, v_conv_w, v_conv_b, v_conv_ln_g, v_conv_ln_b, v_conv_pw_w,
                               v_conv_pw_b, v_q_norm_g, v_w_uq, v_kv_norm_g, v_w_ukv, v_qk_q_g, v_qk_k_g,
                               v_sg_ln_g, v_sg_ln_b, v_sg_w, v_sg_b, v_branch_norm_g, v_w_out]))
    xs = x[0]
    tgt = loss_target[0]
    T = xs.shape[0]
    tm = min(256, T)
    tmb = min(128, T)
    tmm = min(512, T)
    tb = min(512, T // 2)

    big = list(BIG)
    rest = [n for n in big if n != "w_in"]
    nb = len(big)
    shards = [{n: wts[n][l] if n == "conv_w" else wts[n][l].astype(bf16) for n in big} for l in range(DEPTH)]
    small = {n: wts[n] for n in SMALL}
    tabs = _rope_tables(T)
    gathered = dict(zip(["w_in"], _chip_exchange([shards[0]["w_in"]], [False], "gather_w_in_0")))

    acts, lw = [], []
    h_in = xs
    for l in range(DEPTH):
        W = {"norm_g": small["norm_g"][l].reshape(1, -1), "w_in": _pad_w_in(gathered["w_in"])}
        ride = ([shards[l][n] for n in rest], [False] * len(rest)) if l == 0 else None
        proj, hb, *late = _inproj_fwd(h_in, W["norm_g"], W["w_in"], tmm, f"inproj_fwd_{l}", ride)
        if ride is not None:
            gathered.update(zip(rest, late))
        W.update(_layer_weights({n: gathered[n] for n in rest}, small, l))
        lw.append(W)
        q, k, v, ync, yns = _branch_fwd(proj, tabs, W["bw"], tm, f"branch_fwd_{l}")
        ride = ([shards[l + 1][n] for n in big], [False] * nb) if l + 1 < DEPTH else None
        o, lse, *ahead = _attn_fwd(q, k, v, tb, f"attn_fwd_{l}", ride)
        gathered = dict(zip(big, ahead))
        acts.append((h_in, hb, proj, q, k, v, ync, yns, o, lse))
        if l + 1 < DEPTH:
            h_in = _out_fwd(h_in, o, proj, ync, yns, W["bn_m"], W["w_out"], tmm, f"out_fwd_{l}")
        else:
            dy, loss_row = _out_fwd_loss(h_in, o, proj, ync, yns, W["bn_m"], W["w_out"], tgt, tmm, f"out_fwd_{l}")
    loss = lax.psum(loss_row[0, 0], ("x", "y", "c"))

    def for_chips(n, g):
        if n == "w_in":
            blocks = [_w_in_grad_shard(*g, j) for j in range(4)]
        else:
            blocks = jnp.split(g, 4, axis=BIG[n] - 1)
        return jnp.stack(blocks).astype(bf16)

    mid = [n for n in rest if n != "w_out"]
    grads = [None] * DEPTH
    contrib = {}
    for l in reversed(range(DEPTH)):
        W = lw[l]
        xin, hb, proj, q, k, v, ync, yns, o, lse = acts[l]
        do, delta, dzm, dync, dyns, dw_out, dbn_m = _out_bwd(
            dy, o, proj, ync, yns, W["bn_m"], W["w_out"], tmm, f"out_bwd_{l}")
        keys = [(l, "w_out")] + ([(l + 1, "w_in")] if l + 1 < DEPTH else [])
        srcs = [for_chips("w_out", dw_out)] + ([for_chips("w_in", grads[l + 1]["w_in"])] if l + 1 < DEPTH else [])
        dq, dk, dv, *sent = _attn_bwd(q, k, v, do, lse, delta.reshape(HEADS, 1, T), tb, f"attn_bwd_{l}",
                                      (srcs, [True] * len(srcs)))
        contrib.update(zip(keys, sent))
        outs = _branch_bwd(proj, tabs, W["bw"], dq, dk, dv, dync, dyns, tmb, f"branch_bwd_{l}")
        dprest, gb = outs[0], dict(zip([n for n, _ in BR_W], outs[1:]))
        grads[l] = _layer_grads(gb, dw_out, dbn_m)
        dw_zm, dw_rest, *sent = _inproj_bwd_dw(
            hb, dzm, dprest, tm, f"inproj_bwd_dw_{l}", ([for_chips(n, grads[l][n]) for n in mid], [True] * len(mid)))
        contrib.update(zip([(l, n) for n in mid], sent))
        grads[l]["w_in"] = (dw_zm, dw_rest)
        ride = ([for_chips("w_in", grads[l]["w_in"])], [True]) if l == 0 else None
        dy, dnorm_g, *sent = _inproj_bwd_dx(xin, W["norm_g"], W["w_in"], dzm, dprest, dy, tmm,
                                            f"inproj_bwd_dx_{l}", ride)
        contrib.update(zip([(l, "w_in")], sent))
        grads[l]["norm_g"] = dnorm_g[0]
    grad_x = dy[None]
    g_small = [jnp.stack([grads[l][n] for l in range(DEPTH)]) for n in SMALL]
    last = _chip_exchange(g_small, [False] * len(SMALL), "exchange_grads_last")
    view = lambda a: a.reshape(-1, a.shape[-1])
    parts = [view(jnp.stack([_sum4(contrib[(l, n)], f"sum_chips_{n}_{l}") for l in range(DEPTH)])) for n in big]
    parts += list(_sum4_whole(last, "sum_chips_small"))
    others = _sibling_exchange(parts, "exchange_cores")

    res = {kind: {} for kind in ("grad", "delta", "new_m", "new_v")}
    for i, n in enumerate(big):
        outs = _adamw(*[view(d[n]) for d in (wts, mom_m, mom_v)], parts[i], others[i], f"adamw_{n}")
        for kind, o in zip(res, outs):
            res[kind][n] = o.reshape(wts[n].shape)
    outs = _adamw_whole(*[[d[n] for n in SMALL] for d in (wts, mom_m, mom_v)], parts[nb:], others[nb:],
                        "adamw_small")
    for kind, o in zip(res, outs):
        res[kind].update(zip(SMALL, o))
    out = [loss, grad_x]
    for kind in ("grad", "delta", "new_m", "new_v"):
        out += [res[kind][n] for n in WEIGHTS]
    return tuple(out)
```
